```python
import math
import jax
import jax.numpy as jnp
from jax import lax
import numpy as np

D_MODEL = 1024
BATCH = 4
SEQ = 8192
DEPTH = 2

GDN_HEADS = 4
GDN_DK = 128
GDN_DV = 128
GDN_CHUNK = 64
CONV_WIDTH = 4
MLSTM_HEADS = 4
MLSTM_DQK = 128
MLSTM_DV = 128
MLSTM_CHUNK = 64
GATE_CAP = 15.0
MLA_HEADS = 4
MLA_NOPE = 128
MLA_ROPE = 64
MLA_V = 128
MLA_Q_LORA = 384
MLA_KV_LORA = 256
ROPE_THETA = 10000.0
Q_BLOCK = 128
N_EXPERTS = 16
N_GROUPS = 4
EXPERTS_PER_GROUP = N_EXPERTS // N_GROUPS
TOP_K = 2
D_EXPERT = 512
EXPERT_BLOCK = 256
LN_EPS = 1e-5
RMS_EPS = 1e-6
DEEPNORM_ALPHA = (2 * DEPTH) ** 0.25
DEEPNORM_BETA = (8 * DEPTH) ** -0.25

GDN_QK_WIDTH = GDN_HEADS * GDN_DK
GDN_V_WIDTH = GDN_HEADS * GDN_DV
MLSTM_QK_WIDTH = MLSTM_HEADS * MLSTM_DQK
MLSTM_V_WIDTH = MLSTM_HEADS * MLSTM_DV
MLA_V_WIDTH = MLA_HEADS * MLA_V
N_BRANCHES = 3
IN_SIZES = (GDN_QK_WIDTH, GDN_QK_WIDTH, GDN_V_WIDTH, GDN_V_WIDTH, GDN_HEADS, GDN_HEADS,
            MLSTM_QK_WIDTH, MLSTM_QK_WIDTH, MLSTM_V_WIDTH, MLSTM_V_WIDTH, MLSTM_HEADS, MLSTM_HEADS,
            MLA_Q_LORA, MLA_KV_LORA, MLA_ROPE, N_BRANCHES * D_MODEL)
D_IN = (2 * GDN_QK_WIDTH + 2 * GDN_V_WIDTH + 2 * GDN_HEADS
        + 2 * MLSTM_QK_WIDTH + 2 * MLSTM_V_WIDTH + 2 * MLSTM_HEADS
        + MLA_Q_LORA + MLA_KV_LORA + MLA_ROPE + N_BRANCHES * D_MODEL)

kernel_name = 'hybrid_gdn_mlstm_mla_grouped_moe_deepnorm'


def layer_norm(x, g, b):
    xf = x.astype(jnp.float32)
    mu = jnp.mean(xf, axis=-1, keepdims=True)
    var = jnp.mean(jnp.square(xf - mu), axis=-1, keepdims=True)
    return ((xf - mu) * lax.rsqrt(var + LN_EPS) * g.astype(jnp.float32) + b.astype(jnp.float32)).astype(x.dtype)


def rms_norm(x, g):
    xf = x.astype(jnp.float32)
    y = xf * lax.rsqrt(jnp.mean(xf * xf, axis=-1, keepdims=True) + RMS_EPS)
    return (y * g.astype(jnp.float32)).astype(x.dtype)


def l2_normalize(x):
    return x * lax.rsqrt(jnp.sum(x * x, axis=-1, keepdims=True) + RMS_EPS)


def soft_cap(x):
    return GATE_CAP * jnp.tanh(x / GATE_CAP)


def split_columns(h):
    points = []
    acc = 0
    for size in IN_SIZES[:-1]:
        acc += size
        points.append(acc)
    return jnp.split(h, points, axis=-1)


def causal_depthwise_conv(x, w):
    width, ch = w.shape
    return lax.conv_general_dilated(
        x, w[:, None, :].astype(x.dtype), window_strides=(1,), padding=[(width - 1, 0)],
        dimension_numbers=('NWC', 'WIO', 'NWC'), feature_group_count=ch)


def to_chunks(x, size):
    b, s, h, d = x.shape
    return x.reshape(b, s // size, size, h, d).transpose(1, 0, 3, 2, 4)


def scalar_chunks(x, size):
    b, s, h = x.shape
    return x.reshape(b, s // size, size, h).transpose(1, 0, 3, 2)


def from_chunks(x):
    n, b, h, size, d = x.shape
    return x.transpose(1, 0, 3, 2, 4).reshape(b, n * size, h, d)


def rope_angles(positions, dim):
    inv_freq = 1.0 / (ROPE_THETA ** (jnp.arange(0, dim, 2, dtype=jnp.float32) / dim))
    ang = positions.astype(jnp.float32)[..., None] * inv_freq
    return jnp.cos(ang), jnp.sin(ang)


def apply_rope(x, cos, sin):
    x1, x2 = jnp.split(x, 2, axis=-1)
    return jnp.concatenate([x1 * cos - x2 * sin, x2 * cos + x1 * sin], axis=-1).astype(x.dtype)


def gated_deltanet(q, k, v, a, b, z, conv_w, a_log, dt_bias, norm_g):
    bsz, seq, _ = q.shape
    size = GDN_CHUNK
    qkv = jax.nn.silu(causal_depthwise_conv(jnp.concatenate([q, k, v], axis=-1), conv_w))
    q, k, v = jnp.split(qkv.astype(jnp.float32), [GDN_QK_WIDTH, 2 * GDN_QK_WIDTH], axis=-1)
    q = l2_normalize(q.reshape(bsz, seq, GDN_HEADS, GDN_DK)) * GDN_DK ** -0.5
    k = l2_normalize(k.reshape(bsz, seq, GDN_HEADS, GDN_DK))
    v = v.reshape(bsz, seq, GDN_HEADS, GDN_DV)
    g = -jnp.exp(a_log.astype(jnp.float32)) * jax.nn.softplus(a.astype(jnp.float32) + dt_bias.astype(jnp.float32))
    beta = jax.nn.sigmoid(b.astype(jnp.float32))

    qc, kc, vc = to_chunks(q, size), to_chunks(k, size), to_chunks(v, size)
    gcum = jnp.cumsum(scalar_chunks(g, size), axis=-1)
    betac = scalar_chunks(beta, size)
    idx = jnp.arange(size)
    causal = idx[:, None] >= idx[None, :]
    strict = idx[:, None] > idx[None, :]
    decay = jnp.exp(jnp.where(causal, gcum[..., :, None] - gcum[..., None, :], -jnp.inf))
    eye = jnp.eye(size, dtype=jnp.float32)
    a_mat = jnp.where(strict, jnp.einsum('nbhid,nbhjd->nbhij', kc * betac[..., None], kc) * decay, 0.0)
    t_mat = lax.linalg.triangular_solve(a_mat + eye, jnp.broadcast_to(eye, a_mat.shape),
                                        left_side=True, lower=True, unit_diagonal=True)
    u = jnp.einsum('nbhij,nbhjd->nbhid', t_mat, vc * betac[..., None])
    w = jnp.einsum('nbhij,nbhjd->nbhid', t_mat, kc * (betac * jnp.exp(gcum))[..., None])
    qk = jnp.einsum('nbhid,nbhjd->nbhij', qc, kc) * decay
    q_dec = qc * jnp.exp(gcum)[..., None]
    k_tail = kc * jnp.exp(gcum[..., -1:] - gcum)[..., None]
    g_last = jnp.exp(gcum[..., -1])

    def step(state, inp):
        u_c, w_c, qd_c, qk_c, kt_c, gl_c = inp
        v_new = u_c - jnp.einsum('bhld,bhde->bhle', w_c, state)
        out = jnp.einsum('bhld,bhde->bhle', qd_c, state) + jnp.einsum('bhij,bhje->bhie', qk_c, v_new)
        state = state * gl_c[..., None, None] + jnp.einsum('bhld,bhle->bhde', kt_c, v_new)
        return state, out

    s0 = jnp.zeros((bsz, GDN_HEADS, GDN_DK, GDN_DV), jnp.float32)
    _, o = lax.scan(step, s0, (u, w, q_dec, qk, k_tail, g_last))
    o = from_chunks(o)
    o = rms_norm(o, norm_g) * jax.nn.silu(z.astype(jnp.float32).reshape(bsz, seq, GDN_HEADS, GDN_DV))
    return o.reshape(bsz, seq, GDN_V_WIDTH).astype(z.dtype)


def mlstm(q, k, v, o_pre, i_pre, f_pre, gate_bias, norm_g):
    bsz, seq, _ = q.shape
    size = MLSTM_CHUNK
    q = q.astype(jnp.float32).reshape(bsz, seq, MLSTM_HEADS, MLSTM_DQK)
    k = k.astype(jnp.float32).reshape(bsz, seq, MLSTM_HEADS, MLSTM_DQK) * MLSTM_DQK ** -0.5
    v = v.astype(jnp.float32).reshape(bsz, seq, MLSTM_HEADS, MLSTM_DV)
    gb = gate_bias.astype(jnp.float32)
    i_log = soft_cap(i_pre.astype(jnp.float32) + gb[:MLSTM_HEADS])
    f_log = jax.nn.log_sigmoid(soft_cap(f_pre.astype(jnp.float32) + gb[MLSTM_HEADS:]))

    qc, kc, vc = to_chunks(q, size), to_chunks(k, size), to_chunks(v, size)
    ic = scalar_chunks(i_log, size)
    bcum = jnp.cumsum(scalar_chunks(f_log, size), axis=-1)
    idx = jnp.arange(size)
    causal = idx[:, None] >= idx[None, :]
    log_d = jnp.where(causal, bcum[..., :, None] - bcum[..., None, :] + ic[..., None, :], -jnp.inf)
    qk = jnp.einsum('nbhid,nbhjd->nbhij', qc, kc)
    log_kw = bcum[..., -1:] - bcum + ic
    b_last = bcum[..., -1]

    def step(carry, inp):
        c_st, n_st, m_st = carry
        q_c, k_c, v_c, ld_c, qk_c, b_c, lkw_c, bl_c = inp
        log_inter = b_c + m_st[..., None]
        m_t = jnp.maximum(log_inter, jnp.max(ld_c, axis=-1))
        w_inter = jnp.exp(log_inter - m_t)
        s = qk_c * jnp.exp(ld_c - m_t[..., None])
        num = w_inter[..., None] * jnp.einsum('bhld,bhde->bhle', q_c, c_st) + jnp.einsum('bhij,bhje->bhie', s, v_c)
        den = w_inter * jnp.einsum('bhld,bhd->bhl', q_c, n_st) + jnp.sum(s, axis=-1)
        h = num / jnp.maximum(jnp.abs(den), jnp.exp(-m_t))[..., None]
        m_new = jnp.maximum(bl_c + m_st, jnp.max(lkw_c, axis=-1))
        carry_decay = jnp.exp(bl_c + m_st - m_new)
        kw = jnp.exp(lkw_c - m_new[..., None])
        c_st = carry_decay[..., None, None] * c_st + jnp.einsum('bhld,bhle->bhde', k_c * kw[..., None], v_c)
        n_st = carry_decay[..., None] * n_st + jnp.einsum('bhl,bhld->bhd', kw, k_c)
        return (c_st, n_st, m_new), h

    init = (jnp.zeros((bsz, MLSTM_HEADS, MLSTM_DQK, MLSTM_DV), jnp.float32),
            jnp.zeros((bsz, MLSTM_HEADS, MLSTM_DQK), jnp.float32),
            jnp.zeros((bsz, MLSTM_HEADS), jnp.float32))
    _, h = lax.scan(step, init, (qc, kc, vc, log_d, qk, bcum, log_kw, b_last))
    h = rms_norm(from_chunks(h), norm_g.reshape(MLSTM_HEADS, MLSTM_DV))
    h = jax.nn.sigmoid(o_pre.astype(jnp.float32)).reshape(bsz, seq, MLSTM_HEADS, MLSTM_DV) * h
    return h.reshape(bsz, seq, MLSTM_V_WIDTH).astype(o_pre.dtype)


def latent_attention(c_q, c_kv, k_rope, positions, q_norm_g, kv_norm_g, w_uq, w_ukv):
    bsz, seq, _ = c_q.shape
    q = (rms_norm(c_q, q_norm_g) @ w_uq).reshape(bsz, seq, MLA_HEADS, MLA_NOPE + MLA_ROPE)
    kv = (rms_norm(c_kv, kv_norm_g) @ w_ukv).reshape(bsz, seq, MLA_HEADS, MLA_NOPE + MLA_V)
    q_nope, q_rope = jnp.split(q, [MLA_NOPE], axis=-1)
    k_nope, v = jnp.split(kv, [MLA_NOPE], axis=-1)
    cos, sin = rope_angles(positions, MLA_ROPE)
    q_rope = apply_rope(q_rope, cos[:, :, None, :], sin[:, :, None, :])
    k_rope = apply_rope(k_rope, cos, sin)
    scale = (MLA_NOPE + MLA_ROPE) ** -0.5
    n_blk = seq // Q_BLOCK

    def blocks(t):
        return t.reshape(bsz, n_blk, Q_BLOCK, MLA_HEADS, t.shape[-1]).transpose(1, 0, 2, 3, 4)

    key_idx = jnp.arange(seq)

    def attend(args):
        qn, qr, blk = args
        s = jnp.einsum('bqhd,bkhd->bhqk', qn, k_nope) + jnp.einsum('bqhd,bkd->bhqk', qr, k_rope)
        q_idx = blk * Q_BLOCK + jnp.arange(Q_BLOCK)
        s = jnp.where(q_idx[:, None] >= key_idx[None, :], s.astype(jnp.float32) * scale, -jnp.inf)
        p = jax.nn.softmax(s, axis=-1).astype(v.dtype)
        return jnp.einsum('bhqk,bkhd->bqhd', p, v)

    out = lax.map(attend, (blocks(q_nope), blocks(q_rope), jnp.arange(n_blk)))
    return out.transpose(1, 0, 2, 3, 4).reshape(bsz, seq, MLA_V_WIDTH)


def token_mixer(x, positions, w_in, gdn_conv, gdn_a_log, gdn_dt_bias, gdn_norm, mlstm_gate_bias, mlstm_norm,
                mla_q_norm, mla_kv_norm, mla_w_uq, mla_w_ukv, w_br_gdn, w_br_mlstm, w_br_mla, gate_bias, w_out):
    bsz, seq, _ = x.shape
    (g_q, g_k, g_v, g_z, g_a, g_b, m_q, m_k, m_v, m_o, m_i, m_f,
     c_q, c_kv, k_rope, gate_logits) = split_columns(x @ w_in)
    y_gdn = gated_deltanet(g_q, g_k, g_v, g_a, g_b, g_z, gdn_conv, gdn_a_log, gdn_dt_bias, gdn_norm)
    y_mlstm = mlstm(m_q, m_k, m_v, m_o, m_i, m_f, mlstm_gate_bias, mlstm_norm)
    y_mla = latent_attention(c_q, c_kv, k_rope, positions, mla_q_norm, mla_kv_norm, mla_w_uq, mla_w_ukv)
    gates = jax.nn.sigmoid((gate_logits + gate_bias).astype(jnp.float32)).astype(x.dtype)
    gates = gates.reshape(bsz, seq, N_BRANCHES, D_MODEL)
    merged = (gates[:, :, 0] * (y_gdn @ w_br_gdn)
              + gates[:, :, 1] * (y_mlstm @ w_br_mlstm)
              + gates[:, :, 2] * (y_mla @ w_br_mla))
    return merged @ w_out


def route(x_flat, router_w, router_bias):
    n_tok = x_flat.shape[0]
    scores = jax.nn.sigmoid((x_flat @ router_w).astype(jnp.float32))
    biased = (scores + router_bias.astype(jnp.float32)).reshape(n_tok, N_GROUPS, EXPERTS_PER_GROUP)
    group_score = jnp.sum(lax.top_k(biased, TOP_K)[0], axis=-1)
    group = jnp.argmax(group_score, axis=-1).astype(jnp.int32)
    in_group = biased[jnp.arange(n_tok), group]
    _, local = lax.top_k(in_group, TOP_K)
    expert = group[:, None] * EXPERTS_PER_GROUP + local.astype(jnp.int32)
    weight = jnp.take_along_axis(scores, expert, axis=1)
    weight = weight / jnp.sum(weight, axis=-1, keepdims=True)
    return expert, weight


def routed_experts(x, router_w, router_bias, w1, w3, w2):
    bsz, seq, d = x.shape
    n_tok = bsz * seq
    n_assign = n_tok * TOP_K
    x_flat = x.reshape(n_tok, d)
    expert, weight = route(x_flat, router_w, router_bias)
    e_flat = expert.reshape(n_assign)
    tok_flat = jnp.repeat(jnp.arange(n_tok, dtype=jnp.int32), TOP_K)
    order = jnp.argsort(e_flat)
    e_sorted = e_flat[order]
    tok_sorted = tok_flat[order]
    w_sorted = weight.reshape(n_assign)[order]
    counts = jnp.bincount(e_flat, length=N_EXPERTS)
    starts = jnp.cumsum(counts) - counts
    padded = (counts + EXPERT_BLOCK - 1) // EXPERT_BLOCK * EXPERT_BLOCK
    padded_ends = jnp.cumsum(padded)
    padded_starts = padded_ends - padded
    dest = padded_starts[e_sorted] + jnp.arange(n_assign, dtype=jnp.int32) - starts[e_sorted]
    n_rows = n_assign + N_EXPERTS * EXPERT_BLOCK
    n_blocks = n_rows // EXPERT_BLOCK
    row_tok = jnp.zeros((n_rows,), jnp.int32).at[dest].set(tok_sorted)
    row_w = jnp.zeros((n_rows,), jnp.float32).at[dest].set(w_sorted)
    block_start = jnp.arange(n_blocks, dtype=jnp.int32) * EXPERT_BLOCK
    block_expert = jnp.minimum(jnp.searchsorted(padded_ends, block_start, side='right'), N_EXPERTS - 1)
    x_rows = x_flat[row_tok].reshape(n_blocks, EXPERT_BLOCK, d)

    def expert_ffn(args):
        xb, e = args
        return (jax.nn.silu(xb @ w1[e]) * (xb @ w3[e])) @ w2[e]

    y_rows = lax.map(expert_ffn, (x_rows, block_expert)).reshape(n_rows, d)
    out = jnp.zeros((n_tok, d), x.dtype).at[row_tok].add((y_rows * row_w[:, None]).astype(x.dtype))
    return out.reshape(bsz, seq, d)


def setup_inputs(seed: int = 0) -> dict:
    key = jax.random.key(seed)
    ks = jax.random.split(key, 32)
    f32 = jnp.float32

    def nrm(k, shape, scale):
        return jax.random.normal(k, shape, f32) * scale

    x = nrm(ks[0], (BATCH, SEQ, D_MODEL), 1.0)
    offsets = jax.random.randint(ks[1], (BATCH, 1), 0, SEQ, dtype=jnp.int32)
    positions = offsets + jnp.arange(SEQ, dtype=jnp.int32)[None, :]
    ln_in_g = 1.0 + nrm(ks[2], (D_MODEL,), 0.02)
    ln_in_b = nrm(ks[3], (D_MODEL,), 0.02)
    w_in = nrm(ks[4], (DEPTH, D_MODEL, D_IN), D_MODEL ** -0.5)
    gdn_conv = nrm(ks[5], (DEPTH, CONV_WIDTH, 2 * GDN_QK_WIDTH + GDN_V_WIDTH), CONV_WIDTH ** -0.5)
    gdn_a_log = jnp.log(jax.random.uniform(ks[6], (DEPTH, GDN_HEADS), f32, 1.0, 16.0))
    dt = jnp.exp(jax.random.uniform(ks[7], (DEPTH, GDN_HEADS), f32, math.log(1e-3), math.log(1e-1)))
    gdn_dt_bias = dt + jnp.log(-jnp.expm1(-dt))
    gdn_norm = 1.0 + nrm(ks[8], (DEPTH, GDN_DV), 0.02)
    mlstm_gate_bias = jnp.concatenate([nrm(ks[9], (DEPTH, MLSTM_HEADS), 0.1),
                                       3.0 + nrm(ks[10], (DEPTH, MLSTM_HEADS), 0.5)], axis=-1)
    mlstm_norm = 1.0 + nrm(ks[11], (DEPTH, MLSTM_V_WIDTH), 0.02)
    mla_q_norm = 1.0 + nrm(ks[12], (DEPTH, MLA_Q_LORA), 0.02)
    mla_kv_norm = 1.0 + nrm(ks[13], (DEPTH, MLA_KV_LORA), 0.02)
    mla_w_uq = nrm(ks[14], (DEPTH, MLA_Q_LORA, MLA_HEADS * (MLA_NOPE + MLA_ROPE)), MLA_Q_LORA ** -0.5)
    mla_w_ukv = nrm(ks[15], (DEPTH, MLA_KV_LORA, MLA_HEADS * (MLA_NOPE + MLA_V)), MLA_KV_LORA ** -0.5)
    w_br_gdn = nrm(ks[16], (DEPTH, GDN_V_WIDTH, D_MODEL), GDN_V_WIDTH ** -0.5)
    w_br_mlstm = nrm(ks[17], (DEPTH, MLSTM_V_WIDTH, D_MODEL), MLSTM_V_WIDTH ** -0.5)
    w_br_mla = nrm(ks[18], (DEPTH, MLA_V_WIDTH, D_MODEL), MLA_V_WIDTH ** -0.5)
    gate_bias = nrm(ks[19], (DEPTH, N_BRANCHES * D_MODEL), 0.02)
    w_out = nrm(ks[20], (DEPTH, D_MODEL, D_MODEL), DEEPNORM_BETA * D_MODEL ** -0.5)
    ln1_g = 1.0 + nrm(ks[21], (DEPTH, D_MODEL), 0.02)
    ln1_b = nrm(ks[22], (DEPTH, D_MODEL), 0.02)
    router_w = nrm(ks[23], (D_MODEL, N_EXPERTS), D_MODEL ** -0.5)
    router_bias = nrm(ks[24], (N_EXPERTS,), 0.01)
    moe_w1 = nrm(ks[25], (DEPTH, N_EXPERTS, D_MODEL, D_EXPERT), D_MODEL ** -0.5)
    moe_w3 = nrm(ks[26], (DEPTH, N_EXPERTS, D_MODEL, D_EXPERT), D_MODEL ** -0.5)
    moe_w2 = nrm(ks[27], (DEPTH, N_EXPERTS, D_EXPERT, D_MODEL), DEEPNORM_BETA * D_EXPERT ** -0.5)
    ln2_g = 1.0 + nrm(ks[28], (DEPTH, D_MODEL), 0.02)
    ln2_b = nrm(ks[29], (DEPTH, D_MODEL), 0.02)
    return {'x': x, 'positions': positions, 'ln_in_g': ln_in_g, 'ln_in_b': ln_in_b, 'w_in': w_in,
            'gdn_conv': gdn_conv, 'gdn_a_log': gdn_a_log, 'gdn_dt_bias': gdn_dt_bias, 'gdn_norm': gdn_norm,
            'mlstm_gate_bias': mlstm_gate_bias, 'mlstm_norm': mlstm_norm,
            'mla_q_norm': mla_q_norm, 'mla_kv_norm': mla_kv_norm, 'mla_w_uq': mla_w_uq, 'mla_w_ukv': mla_w_ukv,
            'w_br_gdn': w_br_gdn, 'w_br_mlstm': w_br_mlstm, 'w_br_mla': w_br_mla, 'gate_bias': gate_bias,
            'w_out': w_out, 'ln1_g': ln1_g, 'ln1_b': ln1_b, 'router_w': router_w, 'router_bias': router_bias,
            'moe_w1': moe_w1, 'moe_w3': moe_w3, 'moe_w2': moe_w2, 'ln2_g': ln2_g, 'ln2_b': ln2_b}


def reference(x, positions, ln_in_g, ln_in_b, w_in, gdn_conv, gdn_a_log, gdn_dt_bias, gdn_norm,
              mlstm_gate_bias, mlstm_norm, mla_q_norm, mla_kv_norm, mla_w_uq, mla_w_ukv,
              w_br_gdn, w_br_mlstm, w_br_mla, gate_bias, w_out, ln1_g, ln1_b, router_w, router_bias,
              moe_w1, moe_w3, moe_w2, ln2_g, ln2_b):
    x = layer_norm(x, ln_in_g, ln_in_b)
    for l in range(DEPTH):
        mix = token_mixer(x, positions, w_in[l], gdn_conv[l], gdn_a_log[l], gdn_dt_bias[l], gdn_norm[l],
                          mlstm_gate_bias[l], mlstm_norm[l], mla_q_norm[l], mla_kv_norm[l],
                          mla_w_uq[l], mla_w_ukv[l], w_br_gdn[l], w_br_mlstm[l], w_br_mla[l],
                          gate_bias[l], w_out[l])
        x = layer_norm(DEEPNORM_ALPHA * x + mix, ln1_g[l], ln1_b[l])
        ffn = routed_experts(x, router_w, router_bias, moe_w1[l], moe_w3[l], moe_w2[l])
        x = layer_norm(DEEPNORM_ALPHA * x + ffn, ln2_g[l], ln2_b[l])
    return x
```

```python
import functools

import jax
import jax.numpy as jnp
from jax import lax
from jax.experimental import pallas as pl
from jax.experimental.pallas import tpu as pltpu

F32 = jnp.float32
BF16 = jnp.bfloat16

D_MODEL = 1024
DEPTH = 2
HEADS = 4
HEAD_DIM = 128
CHUNK = 64
CONV_WIDTH = 4
GATE_CAP = 15.0
MLA_ROPE = 64
MLA_Q_LORA = 384
MLA_KV_LORA = 256
ROPE_THETA = 10000.0
N_EXPERTS = 16
N_GROUPS = 4
EXPERTS_PER_GROUP = 4
TOP_K = 2
D_EXPERT = 512
EXPERT_BLOCK = 256
LN_EPS = 1e-5
RMS_EPS = 1e-6
DEEPNORM_ALPHA = (2 * DEPTH) ** 0.25

LANES = 128
QK_WIDTH = HEADS * HEAD_DIM
H_WIDTH = 8192
COL_MLA = 0
COL_GDN = 8
COL_MLSTM = 24
COL_GATE = 5
SEQ_BLOCK = 512
VMEM_LIMIT = 48 * 1024 * 1024


def _cparams(sem):
    return pltpu.CompilerParams(dimension_semantics=sem, vmem_limit_bytes=VMEM_LIMIT)


def _sigmoid(x):
    return 1.0 / (1.0 + jnp.exp(-x))


def _layer_norm(x, g, b):
    mu = jnp.mean(x, axis=-1, keepdims=True)
    xc = x - mu
    var = jnp.mean(xc * xc, axis=-1, keepdims=True)
    return xc * lax.rsqrt(var + LN_EPS) * g + b


def _dot(a, b):
    return jnp.dot(a, b, preferred_element_type=F32)


def _dot_nt(a, b, precision=None):
    return lax.dot_general(a, b, (((1,), (1,)), ((), ())), preferred_element_type=F32,
                           precision=precision)


def _dot_tn(a, b, precision=None):
    return lax.dot_general(a, b, (((0,), (0,)), ((), ())), preferred_element_type=F32,
                           precision=precision)


def _ln_body(x_ref, g_ref, b_ref, of_ref, ob_ref):
    y = _layer_norm(x_ref[...], g_ref[...], b_ref[...])
    of_ref[...] = y
    ob_ref[...] = y.astype(BF16)


def layer_norm_entry(x, g, b, tm=512):
    t, d = x.shape
    return pl.pallas_call(
        _ln_body,
        grid=(t // tm,),
        in_specs=[pl.BlockSpec((tm, d), lambda i: (i, 0)),
                  pl.BlockSpec((1, d), lambda i: (0, 0)),
                  pl.BlockSpec((1, d), lambda i: (0, 0))],
        out_specs=[pl.BlockSpec((tm, d), lambda i: (i, 0)),
                   pl.BlockSpec((tm, d), lambda i: (i, 0))],
        out_shape=[jax.ShapeDtypeStruct((t, d), F32), jax.ShapeDtypeStruct((t, d), BF16)],
        compiler_params=_cparams(("parallel",)),
        name="ln_entry",
    )(x, g.reshape(1, d), b.reshape(1, d))


def _inproj_body(x_ref, w_ref, ws_ref, h_ref, hs_ref):
    x = x_ref[...]
    h_ref[...] = _dot(x, w_ref[...]).astype(BF16)

    @pl.when(pl.program_id(1) == 0)
    def _():
        hs_ref[...] = _dot(x, ws_ref[...])


def in_projection(xb, w_main, w_small, tm=1024, tn=512):
    t, d = xb.shape
    n = w_main.shape[1]
    return pl.pallas_call(
        _inproj_body,
        grid=(t // tm, n // tn),
        in_specs=[pl.BlockSpec((tm, d), lambda i, j: (i, 0)),
                  pl.BlockSpec((d, tn), lambda i, j: (0, j)),
                  pl.BlockSpec((d, LANES), lambda i, j: (0, 0))],
        out_specs=[pl.BlockSpec((tm, tn), lambda i, j: (i, j)),
                   pl.BlockSpec((tm, LANES), lambda i, j: (i, 0))],
        out_shape=[jax.ShapeDtypeStruct((t, n), BF16), jax.ShapeDtypeStruct((t, LANES), F32)],
        compiler_params=_cparams(("parallel", "arbitrary")),
        name="in_proj",
    )(xb, w_main, w_small)


def _lane_pick(x, lane):
    idx = lax.broadcasted_iota(jnp.int32, x.shape, 1)
    return jnp.sum(jnp.where(idx == lane, x, 0.0), axis=1, keepdims=True)


def _softplus(x):
    return jnp.maximum(x, 0.0) + jnp.log1p(jnp.exp(-jnp.abs(x)))


def _tri(n, kind):
    r = lax.broadcasted_iota(jnp.int32, (n, n), 0)
    c = lax.broadcasted_iota(jnp.int32, (n, n), 1)
    if kind == "lower":
        return r >= c
    if kind == "strict":
        return r > c
    return r <= c


def _chunk_cumsum_col(col):
    low = jnp.where(_tri(CHUNK, "lower"), 1.0, 0.0).astype(F32)
    wide = jnp.broadcast_to(col, (CHUNK, LANES))
    return jnp.dot(low, wide, preferred_element_type=F32, precision=lax.Precision.HIGHEST)[:, :1]


def _chunk_cumsum_rows(rows):
    up = jnp.where(_tri(CHUNK, "upper"), 1.0, 0.0).astype(F32)
    return jnp.dot(rows, up, preferred_element_type=F32, precision=lax.Precision.HIGHEST)


def _rms_norm(x, g):
    return x * lax.rsqrt(jnp.mean(x * x, axis=-1, keepdims=True) + RMS_EPS) * g


def _gdn_body(alog_ref, dtb_ref, q_ref, k_ref, v_ref, z_ref, cq_ref, ck_ref, cv_ref,
              sm_ref, smt_ref, ng_ref, o_ref, state_ref, eq_ref, ek_ref, ev_ref):
    hh = pl.program_id(1)
    lb = q_ref.shape[0]

    @pl.when(pl.program_id(2) == 0)
    def _():
        state_ref[...] = jnp.zeros_like(state_ref)
        for e_ref in (eq_ref, ek_ref, ev_ref):
            e_ref[0:8, :] = jnp.zeros((8, HEAD_DIM), F32)

    def conv_silu(x_ref, w_ref, e_ref):
        e_ref[8:, :] = x_ref[...].astype(F32)
        w = w_ref[...]
        y = w[0:1, :] * e_ref[pl.ds(8 - CONV_WIDTH + 1, lb), :]
        for j in range(1, CONV_WIDTH):
            y = y + w[j:j + 1, :] * e_ref[pl.ds(8 - CONV_WIDTH + 1 + j, lb), :]
        e_ref[0:8, :] = e_ref[lb:lb + 8, :]
        return y * _sigmoid(y)

    q = conv_silu(q_ref, cq_ref, eq_ref)
    k = conv_silu(k_ref, ck_ref, ek_ref)
    v = conv_silu(v_ref, cv_ref, ev_ref)
    q = q * lax.rsqrt(jnp.sum(q * q, axis=-1, keepdims=True) + RMS_EPS) * (HEAD_DIM ** -0.5)
    k = k * lax.rsqrt(jnp.sum(k * k, axis=-1, keepdims=True) + RMS_EPS)

    neg_a = -jnp.exp(jnp.full((1, 1), alog_ref[hh], F32))
    dtb = dtb_ref[hh]
    sm = sm_ref[...]
    g_col = neg_a * _softplus(_lane_pick(sm, hh) + dtb)
    beta_col = _sigmoid(_lane_pick(sm, HEADS + hh))
    g_rows = neg_a * _softplus(smt_ref[hh] + dtb)
    gc_rows = _chunk_cumsum_rows(g_rows)

    causal = _tri(CHUNK, "lower")
    strict = _tri(CHUNK, "strict")
    eye = jnp.where(lax.broadcasted_iota(jnp.int32, (CHUNK, CHUNK), 0)
                    == lax.broadcasted_iota(jnp.int32, (CHUNK, CHUNK), 1), 1.0, 0.0).astype(F32)
    hi = lax.Precision.HIGHEST
    state = state_ref[...]
    z = z_ref[...].astype(F32)
    ng = ng_ref[...]
    for c in range(lb // CHUNK):
        sl = slice(c * CHUNK, (c + 1) * CHUNK)
        qc, kc, vc, bc = q[sl], k[sl], v[sl], beta_col[sl]
        gc_col = _chunk_cumsum_col(g_col[sl])
        gc_row = gc_rows[c:c + 1, :]
        decay = jnp.exp(jnp.where(causal, gc_col - gc_row, -jnp.inf))
        kb = kc * bc
        a_mat = jnp.where(strict, _dot_nt(kb, kc, hi) * decay, 0.0)
        pw = -a_mat
        t_mat = eye + pw
        for _ in range(5):
            pw = jnp.dot(pw, pw, preferred_element_type=F32, precision=hi)
            t_mat = t_mat + jnp.dot(t_mat, pw, preferred_element_type=F32, precision=hi)
        t_b = t_mat.astype(BF16)
        e_gc = jnp.exp(gc_col)
        u = _dot(t_b, (vc * bc).astype(BF16))
        w = _dot(t_b, (kb * e_gc).astype(BF16))
        qk = _dot_nt(qc.astype(BF16), kc.astype(BF16)) * decay
        gc_last = gc_col[CHUNK - 1:CHUNK, :]
        k_tail = kc * jnp.exp(gc_last - gc_col)
        state_b = state.astype(BF16)
        v_new = u - _dot(w.astype(BF16), state_b)
        out = _dot((qc * e_gc).astype(BF16), state_b) + _dot(qk.astype(BF16), v_new.astype(BF16))
        state = state * jnp.exp(gc_last) + _dot_tn(k_tail.astype(BF16), v_new.astype(BF16))
        zc = z[sl]
        o_ref[sl, :] = (_rms_norm(out, ng) * (zc * _sigmoid(zc))).astype(o_ref.dtype)
    state_ref[...] = state


def gated_deltanet(h, small, small_t, conv_w, a_log, dt_bias, norm_g, bsz, seq):
    t = h.shape[0]
    lb = SEQ_BLOCK
    nb = seq // lb
    nch = lb // CHUNK

    def col(off):
        return pl.BlockSpec((lb, HEAD_DIM), lambda b, hh, s, *_: (b * nb + s, COL_GDN + off + hh))

    def conv(off):
        return pl.BlockSpec((CONV_WIDTH, HEAD_DIM), lambda b, hh, s, *_: (0, off + hh))

    grid_spec = pltpu.PrefetchScalarGridSpec(
        num_scalar_prefetch=2,
        grid=(bsz, HEADS, nb),
        in_specs=[col(0), col(HEADS), col(2 * HEADS), col(3 * HEADS),
                  conv(0), conv(HEADS), conv(2 * HEADS),
                  pl.BlockSpec((lb, LANES), lambda b, hh, s, *_: (b * nb + s, 0)),
                  pl.BlockSpec((16, nch, CHUNK), lambda b, hh, s, *_: (0, b * nb + s, 0)),
                  pl.BlockSpec((1, HEAD_DIM), lambda b, hh, s, *_: (0, 0))],
        out_specs=pl.BlockSpec((lb, HEAD_DIM), lambda b, hh, s, *_: (b * nb + s, hh)),
        scratch_shapes=[pltpu.VMEM((HEAD_DIM, HEAD_DIM), F32),
                        pltpu.VMEM((lb + 8, HEAD_DIM), F32),
                        pltpu.VMEM((lb + 8, HEAD_DIM), F32),
                        pltpu.VMEM((lb + 8, HEAD_DIM), F32)])
    return pl.pallas_call(
        _gdn_body,
        grid_spec=grid_spec,
        out_shape=jax.ShapeDtypeStruct((t, QK_WIDTH), BF16),
        compiler_params=_cparams(("parallel", "parallel", "arbitrary")),
        name="gated_deltanet",
    )(a_log, dt_bias, h, h, h, h, conv_w, conv_w, conv_w, small, small_t, norm_g.reshape(1, HEAD_DIM))


def _soft_cap(x):
    return GATE_CAP * jnp.tanh(x / GATE_CAP)


def _log_sigmoid(x):
    return jnp.minimum(x, 0.0) - jnp.log1p(jnp.exp(-jnp.abs(x)))


def _mlstm_body(gb_ref, q_ref, k_ref, v_ref, o_ref, sm_ref, smt_ref, ng_ref,
                y_ref, c_ref, n_ref, m_ref):
    hh = pl.program_id(1)
    lb = q_ref.shape[0]

    @pl.when(pl.program_id(2) == 0)
    def _():
        c_ref[...] = jnp.zeros_like(c_ref)
        n_ref[...] = jnp.zeros_like(n_ref)
        m_ref[...] = jnp.zeros_like(m_ref)

    q = q_ref[...]
    k = k_ref[...].astype(F32) * (HEAD_DIM ** -0.5)
    v = v_ref[...]
    gb_i = gb_ref[hh]
    gb_f = gb_ref[HEADS + hh]
    sm = sm_ref[...]
    i_col = _soft_cap(_lane_pick(sm, 2 * HEADS + hh) + gb_i)
    f_col = _log_sigmoid(_soft_cap(_lane_pick(sm, 3 * HEADS + hh) + gb_f))
    i_rows = _soft_cap(smt_ref[2 * HEADS + hh] + gb_i)
    f_rows = _log_sigmoid(_soft_cap(smt_ref[3 * HEADS + hh] + gb_f))
    bc_rows = _chunk_cumsum_rows(f_rows)

    causal = _tri(CHUNK, "lower")
    c_st = c_ref[...]
    n_st = n_ref[...]
    m_st = m_ref[...][:, :1]
    o_pre = o_ref[...].astype(F32)
    ng = ng_ref[...]
    for c in range(lb // CHUNK):
        sl = slice(c * CHUNK, (c + 1) * CHUNK)
        qc, kc, vc = q[sl], k[sl], v[sl]
        qf = qc.astype(F32)
        bc_col = _chunk_cumsum_col(f_col[sl])
        bc_row = bc_rows[c:c + 1, :]
        log_d = jnp.where(causal, bc_col - bc_row + i_rows[c:c + 1, :], -jnp.inf)
        qk = _dot_nt(qc, kc.astype(BF16))
        log_inter = bc_col + m_st
        m_t = jnp.maximum(log_inter, jnp.max(log_d, axis=-1, keepdims=True))
        w_inter = jnp.exp(log_inter - m_t)
        s = qk * jnp.exp(log_d - m_t)
        num = w_inter * _dot(qc, c_st.astype(BF16)) + _dot(s.astype(BF16), vc)
        den = w_inter * jnp.sum(qf * n_st, axis=-1, keepdims=True) + jnp.sum(s, axis=-1, keepdims=True)
        hid = num / jnp.maximum(jnp.abs(den), jnp.exp(-m_t))
        b_last = bc_col[CHUNK - 1:CHUNK, :]
        lkw = b_last - bc_col + i_col[sl]
        m_new = jnp.maximum(b_last + m_st, jnp.max(lkw, axis=0, keepdims=True))
        carry_decay = jnp.exp(b_last + m_st - m_new)
        kw = kc * jnp.exp(lkw - m_new)
        c_st = carry_decay * c_st + _dot_tn(kw.astype(BF16), vc)
        n_st = carry_decay * n_st + jnp.sum(kw, axis=0, keepdims=True)
        m_st = m_new
        y_ref[sl, :] = (_sigmoid(o_pre[sl]) * _rms_norm(hid, ng)).astype(y_ref.dtype)
    c_ref[...] = c_st
    n_ref[...] = n_st
    m_ref[...] = jnp.broadcast_to(m_st, m_ref.shape)


def mlstm(h, small, small_t, gate_bias, norm_g, bsz, seq):
    t = h.shape[0]
    lb = SEQ_BLOCK
    nb = seq // lb
    nch = lb // CHUNK

    def col(off):
        return pl.BlockSpec((lb, HEAD_DIM), lambda b, hh, s, *_: (b * nb + s, COL_MLSTM + off + hh))

    grid_spec = pltpu.PrefetchScalarGridSpec(
        num_scalar_prefetch=1,
        grid=(bsz, HEADS, nb),
        in_specs=[col(0), col(HEADS), col(2 * HEADS), col(3 * HEADS),
                  pl.BlockSpec((lb, LANES), lambda b, hh, s, *_: (b * nb + s, 0)),
                  pl.BlockSpec((16, nch, CHUNK), lambda b, hh, s, *_: (0, b * nb + s, 0)),
                  pl.BlockSpec((1, HEAD_DIM), lambda b, hh, s, *_: (0, hh))],
        out_specs=pl.BlockSpec((lb, HEAD_DIM), lambda b, hh, s, *_: (b * nb + s, hh)),
        scratch_shapes=[pltpu.VMEM((HEAD_DIM, HEAD_DIM), F32),
                        pltpu.VMEM((1, HEAD_DIM), F32),
                        pltpu.VMEM((1, LANES), F32)])
    return pl.pallas_call(
        _mlstm_body,
        grid_spec=grid_spec,
        out_shape=jax.ShapeDtypeStruct((t, QK_WIDTH), BF16),
        compiler_params=_cparams(("parallel", "parallel", "arbitrary")),
        name="mlstm",
    )(gate_bias, h, h, h, h, small, small_t, norm_g.reshape(1, QK_WIDTH))


def _rope_table_body(pos_ref, freq_ref, sign_ref, cc_ref, ss_ref):
    ang = pos_ref[...] * freq_ref[...]
    sign = sign_ref[...]
    cc_ref[...] = jnp.cos(ang) * jnp.abs(sign)
    ss_ref[...] = jnp.sin(ang) * sign


def rope_tables(positions, tm=512):
    t = positions.size
    half = MLA_ROPE // 2
    inv_freq = 1.0 / (ROPE_THETA ** (jnp.arange(0, MLA_ROPE, 2, dtype=F32) / MLA_ROPE))
    zeros = jnp.zeros((LANES - MLA_ROPE,), F32)
    freq = jnp.concatenate([inv_freq, inv_freq, zeros]).reshape(1, LANES)
    sign = jnp.concatenate([-jnp.ones((half,), F32), jnp.ones((half,), F32), zeros]).reshape(1, LANES)
    return pl.pallas_call(
        _rope_table_body,
        grid=(t // tm,),
        in_specs=[pl.BlockSpec((tm, 1), lambda i: (i, 0)),
                  pl.BlockSpec((1, LANES), lambda i: (0, 0)),
                  pl.BlockSpec((1, LANES), lambda i: (0, 0))],
        out_specs=[pl.BlockSpec((tm, LANES), lambda i: (i, 0)),
                   pl.BlockSpec((tm, LANES), lambda i: (i, 0))],
        out_shape=[jax.ShapeDtypeStruct((t, LANES), F32), jax.ShapeDtypeStruct((t, LANES), F32)],
        compiler_params=_cparams(("parallel",)),
        name="rope_tables",
    )(positions.astype(F32).reshape(t, 1), freq, sign)


def _mla_pre_body(h_ref, cc_ref, ss_ref, qg_ref, kvg_ref, wqa_ref, wqb_ref, wkv_ref,
                  q_ref, kn_ref, kr_ref, v_ref):
    hblk = h_ref[...].astype(F32)
    cc = cc_ref[...]
    ss = ss_ref[...]
    cq = _rms_norm(hblk[:, :MLA_Q_LORA], qg_ref[...]).astype(BF16)
    ckv = _rms_norm(hblk[:, MLA_Q_LORA:MLA_Q_LORA + MLA_KV_LORA], kvg_ref[...]).astype(BF16)
    off = MLA_Q_LORA + MLA_KV_LORA
    kr_ref[...] = (hblk[:, off:off + LANES] * cc + hblk[:, off + LANES:off + 2 * LANES] * ss).astype(BF16)
    kv = _dot(ckv, wkv_ref[...])
    kn_ref[...] = kv[:, :QK_WIDTH].astype(BF16)
    v_ref[...] = kv[:, QK_WIDTH:].astype(BF16)
    qa = _dot(cq, wqa_ref[...])
    qb = _dot(cq, wqb_ref[...])
    scale = (HEAD_DIM + MLA_ROPE) ** -0.5
    for hh in range(HEADS):
        base = 2 * HEAD_DIM * hh
        q_ref[:, base:base + HEAD_DIM] = (qa[:, base:base + HEAD_DIM] * scale).astype(BF16)
        rope = qa[:, base + HEAD_DIM:base + 2 * HEAD_DIM] * cc + qb[:, hh * LANES:(hh + 1) * LANES] * ss
        q_ref[:, base + HEAD_DIM:base + 2 * HEAD_DIM] = (rope * scale).astype(BF16)


def mla_prepare(h, cc, ss, q_norm_g, kv_norm_g, wq_a, wq_b, wkv, tm=512):
    t = h.shape[0]
    row = lambda i: (i, 0)
    fixed = lambda i: (0, 0)
    return pl.pallas_call(
        _mla_pre_body,
        grid=(t // tm,),
        in_specs=[pl.BlockSpec((tm, 1024), row),
                  pl.BlockSpec((tm, LANES), row),
                  pl.BlockSpec((tm, LANES), row),
                  pl.BlockSpec((1, MLA_Q_LORA), fixed),
                  pl.BlockSpec((1, MLA_KV_LORA), fixed),
                  pl.BlockSpec(wq_a.shape, fixed),
                  pl.BlockSpec(wq_b.shape, fixed),
                  pl.BlockSpec(wkv.shape, fixed)],
        out_specs=[pl.BlockSpec((tm, 2 * QK_WIDTH), row),
                   pl.BlockSpec((tm, QK_WIDTH), row),
                   pl.BlockSpec((tm, LANES), row),
                   pl.BlockSpec((tm, QK_WIDTH), row)],
        out_shape=[jax.ShapeDtypeStruct((t, 2 * QK_WIDTH), BF16),
                   jax.ShapeDtypeStruct((t, QK_WIDTH), BF16),
                   jax.ShapeDtypeStruct((t, LANES), BF16),
                   jax.ShapeDtypeStruct((t, QK_WIDTH), BF16)],
        compiler_params=_cparams(("parallel",)),
        name="mla_prepare",
    )(h, cc, ss, q_norm_g.reshape(1, -1), kv_norm_g.reshape(1, -1), wq_a, wq_b, wkv)


def _attn_body(q_ref, kn_ref, kr_ref, v_ref, o_ref, *, tk):
    qi = pl.program_id(2)
    q = q_ref[...]
    tq = q.shape[0]

    def step(j, carry, masked):
        m, l, acc = carry
        start = pl.multiple_of(j * tk, tk)
        kk = jnp.concatenate([kn_ref[pl.ds(start, tk), :], kr_ref[pl.ds(start, tk), :]], axis=1)
        s = _dot_nt(q, kk)
        if masked:
            r = lax.broadcasted_iota(jnp.int32, (tq, tk), 0)
            c = lax.broadcasted_iota(jnp.int32, (tq, tk), 1)
            s = jnp.where(r >= c, s, -jnp.inf)
        m_new = jnp.maximum(m, jnp.max(s, axis=-1, keepdims=True))
        p = jnp.exp(s - m_new)
        alpha = jnp.exp(m - m_new)
        l = alpha * l + jnp.sum(p, axis=-1, keepdims=True)
        acc = alpha * acc + _dot(p.astype(BF16), v_ref[pl.ds(start, tk), :])
        return m_new, l, acc

    init = (jnp.full((tq, 1), -jnp.inf, F32), jnp.zeros((tq, 1), F32), jnp.zeros((tq, HEAD_DIM), F32))
    carry = lax.fori_loop(0, qi, lambda j, cr: step(j, cr, False), init)
    _, l, acc = step(qi, carry, True)
    o_ref[...] = (acc / l).astype(o_ref.dtype)


def latent_attention(q, kn, kr, v, bsz, seq, tq=512):
    t = q.shape[0]
    nq = seq // tq
    return pl.pallas_call(
        functools.partial(_attn_body, tk=tq),
        grid=(bsz, HEADS, nq),
        in_specs=[pl.BlockSpec((tq, 2 * HEAD_DIM), lambda b, hh, i: (b * nq + i, hh)),
                  pl.BlockSpec((seq, HEAD_DIM), lambda b, hh, i: (b, hh)),
                  pl.BlockSpec((seq, LANES), lambda b, hh, i: (b, 0)),
                  pl.BlockSpec((seq, HEAD_DIM), lambda b, hh, i: (b, hh))],
        out_specs=pl.BlockSpec((tq, HEAD_DIM), lambda b, hh, i: (b * nq + i, hh)),
        out_shape=jax.ShapeDtypeStruct((t, QK_WIDTH), BF16),
        compiler_params=_cparams(("parallel", "parallel", "arbitrary")),
        name="latent_attention",
    )(q, kn, kr, v)


def _merge_body(yg_ref, ym_ref, ya_ref, g0_ref, g1_ref, g2_ref, gb_ref, pg_ref, pm_ref, pa_ref,
                wo_ref, x_ref, ln_g_ref, ln_b_ref, of_ref, ob_ref):
    gb = gb_ref[...]

    def branch(y_ref, p_ref, g_ref, idx):
        gate = _sigmoid(g_ref[...].astype(F32) + gb[:, idx * D_MODEL:(idx + 1) * D_MODEL])
        return gate * _dot(y_ref[...], p_ref[...])

    merged = branch(yg_ref, pg_ref, g0_ref, 0) + branch(ym_ref, pm_ref, g1_ref, 1) \
        + branch(ya_ref, pa_ref, g2_ref, 2)
    mix = _dot(merged.astype(BF16), wo_ref[...])
    y = _layer_norm(DEEPNORM_ALPHA * x_ref[...] + mix, ln_g_ref[...], ln_b_ref[...])
    of_ref[...] = y
    ob_ref[...] = y.astype(BF16)


def merge_branches(y_gdn, y_mlstm, y_mla, h, gate_bias, p_gdn, p_mlstm, p_mla, w_out, x, ln_g, ln_b, tm=512):
    t, d = x.shape
    row = lambda i: (i, 0)
    fixed = lambda i: (0, 0)
    ybs = pl.BlockSpec((tm, QK_WIDTH), row)
    pbs = pl.BlockSpec((QK_WIDTH, d), fixed)
    return pl.pallas_call(
        _merge_body,
        grid=(t // tm,),
        in_specs=[ybs, ybs, ybs,
                  pl.BlockSpec((tm, d), lambda i: (i, COL_GATE)),
                  pl.BlockSpec((tm, d), lambda i: (i, COL_GATE + 1)),
                  pl.BlockSpec((tm, d), lambda i: (i, COL_GATE + 2)),
                  pl.BlockSpec((1, 3 * d), fixed),
                  pbs, pbs, pbs,
                  pl.BlockSpec((d, d), fixed),
                  pl.BlockSpec((tm, d), row),
                  pl.BlockSpec((1, d), fixed),
                  pl.BlockSpec((1, d), fixed)],
        out_specs=[pl.BlockSpec((tm, d), row), pl.BlockSpec((tm, d), row)],
        out_shape=[jax.ShapeDtypeStruct((t, d), F32), jax.ShapeDtypeStruct((t, d), BF16)],
        compiler_params=_cparams(("parallel",)),
        name="merge_branches",
    )(y_gdn, y_mlstm, y_mla, h, h, h, gate_bias.reshape(1, 3 * d), p_gdn, p_mlstm, p_mla, w_out, x,
      ln_g.reshape(1, d), ln_b.reshape(1, d))


def _top2_sum(a, b, c, d):
    hi1, lo1 = jnp.maximum(a, b), jnp.minimum(a, b)
    hi2, lo2 = jnp.maximum(c, d), jnp.minimum(c, d)
    return jnp.maximum(hi1, hi2) + jnp.maximum(jnp.minimum(hi1, hi2), jnp.maximum(lo1, lo2))


def _router_body(x_ref, rwt_ref, rb_ref, su_ref, e_ref, w_ref, rank_ref, cnt_ref, carry_ref):
    @pl.when(pl.program_id(0) == 0)
    def _():
        carry_ref[...] = jnp.zeros_like(carry_ref)

    logits = _dot_nt(rwt_ref[...], x_ref[...], lax.Precision.HIGHEST)
    scores = _sigmoid(logits)
    biased = scores + rb_ref[...][:, :1]
    tm = logits.shape[1]
    brow = [biased[e:e + 1, :] for e in range(N_EXPERTS)]
    srow = [scores[e:e + 1, :] for e in range(N_EXPERTS)]
    best = _top2_sum(*brow[0:EXPERTS_PER_GROUP])
    grp = jnp.zeros((1, tm), jnp.int32)
    for g in range(1, N_GROUPS):
        gs = _top2_sum(*brow[g * EXPERTS_PER_GROUP:(g + 1) * EXPERTS_PER_GROUP])
        upd = gs > best
        best = jnp.where(upd, gs, best)
        grp = jnp.where(upd, g, grp)

    def in_group(rows, j):
        out = rows[j]
        for g in range(1, N_GROUPS):
            out = jnp.where(grp == g, rows[g * EXPERTS_PER_GROUP + j], out)
        return out

    ib = [in_group(brow, j) for j in range(EXPERTS_PER_GROUP)]
    isc = [in_group(srow, j) for j in range(EXPERTS_PER_GROUP)]
    v1, i1, s1 = ib[0], jnp.zeros((1, tm), jnp.int32), isc[0]
    for j in range(1, EXPERTS_PER_GROUP):
        upd = ib[j] > v1
        v1 = jnp.where(upd, ib[j], v1)
        i1 = jnp.where(upd, j, i1)
        s1 = jnp.where(upd, isc[j], s1)
    v2 = jnp.full((1, tm), -jnp.inf, F32)
    i2 = jnp.zeros((1, tm), jnp.int32)
    s2 = jnp.zeros((1, tm), F32)
    for j in range(EXPERTS_PER_GROUP):
        upd = jnp.logical_and(i1 != j, ib[j] > v2)
        v2 = jnp.where(upd, ib[j], v2)
        i2 = jnp.where(upd, j, i2)
        s2 = jnp.where(upd, isc[j], s2)
    e1 = grp * EXPERTS_PER_GROUP + i1
    e2 = grp * EXPERTS_PER_GROUP + i2
    total = s1 + s2
    e_ref[0:1, :] = e1
    e_ref[1:2, :] = e2
    w_ref[0:1, :] = s1 / total
    w_ref[1:2, :] = s2 / total
    erow = lax.broadcasted_iota(jnp.int32, (N_EXPERTS, tm), 0)
    oh1 = jnp.where(erow == e1, 1.0, 0.0).astype(F32)
    oh2 = jnp.where(erow == e2, 1.0, 0.0).astype(F32)
    both = oh1 + oh2
    before = _dot(both.astype(BF16), su_ref[...]) + carry_ref[...][:, :1]
    rank_ref[0:1, :] = jnp.sum(oh1 * before, axis=0, keepdims=True).astype(jnp.int32)
    rank_ref[1:2, :] = jnp.sum(oh2 * before, axis=0, keepdims=True).astype(jnp.int32)
    carry = carry_ref[...] + jnp.sum(both, axis=1, keepdims=True)
    carry_ref[...] = carry
    cnt_ref[...] = carry.astype(jnp.int32)


def route_tokens(x, router_w, router_bias, tm=512):
    t, d = x.shape
    strict_upper = jnp.triu(jnp.ones((tm, tm), BF16), k=1)
    fixed = lambda i: (0, 0)
    tok = lambda i: (0, i)
    return pl.pallas_call(
        _router_body,
        grid=(t // tm,),
        in_specs=[pl.BlockSpec((tm, d), lambda i: (i, 0)),
                  pl.BlockSpec((N_EXPERTS, d), fixed),
                  pl.BlockSpec((N_EXPERTS, LANES), fixed),
                  pl.BlockSpec((tm, tm), fixed)],
        out_specs=[pl.BlockSpec((TOP_K, tm), tok), pl.BlockSpec((TOP_K, tm), tok),
                   pl.BlockSpec((TOP_K, tm), tok), pl.BlockSpec((N_EXPERTS, LANES), fixed)],
        out_shape=[jax.ShapeDtypeStruct((TOP_K, t), jnp.int32), jax.ShapeDtypeStruct((TOP_K, t), F32),
                   jax.ShapeDtypeStruct((TOP_K, t), jnp.int32),
                   jax.ShapeDtypeStruct((N_EXPERTS, LANES), jnp.int32)],
        scratch_shapes=[pltpu.VMEM((N_EXPERTS, LANES), F32)],
        compiler_params=_cparams(("arbitrary",)),
        name="route_tokens",
    )(x, router_w.T, jnp.broadcast_to(router_bias.reshape(N_EXPERTS, 1), (N_EXPERTS, LANES)), strict_upper)


def _row_copy(x_hbm, buf, sem, slot, row, tok):
    return pltpu.make_async_copy(x_hbm.at[pl.ds(tok, 1), :], buf.at[slot, pl.ds(row, 1), :], sem.at[slot])


def _expert_body(be_ref, rt_ref, x_hbm, w1_ref, w3_ref, w2_ref, y_ref, buf, sem):
    i = pl.program_id(0)
    n = pl.num_programs(0)

    def issue(blk, slot):
        base = blk * EXPERT_BLOCK

        def one(r, carry):
            _row_copy(x_hbm, buf, sem, slot, r, rt_ref[base + r]).start()
            return carry

        lax.fori_loop(0, EXPERT_BLOCK, one, 0, unroll=8)

    @pl.when(i == 0)
    def _():
        issue(0, 0)

    @pl.when(i + 1 < n)
    def _():
        issue(i + 1, (i + 1) % 2)

    slot = i % 2

    def wait_one(r, carry):
        _row_copy(x_hbm, buf, sem, slot, r, 0).wait()
        return carry

    lax.fori_loop(0, EXPERT_BLOCK, wait_one, 0, unroll=8)
    x = buf[slot].astype(BF16)
    h1 = _dot(x, w1_ref[...])
    h3 = _dot(x, w3_ref[...])
    act = (h1 * _sigmoid(h1) * h3).astype(BF16)
    y_ref[...] = _dot(act, w2_ref[...])


def expert_ffn(block_expert, row_tok, x, w1, w3, w2):
    t, d = x.shape
    n_rows = row_tok.shape[0]
    n_blocks = n_rows // EXPERT_BLOCK
    grid_spec = pltpu.PrefetchScalarGridSpec(
        num_scalar_prefetch=2,
        grid=(n_blocks,),
        in_specs=[pl.BlockSpec(memory_space=pl.ANY),
                  pl.BlockSpec((None, d, D_EXPERT), lambda i, be, rt: (be[i], 0, 0)),
                  pl.BlockSpec((None, d, D_EXPERT), lambda i, be, rt: (be[i], 0, 0)),
                  pl.BlockSpec((None, D_EXPERT, d), lambda i, be, rt: (be[i], 0, 0))],
        out_specs=pl.BlockSpec((EXPERT_BLOCK, d), lambda i, be, rt: (i, 0)),
        scratch_shapes=[pltpu.VMEM((2, EXPERT_BLOCK, d), F32),
                        pltpu.SemaphoreType.DMA((2,))])
    return pl.pallas_call(
        _expert_body,
        grid_spec=grid_spec,
        out_shape=jax.ShapeDtypeStruct((n_rows, d), F32),
        compiler_params=_cparams(("arbitrary",)),
        name="expert_ffn",
    )(block_expert, row_tok, x, w1, w3, w2)


def _pair_copy(y_hbm, buf, sem, slot, k, row, src):
    return pltpu.make_async_copy(y_hbm.at[pl.ds(src, 1), :], buf.at[slot, k, pl.ds(row, 1), :], sem.at[slot])


def _combine_body(dest_ref, y_hbm, w_ref, x_ref, ln_g_ref, ln_b_ref, of_ref, ob_ref, buf, sem, *, n_tok):
    i = pl.program_id(0)
    n = pl.num_programs(0)
    tm = x_ref.shape[0]

    def issue(blk, slot):
        base = blk * tm

        def one(r, carry):
            for k in range(TOP_K):
                _pair_copy(y_hbm, buf, sem, slot, k, r, dest_ref[k * n_tok + base + r]).start()
            return carry

        lax.fori_loop(0, tm, one, 0, unroll=8)

    @pl.when(i == 0)
    def _():
        issue(0, 0)

    @pl.when(i + 1 < n)
    def _():
        issue(i + 1, (i + 1) % 2)

    slot = i % 2

    def wait_one(r, carry):
        for k in range(TOP_K):
            _pair_copy(y_hbm, buf, sem, slot, k, r, 0).wait()
        return carry

    lax.fori_loop(0, tm, wait_one, 0, unroll=8)
    w = w_ref[...]
    ffn = w[:, 0:1] * buf[slot, 0] + w[:, 1:2] * buf[slot, 1]
    y = _layer_norm(DEEPNORM_ALPHA * x_ref[...] + ffn, ln_g_ref[...], ln_b_ref[...])
    of_ref[...] = y
    ob_ref[...] = y.astype(BF16)


def combine_experts(dest, y_rows, w_pad, x, ln_g, ln_b, tm=256):
    t, d = x.shape
    row = lambda i, *_: (i, 0)
    fixed = lambda i, *_: (0, 0)
    grid_spec = pltpu.PrefetchScalarGridSpec(
        num_scalar_prefetch=1,
        grid=(t // tm,),
        in_specs=[pl.BlockSpec(memory_space=pl.ANY),
                  pl.BlockSpec((tm, LANES), row),
                  pl.BlockSpec((tm, d), row),
                  pl.BlockSpec((1, d), fixed),
                  pl.BlockSpec((1, d), fixed)],
        out_specs=[pl.BlockSpec((tm, d), row), pl.BlockSpec((tm, d), row)],
        scratch_shapes=[pltpu.VMEM((2, TOP_K, tm, d), F32),
                        pltpu.SemaphoreType.DMA((2,))])
    return pl.pallas_call(
        functools.partial(_combine_body, n_tok=t),
        grid_spec=grid_spec,
        out_shape=[jax.ShapeDtypeStruct((t, d), F32), jax.ShapeDtypeStruct((t, d), BF16)],
        compiler_params=_cparams(("arbitrary",)),
        name="combine_experts",
    )(dest, y_rows, w_pad, x, ln_g.reshape(1, d), ln_b.reshape(1, d))


def routed_experts(xf, router_w, router_bias, w1, w3, w2, ln_g, ln_b):
    t, d = xf.shape
    expert, weight, rank, counts = route_tokens(xf, router_w, router_bias)
    counts = counts[:, 0]
    padded = (counts + EXPERT_BLOCK - 1) // EXPERT_BLOCK * EXPERT_BLOCK
    padded_ends = jnp.cumsum(padded)
    padded_starts = padded_ends - padded
    dest = (padded_starts[expert] + rank).reshape(TOP_K * t)
    n_rows = TOP_K * t + N_EXPERTS * EXPERT_BLOCK
    tok = jnp.tile(jnp.arange(t, dtype=jnp.int32), TOP_K)
    row_tok = jnp.zeros((n_rows,), jnp.int32).at[dest].set(tok)
    block_start = jnp.arange(n_rows // EXPERT_BLOCK, dtype=jnp.int32) * EXPERT_BLOCK
    block_expert = jnp.minimum(jnp.searchsorted(padded_ends, block_start, side='right'),
                               N_EXPERTS - 1).astype(jnp.int32)
    y_rows = expert_ffn(block_expert, row_tok, xf, w1.astype(BF16), w3.astype(BF16), w2.astype(BF16))
    w_pad = jnp.pad(weight.T, ((0, 0), (0, LANES - TOP_K)))
    return combine_experts(dest, y_rows, w_pad, xf, ln_g, ln_b)


def _split_w_in(w_in):
    sizes = (QK_WIDTH, QK_WIDTH, QK_WIDTH, QK_WIDTH, HEADS, HEADS,
             QK_WIDTH, QK_WIDTH, QK_WIDTH, QK_WIDTH, HEADS, HEADS,
             MLA_Q_LORA, MLA_KV_LORA, MLA_ROPE, 3 * D_MODEL)
    parts, acc = [], 0
    for size in sizes:
        parts.append(w_in[:, acc:acc + size])
        acc += size
    return parts


def _arrange_w_in(w_in):
    (g_q, g_k, g_v, g_z, g_a, g_b, m_q, m_k, m_v, m_o, m_i, m_f, c_q, c_kv, k_rope, gates) = _split_w_in(w_in)
    d = w_in.shape[0]
    half = MLA_ROPE // 2
    pad64 = jnp.zeros((d, LANES - MLA_ROPE), w_in.dtype)
    rope_sw = jnp.concatenate([k_rope[:, half:], k_rope[:, :half]], axis=1)
    main = jnp.concatenate([c_q, c_kv, k_rope, pad64, rope_sw, pad64, jnp.zeros((d, LANES), w_in.dtype),
                            g_q, g_k, g_v, g_z, m_q, m_k, m_v, m_o, gates], axis=1)
    small = jnp.concatenate([g_a, g_b, m_i, m_f, jnp.zeros((d, LANES - 4 * HEADS), w_in.dtype)], axis=1)
    return main.astype(BF16), small.astype(BF16)


def _arrange_mla(w_uq, w_ukv):
    half = MLA_ROPE // 2
    wq = w_uq.reshape(MLA_Q_LORA, HEADS, HEAD_DIM + MLA_ROPE)
    nope, rope = wq[:, :, :HEAD_DIM], wq[:, :, HEAD_DIM:]
    pad = jnp.zeros((MLA_Q_LORA, HEADS, LANES - MLA_ROPE), w_uq.dtype)
    wq_a = jnp.concatenate([nope, rope, pad], axis=2).reshape(MLA_Q_LORA, HEADS * 2 * HEAD_DIM)
    rope_sw = jnp.concatenate([rope[:, :, half:], rope[:, :, :half]], axis=2)
    wq_b = jnp.concatenate([rope_sw, pad], axis=2).reshape(MLA_Q_LORA, HEADS * LANES)
    wkv = w_ukv.reshape(MLA_KV_LORA, HEADS, 2 * HEAD_DIM)
    wkv = jnp.concatenate([wkv[:, :, :HEAD_DIM].reshape(MLA_KV_LORA, QK_WIDTH),
                           wkv[:, :, HEAD_DIM:].reshape(MLA_KV_LORA, QK_WIDTH)], axis=1)
    return wq_a.astype(BF16), wq_b.astype(BF16), wkv.astype(BF16)


def kernel(x, positions, ln_in_g, ln_in_b, w_in, gdn_conv, gdn_a_log, gdn_dt_bias, gdn_norm, mlstm_gate_bias, mlstm_norm, mla_q_norm, mla_kv_norm, mla_w_uq, mla_w_ukv, w_br_gdn, w_br_mlstm, w_br_mla, gate_bias, w_out, ln1_g, ln1_b, router_w, router_bias, moe_w1, moe_w3, moe_w2, ln2_g, ln2_b):
    bsz, seq, d = x.shape
    t = bsz * seq
    xf, xb = layer_norm_entry(x.reshape(t, d), ln_in_g, ln_in_b)
    cc, ss = rope_tables(positions)
    for l in range(DEPTH):
        w_main, w_small = _arrange_w_in(w_in[l])
        h, small = in_projection(xb, w_main, w_small)
        small_t = small[:, :4 * HEADS].T.reshape(4 * HEADS, t // CHUNK, CHUNK)
        y_gdn = gated_deltanet(h, small, small_t, gdn_conv[l], gdn_a_log[l], gdn_dt_bias[l], gdn_norm[l],
                               bsz, seq)
        y_mlstm = mlstm(h, small, small_t, mlstm_gate_bias[l], mlstm_norm[l], bsz, seq)
        wq_a, wq_b, wkv = _arrange_mla(mla_w_uq[l], mla_w_ukv[l])
        q, kn, kr, v = mla_prepare(h, cc, ss, mla_q_norm[l], mla_kv_norm[l], wq_a, wq_b, wkv)
        y_mla = latent_attention(q, kn, kr, v, bsz, seq)
        xf, xb = merge_branches(y_gdn, y_mlstm, y_mla, h, gate_bias[l], w_br_gdn[l].astype(BF16),
                                w_br_mlstm[l].astype(BF16), w_br_mla[l].astype(BF16),
                                w_out[l].astype(BF16), xf, ln1_g[l], ln1_b[l])
        xf, xb = routed_experts(xf, router_w, router_bias, moe_w1[l], moe_w3[l], moe_w2[l],
                                ln2_g[l], ln2_b[l])
    return xf.reshape(bsz, seq, d)
```

```python
import functools

import jax
import jax.numpy as jnp
from jax import lax
from jax.experimental import pallas as pl
from jax.experimental.pallas import tpu as pltpu

F32 = jnp.float32
BF16 = jnp.bfloat16

D_MODEL = 1024
DEPTH = 2
HEADS = 4
HEAD_DIM = 128
CHUNK = 64
CONV_WIDTH = 4
GATE_CAP = 15.0
MLA_ROPE = 64
MLA_Q_LORA = 384
MLA_KV_LORA = 256
ROPE_THETA = 10000.0
N_EXPERTS = 16
N_GROUPS = 4
EXPERTS_PER_GROUP = 4
TOP_K = 2
D_EXPERT = 512
EXPERT_BLOCK = 256
LN_EPS = 1e-5
RMS_EPS = 1e-6
DEEPNORM_ALPHA = (2 * DEPTH) ** 0.25

LANES = 128
QK_WIDTH = HEADS * HEAD_DIM
H_WIDTH = 8192
COL_MLA = 0
COL_GDN = 8
COL_MLSTM = 24
COL_GATE = 5
SEQ_BLOCK = 512
GROUP = 256
VMEM_LIMIT = 48 * 1024 * 1024


def _cparams(sem):
    return pltpu.CompilerParams(dimension_semantics=sem, vmem_limit_bytes=VMEM_LIMIT)


def _sigmoid(x):
    return 1.0 / (1.0 + jnp.exp(-x))


def _layer_norm(x, g, b):
    mu = jnp.mean(x, axis=-1, keepdims=True)
    xc = x - mu
    var = jnp.mean(xc * xc, axis=-1, keepdims=True)
    return xc * lax.rsqrt(var + LN_EPS) * g + b


def _dot(a, b):
    return jnp.dot(a, b, preferred_element_type=F32)


def _dot_nt(a, b, precision=None):
    return lax.dot_general(a, b, (((1,), (1,)), ((), ())), preferred_element_type=F32,
                           precision=precision)


def _dot_tn(a, b, precision=None):
    return lax.dot_general(a, b, (((0,), (0,)), ((), ())), preferred_element_type=F32,
                           precision=precision)


def _ln_body(x_ref, g_ref, b_ref, of_ref, ob_ref):
    y = _layer_norm(x_ref[...], g_ref[...], b_ref[...])
    of_ref[...] = y
    ob_ref[...] = y.astype(BF16)


def layer_norm_entry(x, g, b, tm=512):
    t, d = x.shape
    return pl.pallas_call(
        _ln_body,
        grid=(t // tm,),
        in_specs=[pl.BlockSpec((tm, d), lambda i: (i, 0)),
                  pl.BlockSpec((1, d), lambda i: (0, 0)),
                  pl.BlockSpec((1, d), lambda i: (0, 0))],
        out_specs=[pl.BlockSpec((tm, d), lambda i: (i, 0)),
                   pl.BlockSpec((tm, d), lambda i: (i, 0))],
        out_shape=[jax.ShapeDtypeStruct((t, d), F32), jax.ShapeDtypeStruct((t, d), BF16)],
        compiler_params=_cparams(("parallel",)),
        name="ln_entry",
    )(x, g.reshape(1, d), b.reshape(1, d))


def _inproj_body(x_ref, w_ref, ws_ref, h_ref, hs_ref):
    x = x_ref[...]
    h_ref[...] = _dot(x, w_ref[...]).astype(BF16)

    @pl.when(pl.program_id(1) == 0)
    def _():
        hs_ref[...] = _dot(x, ws_ref[...])


def in_projection(xb, w_main, w_small, tm=1024, tn=512):
    t, d = xb.shape
    n = w_main.shape[1]
    return pl.pallas_call(
        _inproj_body,
        grid=(t // tm, n // tn),
        in_specs=[pl.BlockSpec((tm, d), lambda i, j: (i, 0)),
                  pl.BlockSpec((d, tn), lambda i, j: (0, j)),
                  pl.BlockSpec((d, LANES), lambda i, j: (0, 0))],
        out_specs=[pl.BlockSpec((tm, tn), lambda i, j: (i, j)),
                   pl.BlockSpec((tm, LANES), lambda i, j: (i, 0))],
        out_shape=[jax.ShapeDtypeStruct((t, n), BF16), jax.ShapeDtypeStruct((t, LANES), F32)],
        compiler_params=_cparams(("parallel", "arbitrary")),
        name="in_proj",
    )(xb, w_main, w_small)


def _lane_pick(x, lane):
    idx = lax.broadcasted_iota(jnp.int32, x.shape, 1)
    return jnp.sum(jnp.where(idx == lane, x, 0.0), axis=1, keepdims=True)


def _softplus(x):
    return jnp.maximum(x, 0.0) + jnp.log1p(jnp.exp(-jnp.abs(x)))


def _group_masks():
    r = lax.broadcasted_iota(jnp.int32, (GROUP, GROUP), 0)
    c = lax.broadcasted_iota(jnp.int32, (GROUP, GROUP), 1)
    same = (r // CHUNK) == (c // CHUNK)
    causal = jnp.logical_and(same, r >= c)
    strict = jnp.logical_and(same, r > c)
    upper = jnp.logical_and(same, r <= c)
    return causal, strict, upper, r == c


def _split_bf16(x):
    hi = x.astype(BF16)
    return hi, (x - hi.astype(F32)).astype(BF16)


def _group_cumsum_col(col, low_b):
    hi, lo = _split_bf16(jnp.broadcast_to(col, (GROUP, LANES)))
    return (_dot(low_b, hi) + _dot(low_b, lo))[:, :1]


def _group_cumsum_row(row, up_b):
    hi, lo = _split_bf16(jnp.broadcast_to(row, (16, GROUP)))
    return (_dot(hi, up_b) + _dot(lo, up_b))[0:1, :]


def _rms_norm(x, g):
    return x * lax.rsqrt(jnp.mean(x * x, axis=-1, keepdims=True) + RMS_EPS) * g


def _gdn_body(alog_ref, dtb_ref, q_ref, k_ref, v_ref, z_ref, cq_ref, ck_ref, cv_ref,
              sm_ref, smt_ref, ng_ref, o_ref, state_ref, eq_ref, ek_ref, ev_ref):
    hh = pl.program_id(1)
    lb = q_ref.shape[0]

    @pl.when(pl.program_id(2) == 0)
    def _():
        state_ref[...] = jnp.zeros_like(state_ref)
        for e_ref in (eq_ref, ek_ref, ev_ref):
            e_ref[0:8, :] = jnp.zeros((8, HEAD_DIM), F32)

    def conv_silu(x_ref, w_ref, e_ref):
        e_ref[8:, :] = x_ref[...].astype(F32)
        w = w_ref[...]
        y = w[0:1, :] * e_ref[pl.ds(8 - CONV_WIDTH + 1, lb), :]
        for j in range(1, CONV_WIDTH):
            y = y + w[j:j + 1, :] * e_ref[pl.ds(8 - CONV_WIDTH + 1 + j, lb), :]
        e_ref[0:8, :] = e_ref[lb:lb + 8, :]
        return y * _sigmoid(y)

    q = conv_silu(q_ref, cq_ref, eq_ref)
    k = conv_silu(k_ref, ck_ref, ek_ref)
    v = conv_silu(v_ref, cv_ref, ev_ref)
    q = q * lax.rsqrt(jnp.sum(q * q, axis=-1, keepdims=True) + RMS_EPS) * (HEAD_DIM ** -0.5)
    k = k * lax.rsqrt(jnp.sum(k * k, axis=-1, keepdims=True) + RMS_EPS)

    neg_a = -jnp.exp(jnp.full((1, 1), alog_ref[hh], F32))
    dtb = dtb_ref[hh]
    sm = sm_ref[...]
    g_col = neg_a * _softplus(_lane_pick(sm, hh) + dtb)
    beta_col = _sigmoid(_lane_pick(sm, HEADS + hh))
    g_row = neg_a * _softplus(smt_ref[pl.ds(hh, 1), :] + dtb)

    causal, strict, upper, diag = _group_masks()
    low_b = jnp.where(causal, 1.0, 0.0).astype(BF16)
    up_b = jnp.where(upper, 1.0, 0.0).astype(BF16)
    eye = jnp.where(diag, 1.0, 0.0).astype(F32)
    state = state_ref[...]
    z = z_ref[...].astype(F32)
    ng = ng_ref[...]
    for gi in range(lb // GROUP):
        gs = slice(gi * GROUP, (gi + 1) * GROUP)
        qg, kg, bg = q[gs], k[gs], beta_col[gs]
        gc_col = _group_cumsum_col(g_col[gs], low_b)
        gc_row = _group_cumsum_row(g_row[:, gs], up_b)
        decay = jnp.exp(jnp.where(causal, gc_col - gc_row, -jnp.inf))
        kg_b = kg.astype(BF16)
        kb = kg * bg
        a_mat = jnp.where(strict, _dot_nt(kb.astype(BF16), kg_b) * decay, 0.0)
        pw = -a_mat
        t_mat = eye + pw
        for _ in range(5):
            pw_b = pw.astype(BF16)
            pw = _dot(pw_b, pw_b)
            t_mat = t_mat + _dot(t_mat.astype(BF16), pw.astype(BF16))
        e_gc = jnp.exp(gc_col)
        rhs = jnp.concatenate([(v[gs] * bg).astype(BF16), (kb * e_gc).astype(BF16)], axis=1)
        uw = _dot(t_mat.astype(BF16), rhs).astype(BF16)
        qk = (_dot_nt(qg.astype(BF16), kg_b) * decay).astype(BF16)
        qk_uw = _dot(qk, uw)
        o_intra = qk_uw[:, :HEAD_DIM]
        q_eff = qg * e_gc - qk_uw[:, HEAD_DIM:]
        for c in range(GROUP // CHUNK):
            sl = slice(c * CHUNK, (c + 1) * CHUNK)
            gc_last = gc_col[(c + 1) * CHUNK - 1:(c + 1) * CHUNK, :]
            k_tail = (kg[sl] * jnp.exp(gc_last - gc_col[sl])).astype(BF16)
            kt_uw = _dot_tn(k_tail, uw[sl])
            lhs = jnp.concatenate([q_eff[sl], kt_uw[:, HEAD_DIM:]], axis=0).astype(BF16)
            res = _dot(lhs, state.astype(BF16))
            out = res[:CHUNK] + o_intra[sl]
            state = state * jnp.exp(gc_last) - res[CHUNK:] + kt_uw[:, :HEAD_DIM]
            rows = slice(gi * GROUP + c * CHUNK, gi * GROUP + (c + 1) * CHUNK)
            zc = z[rows]
            o_ref[rows, :] = (_rms_norm(out, ng) * (zc * _sigmoid(zc))).astype(o_ref.dtype)
    state_ref[...] = state


def gated_deltanet(h, small, small_t, conv_w, a_log, dt_bias, norm_g, bsz, seq):
    t = h.shape[0]
    lb = SEQ_BLOCK
    nb = seq // lb

    def col(off):
        return pl.BlockSpec((lb, HEAD_DIM), lambda b, hh, s, *_: (b * nb + s, COL_GDN + off + hh))

    def conv(off):
        return pl.BlockSpec((CONV_WIDTH, HEAD_DIM), lambda b, hh, s, *_: (0, off + hh))

    grid_spec = pltpu.PrefetchScalarGridSpec(
        num_scalar_prefetch=2,
        grid=(bsz, HEADS, nb),
        in_specs=[col(0), col(HEADS), col(2 * HEADS), col(3 * HEADS),
                  conv(0), conv(HEADS), conv(2 * HEADS),
                  pl.BlockSpec((lb, LANES), lambda b, hh, s, *_: (b * nb + s, 0)),
                  pl.BlockSpec((4 * HEADS, lb), lambda b, hh, s, *_: (0, b * nb + s)),
                  pl.BlockSpec((1, HEAD_DIM), lambda b, hh, s, *_: (0, 0))],
        out_specs=pl.BlockSpec((lb, HEAD_DIM), lambda b, hh, s, *_: (b * nb + s, hh)),
        scratch_shapes=[pltpu.VMEM((HEAD_DIM, HEAD_DIM), F32),
                        pltpu.VMEM((lb + 8, HEAD_DIM), F32),
                        pltpu.VMEM((lb + 8, HEAD_DIM), F32),
                        pltpu.VMEM((lb + 8, HEAD_DIM), F32)])
    return pl.pallas_call(
        _gdn_body,
        grid_spec=grid_spec,
        out_shape=jax.ShapeDtypeStruct((t, QK_WIDTH), BF16),
        compiler_params=_cparams(("parallel", "parallel", "arbitrary")),
        name="gated_deltanet",
    )(a_log, dt_bias, h, h, h, h, conv_w, conv_w, conv_w, small, small_t, norm_g.reshape(1, HEAD_DIM))


def _soft_cap(x):
    return GATE_CAP * jnp.tanh(x / GATE_CAP)


def _log_sigmoid(x):
    return jnp.minimum(x, 0.0) - jnp.log1p(jnp.exp(-jnp.abs(x)))


def _mlstm_body(gb_ref, q_ref, k_ref, v_ref, o_ref, sm_ref, smt_ref, ng_ref,
                y_ref, c_ref, n_ref, m_ref):
    hh = pl.program_id(1)
    lb = q_ref.shape[0]
    n_groups = lb // GROUP
    per_group = GROUP // CHUNK

    @pl.when(pl.program_id(2) == 0)
    def _():
        c_ref[...] = jnp.zeros_like(c_ref)
        n_ref[...] = jnp.zeros_like(n_ref)
        m_ref[...] = jnp.zeros_like(m_ref)

    q = q_ref[...]
    k = k_ref[...].astype(F32) * (HEAD_DIM ** -0.5)
    v = v_ref[...]
    gb_i = gb_ref[hh]
    gb_f = gb_ref[HEADS + hh]
    sm = sm_ref[...]
    i_col = _soft_cap(_lane_pick(sm, 2 * HEADS + hh) + gb_i)
    f_col = _log_sigmoid(_soft_cap(_lane_pick(sm, 3 * HEADS + hh) + gb_f))
    i_row = _soft_cap(smt_ref[pl.ds(2 * HEADS + hh, 1), :] + gb_i)
    f_row = _log_sigmoid(_soft_cap(smt_ref[pl.ds(3 * HEADS + hh, 1), :] + gb_f))

    causal, _, upper, _ = _group_masks()
    low_b = jnp.where(causal, 1.0, 0.0).astype(BF16)
    up_b = jnp.where(upper, 1.0, 0.0).astype(BF16)

    bc_cols, lkws, b_lasts, lkw_maxes = [], [], [], []
    for gi in range(n_groups):
        gs = slice(gi * GROUP, (gi + 1) * GROUP)
        bc_col = _group_cumsum_col(f_col[gs], low_b)
        bc_cols.append(bc_col)
        for c in range(per_group):
            sl = slice(c * CHUNK, (c + 1) * CHUNK)
            b_last = bc_col[(c + 1) * CHUNK - 1:(c + 1) * CHUNK, :]
            lkw = b_last - bc_col[sl] + i_col[gi * GROUP + c * CHUNK:gi * GROUP + (c + 1) * CHUNK]
            b_lasts.append(b_last)
            lkws.append(lkw)
            lkw_maxes.append(jnp.max(lkw, axis=0, keepdims=True))
    m_st = m_ref[...][:, :1]
    m_prev, m_next, carry_decay = [], [], []
    for ci in range(n_groups * per_group):
        m_new = jnp.maximum(b_lasts[ci] + m_st, lkw_maxes[ci])
        m_prev.append(m_st)
        m_next.append(m_new)
        carry_decay.append(jnp.exp(b_lasts[ci] + m_st - m_new))
        m_st = m_new

    c_st = c_ref[...]
    n_st = n_ref[...]
    o_pre = o_ref[...].astype(F32)
    ng = ng_ref[...]
    for gi in range(n_groups):
        gs = slice(gi * GROUP, (gi + 1) * GROUP)
        qg, vg = q[gs], v[gs]
        kg_b = k[gs].astype(BF16)
        bc_col = bc_cols[gi]
        bc_row = _group_cumsum_row(f_row[:, gs], up_b)
        log_d = jnp.where(causal, bc_col - bc_row + i_row[:, gs], -jnp.inf)
        m_prev_col = jnp.concatenate(
            [jnp.broadcast_to(m_prev[gi * per_group + c], (CHUNK, 1)) for c in range(per_group)], axis=0)
        log_inter = bc_col + m_prev_col
        m_t = jnp.maximum(log_inter, jnp.max(log_d, axis=-1, keepdims=True))
        w_inter = jnp.exp(log_inter - m_t)
        s = _dot_nt(qg, kg_b) * jnp.exp(log_d - m_t)
        s_v = _dot(s.astype(BF16), vg)
        s_sum = jnp.sum(s, axis=-1, keepdims=True)
        floor = jnp.exp(-m_t)
        for c in range(per_group):
            ci = gi * per_group + c
            sl = slice(c * CHUNK, (c + 1) * CHUNK)
            rows = slice(gi * GROUP + c * CHUNK, gi * GROUP + (c + 1) * CHUNK)
            qc = qg[sl]
            num = w_inter[sl] * _dot(qc, c_st.astype(BF16)) + s_v[sl]
            den = w_inter[sl] * jnp.sum(qc.astype(F32) * n_st, axis=-1, keepdims=True) + s_sum[sl]
            hid = num / jnp.maximum(jnp.abs(den), floor[sl])
            kw = k[rows] * jnp.exp(lkws[ci] - m_next[ci])
            c_st = carry_decay[ci] * c_st + _dot_tn(kw.astype(BF16), vg[sl])
            n_st = carry_decay[ci] * n_st + jnp.sum(kw, axis=0, keepdims=True)
            y_ref[rows, :] = (_sigmoid(o_pre[rows]) * _rms_norm(hid, ng)).astype(y_ref.dtype)
    c_ref[...] = c_st
    n_ref[...] = n_st
    m_ref[...] = jnp.broadcast_to(m_st, m_ref.shape)


def mlstm(h, small, small_t, gate_bias, norm_g, bsz, seq):
    t = h.shape[0]
    lb = SEQ_BLOCK
    nb = seq // lb

    def col(off):
        return pl.BlockSpec((lb, HEAD_DIM), lambda b, hh, s, *_: (b * nb + s, COL_MLSTM + off + hh))

    grid_spec = pltpu.PrefetchScalarGridSpec(
        num_scalar_prefetch=1,
        grid=(bsz, HEADS, nb),
        in_specs=[col(0), col(HEADS), col(2 * HEADS), col(3 * HEADS),
                  pl.BlockSpec((lb, LANES), lambda b, hh, s, *_: (b * nb + s, 0)),
                  pl.BlockSpec((4 * HEADS, lb), lambda b, hh, s, *_: (0, b * nb + s)),
                  pl.BlockSpec((1, HEAD_DIM), lambda b, hh, s, *_: (0, hh))],
        out_specs=pl.BlockSpec((lb, HEAD_DIM), lambda b, hh, s, *_: (b * nb + s, hh)),
        scratch_shapes=[pltpu.VMEM((HEAD_DIM, HEAD_DIM), F32),
                        pltpu.VMEM((1, HEAD_DIM), F32),
                        pltpu.VMEM((1, LANES), F32)])
    return pl.pallas_call(
        _mlstm_body,
        grid_spec=grid_spec,
        out_shape=jax.ShapeDtypeStruct((t, QK_WIDTH), BF16),
        compiler_params=_cparams(("parallel", "parallel", "arbitrary")),
        name="mlstm",
    )(gate_bias, h, h, h, h, small, small_t, norm_g.reshape(1, QK_WIDTH))


def _rope_table_body(pos_ref, freq_ref, sign_ref, cc_ref, ss_ref):
    ang = pos_ref[...] * freq_ref[...]
    sign = sign_ref[...]
    cc_ref[...] = jnp.cos(ang) * jnp.abs(sign)
    ss_ref[...] = jnp.sin(ang) * sign


def rope_tables(positions, tm=512):
    t = positions.size
    half = MLA_ROPE // 2
    inv_freq = 1.0 / (ROPE_THETA ** (jnp.arange(0, MLA_ROPE, 2, dtype=F32) / MLA_ROPE))
    zeros = jnp.zeros((LANES - MLA_ROPE,), F32)
    freq = jnp.concatenate([inv_freq, inv_freq, zeros]).reshape(1, LANES)
    sign = jnp.concatenate([-jnp.ones((half,), F32), jnp.ones((half,), F32), zeros]).reshape(1, LANES)
    return pl.pallas_call(
        _rope_table_body,
        grid=(t // tm,),
        in_specs=[pl.BlockSpec((tm, 1), lambda i: (i, 0)),
                  pl.BlockSpec((1, LANES), lambda i: (0, 0)),
                  pl.BlockSpec((1, LANES), lambda i: (0, 0))],
        out_specs=[pl.BlockSpec((tm, LANES), lambda i: (i, 0)),
                   pl.BlockSpec((tm, LANES), lambda i: (i, 0))],
        out_shape=[jax.ShapeDtypeStruct((t, LANES), F32), jax.ShapeDtypeStruct((t, LANES), F32)],
        compiler_params=_cparams(("parallel",)),
        name="rope_tables",
    )(positions.astype(F32).reshape(t, 1), freq, sign)


def _mla_pre_body(h_ref, cc_ref, ss_ref, qg_ref, kvg_ref, wqa_ref, wqb_ref, wkv_ref,
                  q_ref, kn_ref, kr_ref, v_ref):
    hblk = h_ref[...].astype(F32)
    cc = cc_ref[...]
    ss = ss_ref[...]
    cq = _rms_norm(hblk[:, :MLA_Q_LORA], qg_ref[...]).astype(BF16)
    ckv = _rms_norm(hblk[:, MLA_Q_LORA:MLA_Q_LORA + MLA_KV_LORA], kvg_ref[...]).astype(BF16)
    off = MLA_Q_LORA + MLA_KV_LORA
    kr_ref[...] = (hblk[:, off:off + LANES] * cc + hblk[:, off + LANES:off + 2 * LANES] * ss).astype(BF16)
    kv = _dot(ckv, wkv_ref[...])
    kn_ref[...] = kv[:, :QK_WIDTH].astype(BF16)
    v_ref[...] = kv[:, QK_WIDTH:].astype(BF16)
    qa = _dot(cq, wqa_ref[...])
    qb = _dot(cq, wqb_ref[...])
    scale = (HEAD_DIM + MLA_ROPE) ** -0.5
    for hh in range(HEADS):
        base = 2 * HEAD_DIM * hh
        q_ref[:, base:base + HEAD_DIM] = (qa[:, base:base + HEAD_DIM] * scale).astype(BF16)
        rope = qa[:, base + HEAD_DIM:base + 2 * HEAD_DIM] * cc + qb[:, hh * LANES:(hh + 1) * LANES] * ss
        q_ref[:, base + HEAD_DIM:base + 2 * HEAD_DIM] = (rope * scale).astype(BF16)


def mla_prepare(h, cc, ss, q_norm_g, kv_norm_g, wq_a, wq_b, wkv, tm=512):
    t = h.shape[0]
    row = lambda i: (i, 0)
    fixed = lambda i: (0, 0)
    return pl.pallas_call(
        _mla_pre_body,
        grid=(t // tm,),
        in_specs=[pl.BlockSpec((tm, 1024), row),
                  pl.BlockSpec((tm, LANES), row),
                  pl.BlockSpec((tm, LANES), row),
                  pl.BlockSpec((1, MLA_Q_LORA), fixed),
                  pl.BlockSpec((1, MLA_KV_LORA), fixed),
                  pl.BlockSpec(wq_a.shape, fixed),
                  pl.BlockSpec(wq_b.shape, fixed),
                  pl.BlockSpec(wkv.shape, fixed)],
        out_specs=[pl.BlockSpec((tm, 2 * QK_WIDTH), row),
                   pl.BlockSpec((tm, QK_WIDTH), row),
                   pl.BlockSpec((tm, LANES), row),
                   pl.BlockSpec((tm, QK_WIDTH), row)],
        out_shape=[jax.ShapeDtypeStruct((t, 2 * QK_WIDTH), BF16),
                   jax.ShapeDtypeStruct((t, QK_WIDTH), BF16),
                   jax.ShapeDtypeStruct((t, LANES), BF16),
                   jax.ShapeDtypeStruct((t, QK_WIDTH), BF16)],
        compiler_params=_cparams(("parallel",)),
        name="mla_prepare",
    )(h, cc, ss, q_norm_g.reshape(1, -1), kv_norm_g.reshape(1, -1), wq_a, wq_b, wkv)


def _attn_body(q_ref, kn_ref, kr_ref, v_ref, o_ref, *, tk):
    qi = pl.program_id(2)
    q = q_ref[...]
    tq = q.shape[0]

    def step(j, carry, masked):
        m, l, acc = carry
        start = pl.multiple_of(j * tk, tk)
        kk = jnp.concatenate([kn_ref[pl.ds(start, tk), :], kr_ref[pl.ds(start, tk), :]], axis=1)
        s = _dot_nt(q, kk)
        if masked:
            r = lax.broadcasted_iota(jnp.int32, (tq, tk), 0)
            c = lax.broadcasted_iota(jnp.int32, (tq, tk), 1)
            s = jnp.where(r >= c, s, -jnp.inf)
        m_new = jnp.maximum(m, jnp.max(s, axis=-1, keepdims=True))
        p = jnp.exp(s - m_new)
        alpha = jnp.exp(m - m_new)
        l = alpha * l + jnp.sum(p, axis=-1, keepdims=True)
        acc = alpha * acc + _dot(p.astype(BF16), v_ref[pl.ds(start, tk), :])
        return m_new, l, acc

    init = (jnp.full((tq, 1), -jnp.inf, F32), jnp.zeros((tq, 1), F32), jnp.zeros((tq, HEAD_DIM), F32))
    carry = lax.fori_loop(0, qi, lambda j, cr: step(j, cr, False), init)
    _, l, acc = step(qi, carry, True)
    o_ref[...] = (acc / l).astype(o_ref.dtype)


def latent_attention(q, kn, kr, v, bsz, seq, tq=512):
    t = q.shape[0]
    nq = seq // tq
    return pl.pallas_call(
        functools.partial(_attn_body, tk=tq),
        grid=(bsz, HEADS, nq),
        in_specs=[pl.BlockSpec((tq, 2 * HEAD_DIM), lambda b, hh, i: (b * nq + i, hh)),
                  pl.BlockSpec((seq, HEAD_DIM), lambda b, hh, i: (b, hh)),
                  pl.BlockSpec((seq, LANES), lambda b, hh, i: (b, 0)),
                  pl.BlockSpec((seq, HEAD_DIM), lambda b, hh, i: (b, hh))],
        out_specs=pl.BlockSpec((tq, HEAD_DIM), lambda b, hh, i: (b * nq + i, hh)),
        out_shape=jax.ShapeDtypeStruct((t, QK_WIDTH), BF16),
        compiler_params=_cparams(("parallel", "parallel", "arbitrary")),
        name="latent_attention",
    )(q, kn, kr, v)


def _merge_body(yg_ref, ym_ref, ya_ref, g0_ref, g1_ref, g2_ref, gb_ref, pg_ref, pm_ref, pa_ref,
                wo_ref, x_ref, ln_g_ref, ln_b_ref, of_ref, ob_ref):
    gb = gb_ref[...]

    def branch(y_ref, p_ref, g_ref, idx):
        gate = _sigmoid(g_ref[...].astype(F32) + gb[:, idx * D_MODEL:(idx + 1) * D_MODEL])
        return gate * _dot(y_ref[...], p_ref[...])

    merged = branch(yg_ref, pg_ref, g0_ref, 0) + branch(ym_ref, pm_ref, g1_ref, 1) \
        + branch(ya_ref, pa_ref, g2_ref, 2)
    mix = _dot(merged.astype(BF16), wo_ref[...])
    y = _layer_norm(DEEPNORM_ALPHA * x_ref[...] + mix, ln_g_ref[...], ln_b_ref[...])
    of_ref[...] = y
    ob_ref[...] = y.astype(BF16)


def merge_branches(y_gdn, y_mlstm, y_mla, h, gate_bias, p_gdn, p_mlstm, p_mla, w_out, x, ln_g, ln_b, tm=512):
    t, d = x.shape
    row = lambda i: (i, 0)
    fixed = lambda i: (0, 0)
    ybs = pl.BlockSpec((tm, QK_WIDTH), row)
    pbs = pl.BlockSpec((QK_WIDTH, d), fixed)
    return pl.pallas_call(
        _merge_body,
        grid=(t // tm,),
        in_specs=[ybs, ybs, ybs,
                  pl.BlockSpec((tm, d), lambda i: (i, COL_GATE)),
                  pl.BlockSpec((tm, d), lambda i: (i, COL_GATE + 1)),
                  pl.BlockSpec((tm, d), lambda i: (i, COL_GATE + 2)),
                  pl.BlockSpec((1, 3 * d), fixed),
                  pbs, pbs, pbs,
                  pl.BlockSpec((d, d), fixed),
                  pl.BlockSpec((tm, d), row),
                  pl.BlockSpec((1, d), fixed),
                  pl.BlockSpec((1, d), fixed)],
        out_specs=[pl.BlockSpec((tm, d), row), pl.BlockSpec((tm, d), row)],
        out_shape=[jax.ShapeDtypeStruct((t, d), F32), jax.ShapeDtypeStruct((t, d), BF16)],
        compiler_params=_cparams(("parallel",)),
        name="merge_branches",
    )(y_gdn, y_mlstm, y_mla, h, h, h, gate_bias.reshape(1, 3 * d), p_gdn, p_mlstm, p_mla, w_out, x,
      ln_g.reshape(1, d), ln_b.reshape(1, d))


def _top2_sum(a, b, c, d):
    hi1, lo1 = jnp.maximum(a, b), jnp.minimum(a, b)
    hi2, lo2 = jnp.maximum(c, d), jnp.minimum(c, d)
    return jnp.maximum(hi1, hi2) + jnp.maximum(jnp.minimum(hi1, hi2), jnp.maximum(lo1, lo2))


def _router_body(x_ref, rwt_ref, rb_ref, su_ref, e_ref, w_ref, rank_ref, cnt_ref, carry_ref):
    @pl.when(pl.program_id(0) == 0)
    def _():
        carry_ref[...] = jnp.zeros_like(carry_ref)

    logits = _dot_nt(rwt_ref[...], x_ref[...], lax.Precision.HIGHEST)
    scores = _sigmoid(logits)
    biased = scores + rb_ref[...][:, :1]
    tm = logits.shape[1]
    brow = [biased[e:e + 1, :] for e in range(N_EXPERTS)]
    srow = [scores[e:e + 1, :] for e in range(N_EXPERTS)]
    best = _top2_sum(*brow[0:EXPERTS_PER_GROUP])
    grp = jnp.zeros((1, tm), jnp.int32)
    for g in range(1, N_GROUPS):
        gs = _top2_sum(*brow[g * EXPERTS_PER_GROUP:(g + 1) * EXPERTS_PER_GROUP])
        upd = gs > best
        best = jnp.where(upd, gs, best)
        grp = jnp.where(upd, g, grp)

    def in_group(rows, j):
        out = rows[j]
        for g in range(1, N_GROUPS):
            out = jnp.where(grp == g, rows[g * EXPERTS_PER_GROUP + j], out)
        return out

    ib = [in_group(brow, j) for j in range(EXPERTS_PER_GROUP)]
    isc = [in_group(srow, j) for j in range(EXPERTS_PER_GROUP)]
    v1, i1, s1 = ib[0], jnp.zeros((1, tm), jnp.int32), isc[0]
    for j in range(1, EXPERTS_PER_GROUP):
        upd = ib[j] > v1
        v1 = jnp.where(upd, ib[j], v1)
        i1 = jnp.where(upd, j, i1)
        s1 = jnp.where(upd, isc[j], s1)
    v2 = jnp.full((1, tm), -jnp.inf, F32)
    i2 = jnp.zeros((1, tm), jnp.int32)
    s2 = jnp.zeros((1, tm), F32)
    for j in range(EXPERTS_PER_GROUP):
        upd = jnp.logical_and(i1 != j, ib[j] > v2)
        v2 = jnp.where(upd, ib[j], v2)
        i2 = jnp.where(upd, j, i2)
        s2 = jnp.where(upd, isc[j], s2)
    e1 = grp * EXPERTS_PER_GROUP + i1
    e2 = grp * EXPERTS_PER_GROUP + i2
    total = s1 + s2
    e_ref[0:1, :] = e1
    e_ref[1:2, :] = e2
    w_ref[0:1, :] = s1 / total
    w_ref[1:2, :] = s2 / total
    erow = lax.broadcasted_iota(jnp.int32, (N_EXPERTS, tm), 0)
    oh1 = jnp.where(erow == e1, 1.0, 0.0).astype(F32)
    oh2 = jnp.where(erow == e2, 1.0, 0.0).astype(F32)
    both = oh1 + oh2
    before = _dot(both.astype(BF16), su_ref[...]) + carry_ref[...][:, :1]
    rank_ref[0:1, :] = jnp.sum(oh1 * before, axis=0, keepdims=True).astype(jnp.int32)
    rank_ref[1:2, :] = jnp.sum(oh2 * before, axis=0, keepdims=True).astype(jnp.int32)
    carry = carry_ref[...] + jnp.sum(both, axis=1, keepdims=True)
    carry_ref[...] = carry
    cnt_ref[...] = carry.astype(jnp.int32)


def route_tokens(x, router_w, router_bias, tm=512):
    t, d = x.shape
    strict_upper = jnp.triu(jnp.ones((tm, tm), BF16), k=1)
    fixed = lambda i: (0, 0)
    tok = lambda i: (0, i)
    return pl.pallas_call(
        _router_body,
        grid=(t // tm,),
        in_specs=[pl.BlockSpec((tm, d), lambda i: (i, 0)),
                  pl.BlockSpec((N_EXPERTS, d), fixed),
                  pl.BlockSpec((N_EXPERTS, LANES), fixed),
                  pl.BlockSpec((tm, tm), fixed)],
        out_specs=[pl.BlockSpec((TOP_K, tm), tok), pl.BlockSpec((TOP_K, tm), tok),
                   pl.BlockSpec((TOP_K, tm), tok), pl.BlockSpec((N_EXPERTS, LANES), fixed)],
        out_shape=[jax.ShapeDtypeStruct((TOP_K, t), jnp.int32), jax.ShapeDtypeStruct((TOP_K, t), F32),
                   jax.ShapeDtypeStruct((TOP_K, t), jnp.int32),
                   jax.ShapeDtypeStruct((N_EXPERTS, LANES), jnp.int32)],
        scratch_shapes=[pltpu.VMEM((N_EXPERTS, LANES), F32)],
        compiler_params=_cparams(("arbitrary",)),
        name="route_tokens",
    )(x, router_w.T, jnp.broadcast_to(router_bias.reshape(N_EXPERTS, 1), (N_EXPERTS, LANES)), strict_upper)


def _row_copy(x_hbm, buf, sem, slot, row, tok):
    return pltpu.make_async_copy(x_hbm.at[pl.ds(tok, 1), :], buf.at[slot, pl.ds(row, 1), :], sem.at[slot])


def _expert_body(be_ref, rt_ref, x_hbm, w1_ref, w3_ref, w2_ref, y_ref, buf, sem):
    i = pl.program_id(0)
    n = pl.num_programs(0)

    def issue(blk, slot):
        base = blk * EXPERT_BLOCK

        def one(r, carry):
            _row_copy(x_hbm, buf, sem, slot, r, rt_ref[base + r]).start()
            return carry

        lax.fori_loop(0, EXPERT_BLOCK, one, 0, unroll=8)

    @pl.when(i == 0)
    def _():
        issue(0, 0)

    @pl.when(i + 1 < n)
    def _():
        issue(i + 1, (i + 1) % 2)

    slot = i % 2

    def wait_one(r, carry):
        _row_copy(x_hbm, buf, sem, slot, r, 0).wait()
        return carry

    lax.fori_loop(0, EXPERT_BLOCK, wait_one, 0, unroll=8)
    x = buf[slot].astype(BF16)
    h1 = _dot(x, w1_ref[...])
    h3 = _dot(x, w3_ref[...])
    act = (h1 * _sigmoid(h1) * h3).astype(BF16)
    y_ref[...] = _dot(act, w2_ref[...])


def expert_ffn(block_expert, row_tok, x, w1, w3, w2):
    t, d = x.shape
    n_rows = row_tok.shape[0]
    n_blocks = n_rows // EXPERT_BLOCK
    grid_spec = pltpu.PrefetchScalarGridSpec(
        num_scalar_prefetch=2,
        grid=(n_blocks,),
        in_specs=[pl.BlockSpec(memory_space=pl.ANY),
                  pl.BlockSpec((None, d, D_EXPERT), lambda i, be, rt: (be[i], 0, 0)),
                  pl.BlockSpec((None, d, D_EXPERT), lambda i, be, rt: (be[i], 0, 0)),
                  pl.BlockSpec((None, D_EXPERT, d), lambda i, be, rt: (be[i], 0, 0))],
        out_specs=pl.BlockSpec((EXPERT_BLOCK, d), lambda i, be, rt: (i, 0)),
        scratch_shapes=[pltpu.VMEM((2, EXPERT_BLOCK, d), F32),
                        pltpu.SemaphoreType.DMA((2,))])
    return pl.pallas_call(
        _expert_body,
        grid_spec=grid_spec,
        out_shape=jax.ShapeDtypeStruct((n_rows, d), F32),
        compiler_params=_cparams(("arbitrary",)),
        name="expert_ffn",
    )(block_expert, row_tok, x, w1, w3, w2)


def _pair_copy(y_hbm, buf, sem, slot, k, row, src):
    return pltpu.make_async_copy(y_hbm.at[pl.ds(src, 1), :], buf.at[slot, k, pl.ds(row, 1), :], sem.at[slot])


def _combine_body(dest_ref, y_hbm, w_ref, x_ref, ln_g_ref, ln_b_ref, of_ref, ob_ref, buf, sem, *, n_tok):
    i = pl.program_id(0)
    n = pl.num_programs(0)
    tm = x_ref.shape[0]

    def issue(blk, slot):
        base = blk * tm

        def one(r, carry):
            for k in range(TOP_K):
                _pair_copy(y_hbm, buf, sem, slot, k, r, dest_ref[k * n_tok + base + r]).start()
            return carry

        lax.fori_loop(0, tm, one, 0, unroll=8)

    @pl.when(i == 0)
    def _():
        issue(0, 0)

    @pl.when(i + 1 < n)
    def _():
        issue(i + 1, (i + 1) % 2)

    slot = i % 2

    def wait_one(r, carry):
        for k in range(TOP_K):
            _pair_copy(y_hbm, buf, sem, slot, k, r, 0).wait()
        return carry

    lax.fori_loop(0, tm, wait_one, 0, unroll=8)
    w = w_ref[...]
    ffn = w[:, 0:1] * buf[slot, 0] + w[:, 1:2] * buf[slot, 1]
    y = _layer_norm(DEEPNORM_ALPHA * x_ref[...] + ffn, ln_g_ref[...], ln_b_ref[...])
    of_ref[...] = y
    ob_ref[...] = y.astype(BF16)


def combine_experts(dest, y_rows, w_pad, x, ln_g, ln_b, tm=256):
    t, d = x.shape
    row = lambda i, *_: (i, 0)
    fixed = lambda i, *_: (0, 0)
    grid_spec = pltpu.PrefetchScalarGridSpec(
        num_scalar_prefetch=1,
        grid=(t // tm,),
        in_specs=[pl.BlockSpec(memory_space=pl.ANY),
                  pl.BlockSpec((tm, LANES), row),
                  pl.BlockSpec((tm, d), row),
                  pl.BlockSpec((1, d), fixed),
                  pl.BlockSpec((1, d), fixed)],
        out_specs=[pl.BlockSpec((tm, d), row), pl.BlockSpec((tm, d), row)],
        scratch_shapes=[pltpu.VMEM((2, TOP_K, tm, d), F32),
                        pltpu.SemaphoreType.DMA((2,))])
    return pl.pallas_call(
        functools.partial(_combine_body, n_tok=t),
        grid_spec=grid_spec,
        out_shape=[jax.ShapeDtypeStruct((t, d), F32), jax.ShapeDtypeStruct((t, d), BF16)],
        compiler_params=_cparams(("arbitrary",)),
        name="combine_experts",
    )(dest, y_rows, w_pad, x, ln_g.reshape(1, d), ln_b.reshape(1, d))


def routed_experts(xf, router_w, router_bias, w1, w3, w2, ln_g, ln_b):
    t, d = xf.shape
    expert, weight, rank, counts = route_tokens(xf, router_w, router_bias)
    counts = counts[:, 0]
    padded = (counts + EXPERT_BLOCK - 1) // EXPERT_BLOCK * EXPERT_BLOCK
    padded_ends = jnp.cumsum(padded)
    padded_starts = padded_ends - padded
    start_of = jnp.zeros_like(expert)
    for e in range(N_EXPERTS):
        start_of = jnp.where(expert == e, padded_starts[e], start_of)
    dest = (start_of + rank).reshape(TOP_K * t)
    n_rows = TOP_K * t + N_EXPERTS * EXPERT_BLOCK
    tok = jnp.tile(jnp.arange(t, dtype=jnp.int32), TOP_K)
    row_tok = jnp.zeros((n_rows,), jnp.int32).at[dest].set(tok)
    block_start = jnp.arange(n_rows // EXPERT_BLOCK, dtype=jnp.int32) * EXPERT_BLOCK
    block_expert = jnp.minimum(jnp.searchsorted(padded_ends, block_start, side='right'),
                               N_EXPERTS - 1).astype(jnp.int32)
    y_rows = expert_ffn(block_expert, row_tok, xf, w1.astype(BF16), w3.astype(BF16), w2.astype(BF16))
    w_pad = jnp.pad(weight.T, ((0, 0), (0, LANES - TOP_K)))
    return combine_experts(dest, y_rows, w_pad, xf, ln_g, ln_b)


def _split_w_in(w_in):
    sizes = (QK_WIDTH, QK_WIDTH, QK_WIDTH, QK_WIDTH, HEADS, HEADS,
             QK_WIDTH, QK_WIDTH, QK_WIDTH, QK_WIDTH, HEADS, HEADS,
             MLA_Q_LORA, MLA_KV_LORA, MLA_ROPE, 3 * D_MODEL)
    parts, acc = [], 0
    for size in sizes:
        parts.append(w_in[:, acc:acc + size])
        acc += size
    return parts


def _arrange_w_in(w_in):
    (g_q, g_k, g_v, g_z, g_a, g_b, m_q, m_k, m_v, m_o, m_i, m_f, c_q, c_kv, k_rope, gates) = _split_w_in(w_in)
    d = w_in.shape[0]
    half = MLA_ROPE // 2
    pad64 = jnp.zeros((d, LANES - MLA_ROPE), w_in.dtype)
    rope_sw = jnp.concatenate([k_rope[:, half:], k_rope[:, :half]], axis=1)
    main = jnp.concatenate([c_q, c_kv, k_rope, pad64, rope_sw, pad64, jnp.zeros((d, LANES), w_in.dtype),
                            g_q, g_k, g_v, g_z, m_q, m_k, m_v, m_o, gates], axis=1)
    small = jnp.concatenate([g_a, g_b, m_i, m_f, jnp.zeros((d, LANES - 4 * HEADS), w_in.dtype)], axis=1)
    return main.astype(BF16), small.astype(BF16)


def _arrange_mla(w_uq, w_ukv):
    half = MLA_ROPE // 2
    wq = w_uq.reshape(MLA_Q_LORA, HEADS, HEAD_DIM + MLA_ROPE)
    nope, rope = wq[:, :, :HEAD_DIM], wq[:, :, HEAD_DIM:]
    pad = jnp.zeros((MLA_Q_LORA, HEADS, LANES - MLA_ROPE), w_uq.dtype)
    wq_a = jnp.concatenate([nope, rope, pad], axis=2).reshape(MLA_Q_LORA, HEADS * 2 * HEAD_DIM)
    rope_sw = jnp.concatenate([rope[:, :, half:], rope[:, :, :half]], axis=2)
    wq_b = jnp.concatenate([rope_sw, pad], axis=2).reshape(MLA_Q_LORA, HEADS * LANES)
    wkv = w_ukv.reshape(MLA_KV_LORA, HEADS, 2 * HEAD_DIM)
    wkv = jnp.concatenate([wkv[:, :, :HEAD_DIM].reshape(MLA_KV_LORA, QK_WIDTH),
                           wkv[:, :, HEAD_DIM:].reshape(MLA_KV_LORA, QK_WIDTH)], axis=1)
    return wq_a.astype(BF16), wq_b.astype(BF16), wkv.astype(BF16)


def kernel(x, positions, ln_in_g, ln_in_b, w_in, gdn_conv, gdn_a_log, gdn_dt_bias, gdn_norm, mlstm_gate_bias, mlstm_norm, mla_q_norm, mla_kv_norm, mla_w_uq, mla_w_ukv, w_br_gdn, w_br_mlstm, w_br_mla, gate_bias, w_out, ln1_g, ln1_b, router_w, router_bias, moe_w1, moe_w3, moe_w2, ln2_g, ln2_b):
    bsz, seq, d = x.shape
    t = bsz * seq
    xf, xb = layer_norm_entry(x.reshape(t, d), ln_in_g, ln_in_b)
    cc, ss = rope_tables(positions)
    for l in range(DEPTH):
        w_main, w_small = _arrange_w_in(w_in[l])
        h, small = in_projection(xb, w_main, w_small)
        small_t = small[:, :4 * HEADS].T
        y_gdn = gated_deltanet(h, small, small_t, gdn_conv[l], gdn_a_log[l], gdn_dt_bias[l], gdn_norm[l],
                               bsz, seq)
        y_mlstm = mlstm(h, small, small_t, mlstm_gate_bias[l], mlstm_norm[l], bsz, seq)
        wq_a, wq_b, wkv = _arrange_mla(mla_w_uq[l], mla_w_ukv[l])
        q, kn, kr, v = mla_prepare(h, cc, ss, mla_q_norm[l], mla_kv_norm[l], wq_a, wq_b, wkv)
        y_mla = latent_attention(q, kn, kr, v, bsz, seq)
        xf, xb = merge_branches(y_gdn, y_mlstm, y_mla, h, gate_bias[l], w_br_gdn[l].astype(BF16),
                                w_br_mlstm[l].astype(BF16), w_br_mla[l].astype(BF16),
                                w_out[l].astype(BF16), xf, ln1_g[l], ln1_b[l])
        xf, xb = routed_experts(xf, router_w, router_bias, moe_w1[l], moe_w3[l], moe_w2[l],
                                ln2_g[l], ln2_b[l])
    return xf.reshape(bsz, seq, d)
```

```python
import functools

import jax
import jax.numpy as jnp
from jax import lax
from jax.experimental import pallas as pl
from jax.experimental.pallas import tpu as pltpu

F32 = jnp.float32
BF16 = jnp.bfloat16

D_MODEL = 1024
DEPTH = 2
HEADS = 4
HEAD_DIM = 128
CHUNK = 64
CONV_WIDTH = 4
GATE_CAP = 15.0
MLA_ROPE = 64
MLA_Q_LORA = 384
MLA_KV_LORA = 256
ROPE_THETA = 10000.0
N_EXPERTS = 16
N_GROUPS = 4
EXPERTS_PER_GROUP = 4
TOP_K = 2
D_EXPERT = 512
EXPERT_BLOCK = 256
LN_EPS = 1e-5
RMS_EPS = 1e-6
DEEPNORM_ALPHA = (2 * DEPTH) ** 0.25

LANES = 128
QK_WIDTH = HEADS * HEAD_DIM
H_WIDTH = 8192
COL_MLA = 0
COL_GDN = 8
COL_MLSTM = 24
COL_GATE = 5
SEQ_BLOCK = 512
GROUP = 256
VMEM_LIMIT = 48 * 1024 * 1024


def _cparams(sem):
    return pltpu.CompilerParams(dimension_semantics=sem, vmem_limit_bytes=VMEM_LIMIT)


def _sigmoid(x):
    return 1.0 / (1.0 + jnp.exp(-x))


def _layer_norm(x, g, b):
    mu = jnp.mean(x, axis=-1, keepdims=True)
    xc = x - mu
    var = jnp.mean(xc * xc, axis=-1, keepdims=True)
    return xc * lax.rsqrt(var + LN_EPS) * g + b


def _dot(a, b):
    return jnp.dot(a, b, preferred_element_type=F32)


def _dot_nt(a, b, precision=None):
    return lax.dot_general(a, b, (((1,), (1,)), ((), ())), preferred_element_type=F32,
                           precision=precision)


def _dot_tn(a, b, precision=None):
    return lax.dot_general(a, b, (((0,), (0,)), ((), ())), preferred_element_type=F32,
                           precision=precision)


def _ln_body(x_ref, g_ref, b_ref, of_ref, ob_ref):
    y = _layer_norm(x_ref[...], g_ref[...], b_ref[...])
    of_ref[...] = y
    ob_ref[...] = y.astype(BF16)


def layer_norm_entry(x, g, b, tm=512):
    t, d = x.shape
    return pl.pallas_call(
        _ln_body,
        grid=(t // tm,),
        in_specs=[pl.BlockSpec((tm, d), lambda i: (i, 0)),
                  pl.BlockSpec((1, d), lambda i: (0, 0)),
                  pl.BlockSpec((1, d), lambda i: (0, 0))],
        out_specs=[pl.BlockSpec((tm, d), lambda i: (i, 0)),
                   pl.BlockSpec((tm, d), lambda i: (i, 0))],
        out_shape=[jax.ShapeDtypeStruct((t, d), F32), jax.ShapeDtypeStruct((t, d), BF16)],
        compiler_params=_cparams(("parallel",)),
        name="ln_entry",
    )(x, g.reshape(1, d), b.reshape(1, d))


def _inproj_body(x_ref, w_ref, ws_ref, h_ref, hs_ref):
    x = x_ref[...]
    h_ref[...] = _dot(x, w_ref[...]).astype(BF16)

    @pl.when(pl.program_id(1) == 0)
    def _():
        hs_ref[...] = _dot(x, ws_ref[...])


def in_projection(xb, w_main, w_small, tm=1024, tn=512):
    t, d = xb.shape
    n = w_main.shape[1]
    return pl.pallas_call(
        _inproj_body,
        grid=(t // tm, n // tn),
        in_specs=[pl.BlockSpec((tm, d), lambda i, j: (i, 0)),
                  pl.BlockSpec((d, tn), lambda i, j: (0, j)),
                  pl.BlockSpec((d, LANES), lambda i, j: (0, 0))],
        out_specs=[pl.BlockSpec((tm, tn), lambda i, j: (i, j)),
                   pl.BlockSpec((tm, LANES), lambda i, j: (i, 0))],
        out_shape=[jax.ShapeDtypeStruct((t, n), BF16), jax.ShapeDtypeStruct((t, LANES), F32)],
        compiler_params=_cparams(("parallel", "arbitrary")),
        name="in_proj",
    )(xb, w_main, w_small)


def _lane_pick(x, lane):
    idx = lax.broadcasted_iota(jnp.int32, x.shape, 1)
    return jnp.sum(jnp.where(idx == lane, x, 0.0), axis=1, keepdims=True)


def _softplus(x):
    return jnp.maximum(x, 0.0) + jnp.log1p(jnp.exp(-jnp.abs(x)))


def _group_masks():
    r = lax.broadcasted_iota(jnp.int32, (GROUP, GROUP), 0)
    c = lax.broadcasted_iota(jnp.int32, (GROUP, GROUP), 1)
    same = (r // CHUNK) == (c // CHUNK)
    causal = jnp.logical_and(same, r >= c)
    strict = jnp.logical_and(same, r > c)
    upper = jnp.logical_and(same, r <= c)
    return causal, strict, upper, r == c


def _split_bf16(x):
    hi = x.astype(BF16)
    return hi, (x - hi.astype(F32)).astype(BF16)


def _group_cumsum_col(col, low_b):
    hi, lo = _split_bf16(jnp.broadcast_to(col, (GROUP, LANES)))
    return (_dot(low_b, hi) + _dot(low_b, lo))[:, :1]


def _group_cumsum_row(row, up_b):
    hi, lo = _split_bf16(jnp.broadcast_to(row, (16, GROUP)))
    return (_dot(hi, up_b) + _dot(lo, up_b))[0:1, :]


def _rms_norm(x, g):
    return x * lax.rsqrt(jnp.mean(x * x, axis=-1, keepdims=True) + RMS_EPS) * g


def _gdn_body(alog_ref, dtb_ref, q_ref, k_ref, v_ref, z_ref, cq_ref, ck_ref, cv_ref,
              sm_ref, smt_ref, ng_ref, o_ref, state_ref, eq_ref, ek_ref, ev_ref):
    hh = pl.program_id(1)
    lb = q_ref.shape[0]

    @pl.when(pl.program_id(2) == 0)
    def _():
        state_ref[...] = jnp.zeros_like(state_ref)
        for e_ref in (eq_ref, ek_ref, ev_ref):
            e_ref[0:8, :] = jnp.zeros((8, HEAD_DIM), F32)

    def conv_silu(x_ref, w_ref, e_ref):
        e_ref[8:, :] = x_ref[...].astype(F32)
        w = w_ref[...]
        y = w[0:1, :] * e_ref[pl.ds(8 - CONV_WIDTH + 1, lb), :]
        for j in range(1, CONV_WIDTH):
            y = y + w[j:j + 1, :] * e_ref[pl.ds(8 - CONV_WIDTH + 1 + j, lb), :]
        e_ref[0:8, :] = e_ref[lb:lb + 8, :]
        return y * _sigmoid(y)

    q = conv_silu(q_ref, cq_ref, eq_ref)
    k = conv_silu(k_ref, ck_ref, ek_ref)
    v = conv_silu(v_ref, cv_ref, ev_ref)
    q = q * lax.rsqrt(jnp.sum(q * q, axis=-1, keepdims=True) + RMS_EPS) * (HEAD_DIM ** -0.5)
    k = k * lax.rsqrt(jnp.sum(k * k, axis=-1, keepdims=True) + RMS_EPS)

    neg_a = -jnp.exp(jnp.full((1, 1), alog_ref[hh], F32))
    dtb = dtb_ref[hh]
    sm = sm_ref[...]
    g_col = neg_a * _softplus(_lane_pick(sm, hh) + dtb)
    beta_col = _sigmoid(_lane_pick(sm, HEADS + hh))
    g_row = neg_a * _softplus(smt_ref[pl.ds(hh, 1), :] + dtb)

    causal, strict, upper, diag = _group_masks()
    low_b = jnp.where(causal, 1.0, 0.0).astype(BF16)
    up_b = jnp.where(upper, 1.0, 0.0).astype(BF16)
    eye = jnp.where(diag, 1.0, 0.0).astype(F32)
    state = state_ref[...]
    z = z_ref[...].astype(F32)
    ng = ng_ref[...]
    for gi in range(lb // GROUP):
        gs = slice(gi * GROUP, (gi + 1) * GROUP)
        qg, kg, bg = q[gs], k[gs], beta_col[gs]
        gc_col = _group_cumsum_col(g_col[gs], low_b)
        gc_row = _group_cumsum_row(g_row[:, gs], up_b)
        decay = jnp.exp(jnp.where(causal, gc_col - gc_row, -jnp.inf))
        kg_b = kg.astype(BF16)
        kb = kg * bg
        a_mat = jnp.where(strict, _dot_nt(kb.astype(BF16), kg_b) * decay, 0.0)
        pw = -a_mat
        t_mat = eye + pw
        for _ in range(5):
            pw_b = pw.astype(BF16)
            pw = _dot(pw_b, pw_b)
            t_mat = t_mat + _dot(t_mat.astype(BF16), pw.astype(BF16))
        e_gc = jnp.exp(gc_col)
        rhs = jnp.concatenate([(v[gs] * bg).astype(BF16), (kb * e_gc).astype(BF16)], axis=1)
        uw = _dot(t_mat.astype(BF16), rhs).astype(BF16)
        qk = (_dot_nt(qg.astype(BF16), kg_b) * decay).astype(BF16)
        qk_uw = _dot(qk, uw)
        o_intra = qk_uw[:, :HEAD_DIM]
        q_eff = qg * e_gc - qk_uw[:, HEAD_DIM:]
        for c in range(GROUP // CHUNK):
            sl = slice(c * CHUNK, (c + 1) * CHUNK)
            gc_last = gc_col[(c + 1) * CHUNK - 1:(c + 1) * CHUNK, :]
            k_tail = (kg[sl] * jnp.exp(gc_last - gc_col[sl])).astype(BF16)
            kt_uw = _dot_tn(k_tail, uw[sl])
            lhs = jnp.concatenate([q_eff[sl], kt_uw[:, HEAD_DIM:]], axis=0).astype(BF16)
            res = _dot(lhs, state.astype(BF16))
            out = res[:CHUNK] + o_intra[sl]
            state = state * jnp.exp(gc_last) - res[CHUNK:] + kt_uw[:, :HEAD_DIM]
            rows = slice(gi * GROUP + c * CHUNK, gi * GROUP + (c + 1) * CHUNK)
            zc = z[rows]
            o_ref[rows, :] = (_rms_norm(out, ng) * (zc * _sigmoid(zc))).astype(o_ref.dtype)
    state_ref[...] = state


def gated_deltanet(h, small, small_t, conv_w, a_log, dt_bias, norm_g, bsz, seq):
    t = h.shape[0]
    lb = SEQ_BLOCK
    nb = seq // lb

    def col(off):
        return pl.BlockSpec((lb, HEAD_DIM), lambda b, hh, s, *_: (b * nb + s, COL_GDN + off + hh))

    def conv(off):
        return pl.BlockSpec((CONV_WIDTH, HEAD_DIM), lambda b, hh, s, *_: (0, off + hh))

    grid_spec = pltpu.PrefetchScalarGridSpec(
        num_scalar_prefetch=2,
        grid=(bsz, HEADS, nb),
        in_specs=[col(0), col(HEADS), col(2 * HEADS), col(3 * HEADS),
                  conv(0), conv(HEADS), conv(2 * HEADS),
                  pl.BlockSpec((lb, LANES), lambda b, hh, s, *_: (b * nb + s, 0)),
                  pl.BlockSpec((4 * HEADS, lb), lambda b, hh, s, *_: (0, b * nb + s)),
                  pl.BlockSpec((1, HEAD_DIM), lambda b, hh, s, *_: (0, 0))],
        out_specs=pl.BlockSpec((lb, HEAD_DIM), lambda b, hh, s, *_: (b * nb + s, hh)),
        scratch_shapes=[pltpu.VMEM((HEAD_DIM, HEAD_DIM), F32),
                        pltpu.VMEM((lb + 8, HEAD_DIM), F32),
                        pltpu.VMEM((lb + 8, HEAD_DIM), F32),
                        pltpu.VMEM((lb + 8, HEAD_DIM), F32)])
    return pl.pallas_call(
        _gdn_body,
        grid_spec=grid_spec,
        out_shape=jax.ShapeDtypeStruct((t, QK_WIDTH), BF16),
        compiler_params=_cparams(("parallel", "parallel", "arbitrary")),
        name="gated_deltanet",
    )(a_log, dt_bias, h, h, h, h, conv_w, conv_w, conv_w, small, small_t, norm_g.reshape(1, HEAD_DIM))


def _soft_cap(x):
    return GATE_CAP * jnp.tanh(x / GATE_CAP)


def _log_sigmoid(x):
    return jnp.minimum(x, 0.0) - jnp.log1p(jnp.exp(-jnp.abs(x)))


def _mlstm_body(gb_ref, q_ref, k_ref, v_ref, o_ref, sm_ref, smt_ref, ng_ref,
                y_ref, c_ref, n_ref, m_ref):
    hh = pl.program_id(1)
    lb = q_ref.shape[0]
    n_groups = lb // GROUP
    per_group = GROUP // CHUNK

    @pl.when(pl.program_id(2) == 0)
    def _():
        c_ref[...] = jnp.zeros_like(c_ref)
        n_ref[...] = jnp.zeros_like(n_ref)
        m_ref[...] = jnp.zeros_like(m_ref)

    q = q_ref[...]
    k = k_ref[...].astype(F32) * (HEAD_DIM ** -0.5)
    v = v_ref[...]
    gb_i = gb_ref[hh]
    gb_f = gb_ref[HEADS + hh]
    sm = sm_ref[...]
    i_col = _soft_cap(_lane_pick(sm, 2 * HEADS + hh) + gb_i)
    f_col = _log_sigmoid(_soft_cap(_lane_pick(sm, 3 * HEADS + hh) + gb_f))
    i_row = _soft_cap(smt_ref[pl.ds(2 * HEADS + hh, 1), :] + gb_i)
    f_row = _log_sigmoid(_soft_cap(smt_ref[pl.ds(3 * HEADS + hh, 1), :] + gb_f))

    causal, _, upper, _ = _group_masks()
    low_b = jnp.where(causal, 1.0, 0.0).astype(BF16)
    up_b = jnp.where(upper, 1.0, 0.0).astype(BF16)

    bc_cols, lkws, b_lasts, lkw_maxes = [], [], [], []
    for gi in range(n_groups):
        gs = slice(gi * GROUP, (gi + 1) * GROUP)
        bc_col = _group_cumsum_col(f_col[gs], low_b)
        bc_cols.append(bc_col)
        for c in range(per_group):
            sl = slice(c * CHUNK, (c + 1) * CHUNK)
            b_last = bc_col[(c + 1) * CHUNK - 1:(c + 1) * CHUNK, :]
            lkw = b_last - bc_col[sl] + i_col[gi * GROUP + c * CHUNK:gi * GROUP + (c + 1) * CHUNK]
            b_lasts.append(b_last)
            lkws.append(lkw)
            lkw_maxes.append(jnp.max(lkw, axis=0, keepdims=True))
    m_st = m_ref[...][:, :1]
    m_prev, m_next, carry_decay = [], [], []
    for ci in range(n_groups * per_group):
        m_new = jnp.maximum(b_lasts[ci] + m_st, lkw_maxes[ci])
        m_prev.append(m_st)
        m_next.append(m_new)
        carry_decay.append(jnp.exp(b_lasts[ci] + m_st - m_new))
        m_st = m_new

    c_st = c_ref[...]
    n_st = n_ref[...]
    o_pre = o_ref[...].astype(F32)
    ng = ng_ref[...]
    for gi in range(n_groups):
        gs = slice(gi * GROUP, (gi + 1) * GROUP)
        qg, vg = q[gs], v[gs]
        kg_b = k[gs].astype(BF16)
        bc_col = bc_cols[gi]
        bc_row = _group_cumsum_row(f_row[:, gs], up_b)
        log_d = jnp.where(causal, bc_col - bc_row + i_row[:, gs], -jnp.inf)
        m_prev_col = jnp.concatenate(
            [jnp.broadcast_to(m_prev[gi * per_group + c], (CHUNK, 1)) for c in range(per_group)], axis=0)
        log_inter = bc_col + m_prev_col
        m_t = jnp.maximum(log_inter, jnp.max(log_d, axis=-1, keepdims=True))
        w_inter = jnp.exp(log_inter - m_t)
        s = _dot_nt(qg, kg_b) * jnp.exp(log_d - m_t)
        s_v = _dot(s.astype(BF16), vg)
        s_sum = jnp.sum(s, axis=-1, keepdims=True)
        floor = jnp.exp(-m_t)
        for c in range(per_group):
            ci = gi * per_group + c
            sl = slice(c * CHUNK, (c + 1) * CHUNK)
            rows = slice(gi * GROUP + c * CHUNK, gi * GROUP + (c + 1) * CHUNK)
            qc = qg[sl]
            num = w_inter[sl] * _dot(qc, c_st.astype(BF16)) + s_v[sl]
            den = w_inter[sl] * jnp.sum(qc.astype(F32) * n_st, axis=-1, keepdims=True) + s_sum[sl]
            hid = num / jnp.maximum(jnp.abs(den), floor[sl])
            kw = k[rows] * jnp.exp(lkws[ci] - m_next[ci])
            c_st = carry_decay[ci] * c_st + _dot_tn(kw.astype(BF16), vg[sl])
            n_st = carry_decay[ci] * n_st + jnp.sum(kw, axis=0, keepdims=True)
            y_ref[rows, :] = (_sigmoid(o_pre[rows]) * _rms_norm(hid, ng)).astype(y_ref.dtype)
    c_ref[...] = c_st
    n_ref[...] = n_st
    m_ref[...] = jnp.broadcast_to(m_st, m_ref.shape)


def mlstm(h, small, small_t, gate_bias, norm_g, bsz, seq):
    t = h.shape[0]
    lb = SEQ_BLOCK
    nb = seq // lb

    def col(off):
        return pl.BlockSpec((lb, HEAD_DIM), lambda b, hh, s, *_: (b * nb + s, COL_MLSTM + off + hh))

    grid_spec = pltpu.PrefetchScalarGridSpec(
        num_scalar_prefetch=1,
        grid=(bsz, HEADS, nb),
        in_specs=[col(0), col(HEADS), col(2 * HEADS), col(3 * HEADS),
                  pl.BlockSpec((lb, LANES), lambda b, hh, s, *_: (b * nb + s, 0)),
                  pl.BlockSpec((4 * HEADS, lb), lambda b, hh, s, *_: (0, b * nb + s)),
                  pl.BlockSpec((1, HEAD_DIM), lambda b, hh, s, *_: (0, hh))],
        out_specs=pl.BlockSpec((lb, HEAD_DIM), lambda b, hh, s, *_: (b * nb + s, hh)),
        scratch_shapes=[pltpu.VMEM((HEAD_DIM, HEAD_DIM), F32),
                        pltpu.VMEM((1, HEAD_DIM), F32),
                        pltpu.VMEM((1, LANES), F32)])
    return pl.pallas_call(
        _mlstm_body,
        grid_spec=grid_spec,
        out_shape=jax.ShapeDtypeStruct((t, QK_WIDTH), BF16),
        compiler_params=_cparams(("parallel", "parallel", "arbitrary")),
        name="mlstm",
    )(gate_bias, h, h, h, h, small, small_t, norm_g.reshape(1, QK_WIDTH))


def _group_constants():
    r = jnp.arange(GROUP, dtype=jnp.int32)[:, None]
    c = jnp.arange(GROUP, dtype=jnp.int32)[None, :]
    same = (r // CHUNK) == (c // CHUNK)
    neg = jnp.where(same & (r >= c), 0.0, -jnp.inf).astype(F32)
    strict = (same & (r > c)).astype(F32)
    eye = (r == c).astype(F32)
    upper = (same & (r <= c)).astype(BF16)
    last = (r == (c // CHUNK) * CHUNK + CHUNK - 1).astype(BF16)
    return jnp.stack([neg, strict, eye]), jnp.stack([upper, last, eye.astype(BF16)])


def _split3(x):
    hi = x.astype(BF16)
    r1 = x - hi.astype(F32)
    mid = r1.astype(BF16)
    return hi, mid, (r1 - mid.astype(F32)).astype(BF16)


def _rows_times(rows8, mat_b, terms):
    rows = jnp.concatenate([rows8, jnp.zeros_like(rows8)], axis=0)
    parts = _split3(rows)[:terms]
    out = _dot(parts[0], mat_b)
    for p in parts[1:]:
        out = out + _dot(p, mat_b)
    return out[0:8]


def _rows_to_cols(stack, eye_b):
    parts = _split3(stack)
    out = _dot_nt(eye_b, parts[0])
    for p in parts[1:]:
        out = out + _dot_nt(eye_b, p)
    return out


def _conv_silu(x_ref, w, e_ref):
    lb = x_ref.shape[0]
    e_ref[8:, :] = x_ref[...].astype(F32)
    y = w[0:1, :] * e_ref[pl.ds(8 - CONV_WIDTH + 1, lb), :]
    for j in range(1, CONV_WIDTH):
        y = y + w[j:j + 1, :] * e_ref[pl.ds(8 - CONV_WIDTH + 1 + j, lb), :]
    e_ref[0:8, :] = e_ref[lb:lb + 8, :]
    return y * _sigmoid(y)


def _gdn_heads_body(q_ref, k_ref, v_ref, z_ref, cw_ref, smt_ref, gp_ref, ng_ref, fm_ref, bm_ref,
                    o_ref, state_ref, eq_ref, ek_ref, ev_ref):
    lb = q_ref.shape[0]

    @pl.when(pl.program_id(1) == 0)
    def _():
        state_ref[...] = jnp.zeros_like(state_ref)
        for e_ref in (eq_ref, ek_ref, ev_ref):
            e_ref[0:8, :] = jnp.zeros((8, QK_WIDTH), F32)

    cw = cw_ref[...]
    q_all = _conv_silu(q_ref, cw[:, 0:QK_WIDTH], eq_ref)
    k_all = _conv_silu(k_ref, cw[:, QK_WIDTH:2 * QK_WIDTH], ek_ref)
    v_all = _conv_silu(v_ref, cw[:, 2 * QK_WIDTH:3 * QK_WIDTH], ev_ref)
    z_all = z_ref[...].astype(F32)
    ng = ng_ref[...]
    neg, strict01, eye = fm_ref[0], fm_ref[1], fm_ref[2]
    up_b, last_b, eye_b = bm_ref[0], bm_ref[1], bm_ref[2]

    gp = gp_ref[...]
    neg_a8 = -jnp.exp(gp[0:8, 0:1])
    dtb8 = gp[8:16, 0:1]
    g8 = neg_a8 * _softplus(smt_ref[0:8, :] + dtb8)
    beta8 = _sigmoid(smt_ref[HEADS:HEADS + 8, :])
    gc8 = _rows_times(g8, up_b, 2)
    gl8 = _rows_times(gc8, last_b, 3)
    egc8 = jnp.exp(gc8)
    tail8 = jnp.exp(gl8 - gc8)
    elast8 = jnp.exp(gl8)
    stack = jnp.concatenate([gc8, beta8, egc8, tail8, beta8 * egc8,
                             jnp.zeros((LANES - 40, lb), F32)], axis=0)
    cols = _rows_to_cols(stack, eye_b)

    hds = range(HEADS)
    lanes = [slice(hd * HEAD_DIM, (hd + 1) * HEAD_DIM) for hd in hds]
    col = lambda j, hd: cols[:, 8 * j + hd:8 * j + hd + 1]
    qg = [q_all[:, lanes[hd]] for hd in hds]
    kg = [k_all[:, lanes[hd]] for hd in hds]
    qg = [x * lax.rsqrt(jnp.sum(x * x, axis=-1, keepdims=True) + RMS_EPS) * (HEAD_DIM ** -0.5) for x in qg]
    kg = [x * lax.rsqrt(jnp.sum(x * x, axis=-1, keepdims=True) + RMS_EPS) for x in kg]
    kg_b = [x.astype(BF16) for x in kg]
    decay = [jnp.exp(col(0, hd) - gc8[hd:hd + 1, :] + neg) for hd in hds]
    a_mat = [_dot_nt((kg[hd] * col(1, hd)).astype(BF16), kg_b[hd]) * decay[hd] * strict01 for hd in hds]
    pw = [-a for a in a_mat]
    t_mat = [eye + p for p in pw]
    for _ in range(5):
        pw_b = [p.astype(BF16) for p in pw]
        pw = [_dot(p, p) for p in pw_b]
        t_mat = [t_mat[hd] + _dot(t_mat[hd].astype(BF16), pw[hd].astype(BF16)) for hd in hds]
    rhs = [jnp.concatenate([(v_all[:, lanes[hd]] * col(1, hd)).astype(BF16),
                            (kg[hd] * col(4, hd)).astype(BF16)], axis=1) for hd in hds]
    uw = [_dot(t_mat[hd].astype(BF16), rhs[hd]).astype(BF16) for hd in hds]
    qk = [(_dot_nt(qg[hd].astype(BF16), kg_b[hd]) * decay[hd]).astype(BF16) for hd in hds]
    qk_uw = [_dot(qk[hd], uw[hd]) for hd in hds]
    q_eff = [qg[hd] * col(2, hd) - qk_uw[hd][:, HEAD_DIM:] for hd in hds]
    k_tail = [(kg[hd] * col(3, hd)).astype(BF16) for hd in hds]
    state = [state_ref[hd] for hd in hds]
    for c in range(lb // CHUNK):
        sl = slice(c * CHUNK, (c + 1) * CHUNK)
        kt_uw = [_dot_tn(k_tail[hd][sl], uw[hd][sl]) for hd in hds]
        lhs = [jnp.concatenate([q_eff[hd][sl], kt_uw[hd][:, HEAD_DIM:]], axis=0).astype(BF16) for hd in hds]
        res = [_dot(lhs[hd], state[hd].astype(BF16)) for hd in hds]
        state = [state[hd] * elast8[hd:hd + 1, c * CHUNK:c * CHUNK + 1] - res[hd][CHUNK:]
                 + kt_uw[hd][:, :HEAD_DIM] for hd in hds]
        for hd in hds:
            out = res[hd][:CHUNK] + qk_uw[hd][sl, :HEAD_DIM]
            zc = z_all[sl, lanes[hd]]
            o_ref[sl, lanes[hd]] = (_rms_norm(out, ng) * (zc * _sigmoid(zc))).astype(o_ref.dtype)
    for hd in hds:
        state_ref[hd] = state[hd]


def _gate_params(first, second):
    out = jnp.zeros((16, LANES), F32)
    out = out.at[0:HEADS, :].set(jnp.broadcast_to(first.astype(F32)[:, None], (HEADS, LANES)))
    return out.at[8:8 + HEADS, :].set(jnp.broadcast_to(second.astype(F32)[:, None], (HEADS, LANES)))


def gated_deltanet_heads(h, small_t, conv_w, a_log, dt_bias, norm_g, fmasks, bmasks, bsz, seq):
    t = h.shape[0]
    lb = GROUP
    nb = seq // lb
    wide = QK_WIDTH
    first = COL_GDN * LANES // wide

    def col(j):
        return pl.BlockSpec((lb, wide), lambda b, s: (b * nb + s, first + j))

    fixed2 = lambda b, s: (0, 0)
    fixed3 = lambda b, s: (0, 0, 0)
    return pl.pallas_call(
        _gdn_heads_body,
        grid=(bsz, nb),
        in_specs=[col(0), col(1), col(2), col(3),
                  pl.BlockSpec((CONV_WIDTH, 3 * wide), fixed2),
                  pl.BlockSpec((4 * HEADS, lb), lambda b, s: (0, b * nb + s)),
                  pl.BlockSpec((16, LANES), fixed2),
                  pl.BlockSpec((1, HEAD_DIM), fixed2),
                  pl.BlockSpec((3, GROUP, GROUP), fixed3),
                  pl.BlockSpec((3, GROUP, GROUP), fixed3)],
        out_specs=pl.BlockSpec((lb, wide), lambda b, s: (b * nb + s, 0)),
        out_shape=jax.ShapeDtypeStruct((t, wide), BF16),
        scratch_shapes=[pltpu.VMEM((HEADS, HEAD_DIM, HEAD_DIM), F32),
                        pltpu.VMEM((lb + 8, wide), F32),
                        pltpu.VMEM((lb + 8, wide), F32),
                        pltpu.VMEM((lb + 8, wide), F32)],
        compiler_params=_cparams(("parallel", "arbitrary")),
        name="gated_deltanet",
    )(h, h, h, h, conv_w, small_t, _gate_params(a_log, dt_bias), norm_g.reshape(1, HEAD_DIM), fmasks, bmasks)


def _mlstm_heads_body(q_ref, k_ref, v_ref, o_ref, smt_ref, gp_ref, ng_ref, fm_ref, bm_ref,
                      y_ref, c_ref, n_ref, m_ref):
    lb = q_ref.shape[0]
    n_chunks = lb // CHUNK

    @pl.when(pl.program_id(1) == 0)
    def _():
        c_ref[...] = jnp.zeros_like(c_ref)
        n_ref[...] = jnp.zeros_like(n_ref)
        m_ref[...] = jnp.zeros_like(m_ref)

    hds = range(HEADS)
    lanes = [slice(hd * HEAD_DIM, (hd + 1) * HEAD_DIM) for hd in hds]
    q_all = q_ref[...]
    k_all = k_ref[...].astype(F32) * (HEAD_DIM ** -0.5)
    qg = [q_all[:, lanes[hd]] for hd in hds]
    kg = [k_all[:, lanes[hd]] for hd in hds]
    qk = [_dot_nt(qg[hd], kg[hd].astype(BF16)) for hd in hds]

    neg = fm_ref[0]
    up_b, last_b, eye_b = bm_ref[0], bm_ref[1], bm_ref[2]
    capped = _soft_cap(smt_ref[2 * HEADS:4 * HEADS, :] + gp_ref[...][0:8, 0:1])
    i8 = capped
    f8 = pltpu.roll(_log_sigmoid(capped), HEADS, axis=0)
    bc8 = _rows_times(f8, up_b, 2)
    bl8 = _rows_times(bc8, last_b, 3)
    lkw8 = bl8 - bc8 + i8
    chunk_id = lax.broadcasted_iota(jnp.int32, (8, lb), 1) // CHUNK
    m_st = m_ref[...][:, 0:1]
    m_prev_row = jnp.zeros((8, lb), F32)
    m_next_row = jnp.zeros((8, lb), F32)
    carry_decay = []
    for c in range(n_chunks):
        in_c = chunk_id == c
        b_last = bl8[:, c * CHUNK:c * CHUNK + 1]
        m_new = jnp.maximum(b_last + m_st, jnp.max(jnp.where(in_c, lkw8, -jnp.inf), axis=1, keepdims=True))
        carry_decay.append(jnp.exp(b_last + m_st - m_new))
        m_prev_row = jnp.where(in_c, m_st, m_prev_row)
        m_next_row = jnp.where(in_c, m_new, m_next_row)
        m_st = m_new
    m_ref[...] = jnp.broadcast_to(m_st, m_ref.shape)
    stack = jnp.concatenate([bc8, bc8 + m_prev_row, jnp.exp(lkw8 - m_next_row),
                             jnp.zeros((LANES - 24, lb), F32)], axis=0)
    cols = _rows_to_cols(stack, eye_b)

    v_all = v_ref[...]
    o_all = o_ref[...].astype(F32)
    ng_all = ng_ref[...]
    n_all = n_ref[...]
    col = lambda j, hd: cols[:, 8 * j + hd:8 * j + hd + 1]
    vg = [v_all[:, lanes[hd]] for hd in hds]
    log_d = [col(0, hd) - bc8[hd:hd + 1, :] + i8[hd:hd + 1, :] + neg for hd in hds]
    m_t = [jnp.maximum(col(1, hd), jnp.max(log_d[hd], axis=-1, keepdims=True)) for hd in hds]
    w_inter = [jnp.exp(col(1, hd) - m_t[hd]) for hd in hds]
    s = [qk[hd] * jnp.exp(log_d[hd] - m_t[hd]) for hd in hds]
    ones = jnp.ones((lb, HEAD_DIM), BF16)
    s_vx = [_dot(s[hd].astype(BF16), jnp.concatenate([vg[hd], ones], axis=1)) for hd in hds]
    s_v = [x[:, :HEAD_DIM] for x in s_vx]
    s_sum = [x[:, HEAD_DIM:HEAD_DIM + 1] for x in s_vx]
    floor = [jnp.exp(-m_t[hd]) for hd in hds]
    kw = [kg[hd] * col(2, hd) for hd in hds]
    kw_b = [x.astype(BF16) for x in kw]
    c_st = [c_ref[hd] for hd in hds]
    n_st = [n_all[hd:hd + 1, :] for hd in hds]
    for c in range(n_chunks):
        sl = slice(c * CHUNK, (c + 1) * CHUNK)
        q_c = [_dot(qg[hd][sl], c_st[hd].astype(BF16)) for hd in hds]
        q_n = [jnp.sum(qg[hd][sl].astype(F32) * n_st[hd], axis=-1, keepdims=True) for hd in hds]
        cd = [carry_decay[c][hd:hd + 1, :] for hd in hds]
        c_st = [cd[hd] * c_st[hd] + _dot_tn(kw_b[hd][sl], vg[hd][sl]) for hd in hds]
        n_st = [cd[hd] * n_st[hd] + jnp.sum(kw[hd][sl], axis=0, keepdims=True) for hd in hds]
        for hd in hds:
            num = w_inter[hd][sl] * q_c[hd] + s_v[hd][sl]
            den = w_inter[hd][sl] * q_n[hd] + s_sum[hd][sl]
            hid = num / jnp.maximum(jnp.abs(den), floor[hd][sl])
            y_ref[sl, lanes[hd]] = (_sigmoid(o_all[sl, lanes[hd]])
                                    * _rms_norm(hid, ng_all[:, lanes[hd]])).astype(y_ref.dtype)
    for hd in hds:
        c_ref[hd] = c_st[hd]
        n_ref[hd:hd + 1, :] = n_st[hd]


def mlstm_heads(h, small_t, gate_bias, norm_g, fmasks, bmasks, bsz, seq):
    t = h.shape[0]
    lb = GROUP
    nb = seq // lb
    wide = QK_WIDTH
    first = COL_MLSTM * LANES // wide

    def col(j):
        return pl.BlockSpec((lb, wide), lambda b, s: (b * nb + s, first + j))

    fixed2 = lambda b, s: (0, 0)
    fixed3 = lambda b, s: (0, 0, 0)
    gp = jnp.zeros((16, LANES), F32).at[0:2 * HEADS, :].set(
        jnp.broadcast_to(gate_bias.astype(F32)[:, None], (2 * HEADS, LANES)))
    return pl.pallas_call(
        _mlstm_heads_body,
        grid=(bsz, nb),
        in_specs=[col(0), col(1), col(2), col(3),
                  pl.BlockSpec((4 * HEADS, lb), lambda b, s: (0, b * nb + s)),
                  pl.BlockSpec((16, LANES), fixed2),
                  pl.BlockSpec((1, wide), fixed2),
                  pl.BlockSpec((3, GROUP, GROUP), fixed3),
                  pl.BlockSpec((3, GROUP, GROUP), fixed3)],
        out_specs=pl.BlockSpec((lb, wide), lambda b, s: (b * nb + s, 0)),
        out_shape=jax.ShapeDtypeStruct((t, wide), BF16),
        scratch_shapes=[pltpu.VMEM((HEADS, HEAD_DIM, HEAD_DIM), F32),
                        pltpu.VMEM((8, HEAD_DIM), F32),
                        pltpu.VMEM((8, LANES), F32)],
        compiler_params=_cparams(("parallel", "arbitrary")),
        name="mlstm",
    )(h, h, h, h, small_t, gp, norm_g.reshape(1, wide), fmasks, bmasks)


def _rope_table_body(pos_ref, freq_ref, sign_ref, cc_ref, ss_ref):
    ang = pos_ref[...] * freq_ref[...]
    sign = sign_ref[...]
    cc_ref[...] = jnp.cos(ang) * jnp.abs(sign)
    ss_ref[...] = jnp.sin(ang) * sign


def rope_tables(positions, tm=512):
    t = positions.size
    half = MLA_ROPE // 2
    inv_freq = 1.0 / (ROPE_THETA ** (jnp.arange(0, MLA_ROPE, 2, dtype=F32) / MLA_ROPE))
    zeros = jnp.zeros((LANES - MLA_ROPE,), F32)
    freq = jnp.concatenate([inv_freq, inv_freq, zeros]).reshape(1, LANES)
    sign = jnp.concatenate([-jnp.ones((half,), F32), jnp.ones((half,), F32), zeros]).reshape(1, LANES)
    return pl.pallas_call(
        _rope_table_body,
        grid=(t // tm,),
        in_specs=[pl.BlockSpec((tm, 1), lambda i: (i, 0)),
                  pl.BlockSpec((1, LANES), lambda i: (0, 0)),
                  pl.BlockSpec((1, LANES), lambda i: (0, 0))],
        out_specs=[pl.BlockSpec((tm, LANES), lambda i: (i, 0)),
                   pl.BlockSpec((tm, LANES), lambda i: (i, 0))],
        out_shape=[jax.ShapeDtypeStruct((t, LANES), F32), jax.ShapeDtypeStruct((t, LANES), F32)],
        compiler_params=_cparams(("parallel",)),
        name="rope_tables",
    )(positions.astype(F32).reshape(t, 1), freq, sign)


def _mla_pre_body(h_ref, cc_ref, ss_ref, qg_ref, kvg_ref, wqa_ref, wqb_ref, wkv_ref,
                  q_ref, kn_ref, kr_ref, v_ref):
    hblk = h_ref[...].astype(F32)
    cc = cc_ref[...]
    ss = ss_ref[...]
    cq = _rms_norm(hblk[:, :MLA_Q_LORA], qg_ref[...]).astype(BF16)
    ckv = _rms_norm(hblk[:, MLA_Q_LORA:MLA_Q_LORA + MLA_KV_LORA], kvg_ref[...]).astype(BF16)
    off = MLA_Q_LORA + MLA_KV_LORA
    kr_ref[...] = (hblk[:, off:off + LANES] * cc + hblk[:, off + LANES:off + 2 * LANES] * ss).astype(BF16)
    kv = _dot(ckv, wkv_ref[...])
    kn_ref[...] = kv[:, :QK_WIDTH].astype(BF16)
    v_ref[...] = kv[:, QK_WIDTH:].astype(BF16)
    qa = _dot(cq, wqa_ref[...])
    qb = _dot(cq, wqb_ref[...])
    scale = (HEAD_DIM + MLA_ROPE) ** -0.5
    for hh in range(HEADS):
        base = 2 * HEAD_DIM * hh
        q_ref[:, base:base + HEAD_DIM] = (qa[:, base:base + HEAD_DIM] * scale).astype(BF16)
        rope = qa[:, base + HEAD_DIM:base + 2 * HEAD_DIM] * cc + qb[:, hh * LANES:(hh + 1) * LANES] * ss
        q_ref[:, base + HEAD_DIM:base + 2 * HEAD_DIM] = (rope * scale).astype(BF16)


def mla_prepare(h, cc, ss, q_norm_g, kv_norm_g, wq_a, wq_b, wkv, tm=512):
    t = h.shape[0]
    row = lambda i: (i, 0)
    fixed = lambda i: (0, 0)
    return pl.pallas_call(
        _mla_pre_body,
        grid=(t // tm,),
        in_specs=[pl.BlockSpec((tm, 1024), row),
                  pl.BlockSpec((tm, LANES), row),
                  pl.BlockSpec((tm, LANES), row),
                  pl.BlockSpec((1, MLA_Q_LORA), fixed),
                  pl.BlockSpec((1, MLA_KV_LORA), fixed),
                  pl.BlockSpec(wq_a.shape, fixed),
                  pl.BlockSpec(wq_b.shape, fixed),
                  pl.BlockSpec(wkv.shape, fixed)],
        out_specs=[pl.BlockSpec((tm, 2 * QK_WIDTH), row),
                   pl.BlockSpec((tm, QK_WIDTH), row),
                   pl.BlockSpec((tm, LANES), row),
                   pl.BlockSpec((tm, QK_WIDTH), row)],
        out_shape=[jax.ShapeDtypeStruct((t, 2 * QK_WIDTH), BF16),
                   jax.ShapeDtypeStruct((t, QK_WIDTH), BF16),
                   jax.ShapeDtypeStruct((t, LANES), BF16),
                   jax.ShapeDtypeStruct((t, QK_WIDTH), BF16)],
        compiler_params=_cparams(("parallel",)),
        name="mla_prepare",
    )(h, cc, ss, q_norm_g.reshape(1, -1), kv_norm_g.reshape(1, -1), wq_a, wq_b, wkv)


def _attn_body(q_ref, kn_ref, kr_ref, v_ref, o_ref, *, tk):
    qi = pl.program_id(2)
    q = q_ref[...]
    tq = q.shape[0]

    def step(j, carry, masked):
        m, l, acc = carry
        start = pl.multiple_of(j * tk, tk)
        kk = jnp.concatenate([kn_ref[pl.ds(start, tk), :], kr_ref[pl.ds(start, tk), :]], axis=1)
        s = _dot_nt(q, kk)
        if masked:
            r = lax.broadcasted_iota(jnp.int32, (tq, tk), 0)
            c = lax.broadcasted_iota(jnp.int32, (tq, tk), 1)
            s = jnp.where(r >= c, s, -jnp.inf)
        m_new = jnp.maximum(m, jnp.max(s, axis=-1, keepdims=True))
        p = jnp.exp(s - m_new)
        alpha = jnp.exp(m - m_new)
        l = alpha * l + jnp.sum(p, axis=-1, keepdims=True)
        acc = alpha * acc + _dot(p.astype(BF16), v_ref[pl.ds(start, tk), :])
        return m_new, l, acc

    init = (jnp.full((tq, 1), -jnp.inf, F32), jnp.zeros((tq, 1), F32), jnp.zeros((tq, HEAD_DIM), F32))
    carry = lax.fori_loop(0, qi, lambda j, cr: step(j, cr, False), init)
    _, l, acc = step(qi, carry, True)
    o_ref[...] = (acc / l).astype(o_ref.dtype)


def latent_attention(q, kn, kr, v, bsz, seq, tq=512):
    t = q.shape[0]
    nq = seq // tq
    return pl.pallas_call(
        functools.partial(_attn_body, tk=tq),
        grid=(bsz, HEADS, nq),
        in_specs=[pl.BlockSpec((tq, 2 * HEAD_DIM), lambda b, hh, i: (b * nq + i, hh)),
                  pl.BlockSpec((seq, HEAD_DIM), lambda b, hh, i: (b, hh)),
                  pl.BlockSpec((seq, LANES), lambda b, hh, i: (b, 0)),
                  pl.BlockSpec((seq, HEAD_DIM), lambda b, hh, i: (b, hh))],
        out_specs=pl.BlockSpec((tq, HEAD_DIM), lambda b, hh, i: (b * nq + i, hh)),
        out_shape=jax.ShapeDtypeStruct((t, QK_WIDTH), BF16),
        compiler_params=_cparams(("parallel", "parallel", "arbitrary")),
        name="latent_attention",
    )(q, kn, kr, v)


def _merge_body(yg_ref, ym_ref, ya_ref, g0_ref, g1_ref, g2_ref, gb_ref, pg_ref, pm_ref, pa_ref,
                wo_ref, x_ref, ln_g_ref, ln_b_ref, of_ref, ob_ref):
    gb = gb_ref[...]

    def branch(y_ref, p_ref, g_ref, idx):
        gate = _sigmoid(g_ref[...].astype(F32) + gb[:, idx * D_MODEL:(idx + 1) * D_MODEL])
        return gate * _dot(y_ref[...], p_ref[...])

    merged = branch(yg_ref, pg_ref, g0_ref, 0) + branch(ym_ref, pm_ref, g1_ref, 1) \
        + branch(ya_ref, pa_ref, g2_ref, 2)
    mix = _dot(merged.astype(BF16), wo_ref[...])
    y = _layer_norm(DEEPNORM_ALPHA * x_ref[...] + mix, ln_g_ref[...], ln_b_ref[...])
    of_ref[...] = y
    ob_ref[...] = y.astype(BF16)


def merge_branches(y_gdn, y_mlstm, y_mla, h, gate_bias, p_gdn, p_mlstm, p_mla, w_out, x, ln_g, ln_b, tm=512):
    t, d = x.shape
    row = lambda i: (i, 0)
    fixed = lambda i: (0, 0)
    ybs = pl.BlockSpec((tm, QK_WIDTH), row)
    pbs = pl.BlockSpec((QK_WIDTH, d), fixed)
    return pl.pallas_call(
        _merge_body,
        grid=(t // tm,),
        in_specs=[ybs, ybs, ybs,
                  pl.BlockSpec((tm, d), lambda i: (i, COL_GATE)),
                  pl.BlockSpec((tm, d), lambda i: (i, COL_GATE + 1)),
                  pl.BlockSpec((tm, d), lambda i: (i, COL_GATE + 2)),
                  pl.BlockSpec((1, 3 * d), fixed),
                  pbs, pbs, pbs,
                  pl.BlockSpec((d, d), fixed),
                  pl.BlockSpec((tm, d), row),
                  pl.BlockSpec((1, d), fixed),
                  pl.BlockSpec((1, d), fixed)],
        out_specs=[pl.BlockSpec((tm, d), row), pl.BlockSpec((tm, d), row)],
        out_shape=[jax.ShapeDtypeStruct((t, d), F32), jax.ShapeDtypeStruct((t, d), BF16)],
        compiler_params=_cparams(("parallel",)),
        name="merge_branches",
    )(y_gdn, y_mlstm, y_mla, h, h, h, gate_bias.reshape(1, 3 * d), p_gdn, p_mlstm, p_mla, w_out, x,
      ln_g.reshape(1, d), ln_b.reshape(1, d))


def _top2_sum(a, b, c, d):
    hi1, lo1 = jnp.maximum(a, b), jnp.minimum(a, b)
    hi2, lo2 = jnp.maximum(c, d), jnp.minimum(c, d)
    return jnp.maximum(hi1, hi2) + jnp.maximum(jnp.minimum(hi1, hi2), jnp.maximum(lo1, lo2))


def _router_body(x_ref, rwt_ref, rb_ref, su_ref, e_ref, w_ref, rank_ref, cnt_ref, carry_ref):
    @pl.when(pl.program_id(0) == 0)
    def _():
        carry_ref[...] = jnp.zeros_like(carry_ref)

    logits = _dot_nt(rwt_ref[...], x_ref[...], lax.Precision.HIGHEST)
    scores = _sigmoid(logits)
    biased = scores + rb_ref[...][:, :1]
    tm = logits.shape[1]
    brow = [biased[e:e + 1, :] for e in range(N_EXPERTS)]
    srow = [scores[e:e + 1, :] for e in range(N_EXPERTS)]
    best = _top2_sum(*brow[0:EXPERTS_PER_GROUP])
    grp = jnp.zeros((1, tm), jnp.int32)
    for g in range(1, N_GROUPS):
        gs = _top2_sum(*brow[g * EXPERTS_PER_GROUP:(g + 1) * EXPERTS_PER_GROUP])
        upd = gs > best
        best = jnp.where(upd, gs, best)
        grp = jnp.where(upd, g, grp)

    def in_group(rows, j):
        out = rows[j]
        for g in range(1, N_GROUPS):
            out = jnp.where(grp == g, rows[g * EXPERTS_PER_GROUP + j], out)
        return out

    ib = [in_group(brow, j) for j in range(EXPERTS_PER_GROUP)]
    isc = [in_group(srow, j) for j in range(EXPERTS_PER_GROUP)]
    v1, i1, s1 = ib[0], jnp.zeros((1, tm), jnp.int32), isc[0]
    for j in range(1, EXPERTS_PER_GROUP):
        upd = ib[j] > v1
        v1 = jnp.where(upd, ib[j], v1)
        i1 = jnp.where(upd, j, i1)
        s1 = jnp.where(upd, isc[j], s1)
    v2 = jnp.full((1, tm), -jnp.inf, F32)
    i2 = jnp.zeros((1, tm), jnp.int32)
    s2 = jnp.zeros((1, tm), F32)
    for j in range(EXPERTS_PER_GROUP):
        upd = jnp.logical_and(i1 != j, ib[j] > v2)
        v2 = jnp.where(upd, ib[j], v2)
        i2 = jnp.where(upd, j, i2)
        s2 = jnp.where(upd, isc[j], s2)
    e1 = grp * EXPERTS_PER_GROUP + i1
    e2 = grp * EXPERTS_PER_GROUP + i2
    total = s1 + s2
    e_ref[0:1, :] = e1
    e_ref[1:2, :] = e2
    w_ref[0:1, :] = s1 / total
    w_ref[1:2, :] = s2 / total
    erow = lax.broadcasted_iota(jnp.int32, (N_EXPERTS, tm), 0)
    oh1 = jnp.where(erow == e1, 1.0, 0.0).astype(F32)
    oh2 = jnp.where(erow == e2, 1.0, 0.0).astype(F32)
    both = oh1 + oh2
    before = _dot(both.astype(BF16), su_ref[...]) + carry_ref[...][:, :1]
    rank_ref[0:1, :] = jnp.sum(oh1 * before, axis=0, keepdims=True).astype(jnp.int32)
    rank_ref[1:2, :] = jnp.sum(oh2 * before, axis=0, keepdims=True).astype(jnp.int32)
    carry = carry_ref[...] + jnp.sum(both, axis=1, keepdims=True)
    carry_ref[...] = carry
    cnt_ref[...] = carry.astype(jnp.int32)


def route_tokens(x, router_w, router_bias, tm=512):
    t, d = x.shape
    strict_upper = jnp.triu(jnp.ones((tm, tm), BF16), k=1)
    fixed = lambda i: (0, 0)
    tok = lambda i: (0, i)
    return pl.pallas_call(
        _router_body,
        grid=(t // tm,),
        in_specs=[pl.BlockSpec((tm, d), lambda i: (i, 0)),
                  pl.BlockSpec((N_EXPERTS, d), fixed),
                  pl.BlockSpec((N_EXPERTS, LANES), fixed),
                  pl.BlockSpec((tm, tm), fixed)],
        out_specs=[pl.BlockSpec((TOP_K, tm), tok), pl.BlockSpec((TOP_K, tm), tok),
                   pl.BlockSpec((TOP_K, tm), tok), pl.BlockSpec((N_EXPERTS, LANES), fixed)],
        out_shape=[jax.ShapeDtypeStruct((TOP_K, t), jnp.int32), jax.ShapeDtypeStruct((TOP_K, t), F32),
                   jax.ShapeDtypeStruct((TOP_K, t), jnp.int32),
                   jax.ShapeDtypeStruct((N_EXPERTS, LANES), jnp.int32)],
        scratch_shapes=[pltpu.VMEM((N_EXPERTS, LANES), F32)],
        compiler_params=_cparams(("arbitrary",)),
        name="route_tokens",
    )(x, router_w.T, jnp.broadcast_to(router_bias.reshape(N_EXPERTS, 1), (N_EXPERTS, LANES)), strict_upper)


def _row_copy(x_hbm, buf, sem, slot, row, tok):
    return pltpu.make_async_copy(x_hbm.at[pl.ds(tok, 1), :], buf.at[slot, pl.ds(row, 1), :], sem.at[slot])


def _expert_body(be_ref, rt_ref, x_hbm, w1_ref, w3_ref, w2_ref, y_ref, buf, sem):
    i = pl.program_id(0)
    n = pl.num_programs(0)

    def issue(blk, slot):
        base = blk * EXPERT_BLOCK

        def one(r, carry):
            _row_copy(x_hbm, buf, sem, slot, r, rt_ref[base + r]).start()
            return carry

        lax.fori_loop(0, EXPERT_BLOCK, one, 0, unroll=8)

    @pl.when(i == 0)
    def _():
        issue(0, 0)

    @pl.when(i + 1 < n)
    def _():
        issue(i + 1, (i + 1) % 2)

    slot = i % 2

    def wait_one(r, carry):
        _row_copy(x_hbm, buf, sem, slot, r, 0).wait()
        return carry

    lax.fori_loop(0, EXPERT_BLOCK, wait_one, 0, unroll=8)
    x = buf[slot].astype(BF16)
    h1 = _dot(x, w1_ref[...])
    h3 = _dot(x, w3_ref[...])
    act = (h1 * _sigmoid(h1) * h3).astype(BF16)
    y_ref[...] = _dot(act, w2_ref[...])


def expert_ffn(block_expert, row_tok, x, w1, w3, w2):
    t, d = x.shape
    n_rows = row_tok.shape[0]
    n_blocks = n_rows // EXPERT_BLOCK
    grid_spec = pltpu.PrefetchScalarGridSpec(
        num_scalar_prefetch=2,
        grid=(n_blocks,),
        in_specs=[pl.BlockSpec(memory_space=pl.ANY),
                  pl.BlockSpec((None, d, D_EXPERT), lambda i, be, rt: (be[i], 0, 0)),
                  pl.BlockSpec((None, d, D_EXPERT), lambda i, be, rt: (be[i], 0, 0)),
                  pl.BlockSpec((None, D_EXPERT, d), lambda i, be, rt: (be[i], 0, 0))],
        out_specs=pl.BlockSpec((EXPERT_BLOCK, d), lambda i, be, rt: (i, 0)),
        scratch_shapes=[pltpu.VMEM((2, EXPERT_BLOCK, d), F32),
                        pltpu.SemaphoreType.DMA((2,))])
    return pl.pallas_call(
        _expert_body,
        grid_spec=grid_spec,
        out_shape=jax.ShapeDtypeStruct((n_rows, d), F32),
        compiler_params=_cparams(("arbitrary",)),
        name="expert_ffn",
    )(block_expert, row_tok, x, w1, w3, w2)


def _pair_copy(y_hbm, buf, sem, slot, k, row, src):
    return pltpu.make_async_copy(y_hbm.at[pl.ds(src, 1), :], buf.at[slot, k, pl.ds(row, 1), :], sem.at[slot])


def _combine_body(dest_ref, y_hbm, w_ref, x_ref, ln_g_ref, ln_b_ref, of_ref, ob_ref, buf, sem, *, n_tok):
    i = pl.program_id(0)
    n = pl.num_programs(0)
    tm = x_ref.shape[0]

    def issue(blk, slot):
        base = blk * tm

        def one(r, carry):
            for k in range(TOP_K):
                _pair_copy(y_hbm, buf, sem, slot, k, r, dest_ref[k * n_tok + base + r]).start()
            return carry

        lax.fori_loop(0, tm, one, 0, unroll=8)

    @pl.when(i == 0)
    def _():
        issue(0, 0)

    @pl.when(i + 1 < n)
    def _():
        issue(i + 1, (i + 1) % 2)

    slot = i % 2

    def wait_one(r, carry):
        for k in range(TOP_K):
            _pair_copy(y_hbm, buf, sem, slot, k, r, 0).wait()
        return carry

    lax.fori_loop(0, tm, wait_one, 0, unroll=8)
    w = w_ref[...]
    ffn = w[:, 0:1] * buf[slot, 0] + w[:, 1:2] * buf[slot, 1]
    y = _layer_norm(DEEPNORM_ALPHA * x_ref[...] + ffn, ln_g_ref[...], ln_b_ref[...])
    of_ref[...] = y
    ob_ref[...] = y.astype(BF16)


def combine_experts(dest, y_rows, w_pad, x, ln_g, ln_b, tm=256):
    t, d = x.shape
    row = lambda i, *_: (i, 0)
    fixed = lambda i, *_: (0, 0)
    grid_spec = pltpu.PrefetchScalarGridSpec(
        num_scalar_prefetch=1,
        grid=(t // tm,),
        in_specs=[pl.BlockSpec(memory_space=pl.ANY),
                  pl.BlockSpec((tm, LANES), row),
                  pl.BlockSpec((tm, d), row),
                  pl.BlockSpec((1, d), fixed),
                  pl.BlockSpec((1, d), fixed)],
        out_specs=[pl.BlockSpec((tm, d), row), pl.BlockSpec((tm, d), row)],
        scratch_shapes=[pltpu.VMEM((2, TOP_K, tm, d), F32),
                        pltpu.SemaphoreType.DMA((2,))])
    return pl.pallas_call(
        functools.partial(_combine_body, n_tok=t),
        grid_spec=grid_spec,
        out_shape=[jax.ShapeDtypeStruct((t, d), F32), jax.ShapeDtypeStruct((t, d), BF16)],
        compiler_params=_cparams(("arbitrary",)),
        name="combine_experts",
    )(dest, y_rows, w_pad, x, ln_g.reshape(1, d), ln_b.reshape(1, d))


def routed_experts(xf, router_w, router_bias, w1, w3, w2, ln_g, ln_b):
    t, d = xf.shape
    expert, weight, rank, counts = route_tokens(xf, router_w, router_bias)
    counts = counts[:, 0]
    padded = (counts + EXPERT_BLOCK - 1) // EXPERT_BLOCK * EXPERT_BLOCK
    padded_ends = jnp.cumsum(padded)
    padded_starts = padded_ends - padded
    start_of = jnp.zeros_like(expert)
    for e in range(N_EXPERTS):
        start_of = jnp.where(expert == e, padded_starts[e], start_of)
    dest = (start_of + rank).reshape(TOP_K * t)
    n_rows = TOP_K * t + N_EXPERTS * EXPERT_BLOCK
    tok = jnp.tile(jnp.arange(t, dtype=jnp.int32), TOP_K)
    row_tok = jnp.zeros((n_rows,), jnp.int32).at[dest].set(tok)
    block_start = jnp.arange(n_rows // EXPERT_BLOCK, dtype=jnp.int32) * EXPERT_BLOCK
    block_expert = jnp.minimum(jnp.searchsorted(padded_ends, block_start, side='right'),
                               N_EXPERTS - 1).astype(jnp.int32)
    y_rows = expert_ffn(block_expert, row_tok, xf, w1.astype(BF16), w3.astype(BF16), w2.astype(BF16))
    w_pad = jnp.pad(weight.T, ((0, 0), (0, LANES - TOP_K)))
    return combine_experts(dest, y_rows, w_pad, xf, ln_g, ln_b)


def _split_w_in(w_in):
    sizes = (QK_WIDTH, QK_WIDTH, QK_WIDTH, QK_WIDTH, HEADS, HEADS,
             QK_WIDTH, QK_WIDTH, QK_WIDTH, QK_WIDTH, HEADS, HEADS,
             MLA_Q_LORA, MLA_KV_LORA, MLA_ROPE, 3 * D_MODEL)
    parts, acc = [], 0
    for size in sizes:
        parts.append(w_in[:, acc:acc + size])
        acc += size
    return parts


def _arrange_w_in(w_in):
    (g_q, g_k, g_v, g_z, g_a, g_b, m_q, m_k, m_v, m_o, m_i, m_f, c_q, c_kv, k_rope, gates) = _split_w_in(w_in)
    d = w_in.shape[0]
    half = MLA_ROPE // 2
    pad64 = jnp.zeros((d, LANES - MLA_ROPE), w_in.dtype)
    rope_sw = jnp.concatenate([k_rope[:, half:], k_rope[:, :half]], axis=1)
    main = jnp.concatenate([c_q, c_kv, k_rope, pad64, rope_sw, pad64, jnp.zeros((d, LANES), w_in.dtype),
                            g_q, g_k, g_v, g_z, m_q, m_k, m_v, m_o, gates], axis=1)
    small = jnp.concatenate([g_a, g_b, m_i, m_f, jnp.zeros((d, LANES - 4 * HEADS), w_in.dtype)], axis=1)
    return main.astype(BF16), small.astype(BF16)


def _arrange_mla(w_uq, w_ukv):
    half = MLA_ROPE // 2
    wq = w_uq.reshape(MLA_Q_LORA, HEADS, HEAD_DIM + MLA_ROPE)
    nope, rope = wq[:, :, :HEAD_DIM], wq[:, :, HEAD_DIM:]
    pad = jnp.zeros((MLA_Q_LORA, HEADS, LANES - MLA_ROPE), w_uq.dtype)
    wq_a = jnp.concatenate([nope, rope, pad], axis=2).reshape(MLA_Q_LORA, HEADS * 2 * HEAD_DIM)
    rope_sw = jnp.concatenate([rope[:, :, half:], rope[:, :, :half]], axis=2)
    wq_b = jnp.concatenate([rope_sw, pad], axis=2).reshape(MLA_Q_LORA, HEADS * LANES)
    wkv = w_ukv.reshape(MLA_KV_LORA, HEADS, 2 * HEAD_DIM)
    wkv = jnp.concatenate([wkv[:, :, :HEAD_DIM].reshape(MLA_KV_LORA, QK_WIDTH),
                           wkv[:, :, HEAD_DIM:].reshape(MLA_KV_LORA, QK_WIDTH)], axis=1)
    return wq_a.astype(BF16), wq_b.astype(BF16), wkv.astype(BF16)


def kernel(x, positions, ln_in_g, ln_in_b, w_in, gdn_conv, gdn_a_log, gdn_dt_bias, gdn_norm, mlstm_gate_bias, mlstm_norm, mla_q_norm, mla_kv_norm, mla_w_uq, mla_w_ukv, w_br_gdn, w_br_mlstm, w_br_mla, gate_bias, w_out, ln1_g, ln1_b, router_w, router_bias, moe_w1, moe_w3, moe_w2, ln2_g, ln2_b):
    bsz, seq, d = x.shape
    t = bsz * seq
    xf, xb = layer_norm_entry(x.reshape(t, d), ln_in_g, ln_in_b)
    cc, ss = rope_tables(positions)
    fmasks, bmasks = _group_constants()
    for l in range(DEPTH):
        w_main, w_small = _arrange_w_in(w_in[l])
        h, small = in_projection(xb, w_main, w_small)
        small_t = small[:, :4 * HEADS].T
        y_gdn = gated_deltanet_heads(h, small_t, gdn_conv[l], gdn_a_log[l], gdn_dt_bias[l], gdn_norm[l],
                                     fmasks, bmasks, bsz, seq)
        y_mlstm = mlstm_heads(h, small_t, mlstm_gate_bias[l], mlstm_norm[l], fmasks, bmasks, bsz, seq)
        wq_a, wq_b, wkv = _arrange_mla(mla_w_uq[l], mla_w_ukv[l])
        q, kn, kr, v = mla_prepare(h, cc, ss, mla_q_norm[l], mla_kv_norm[l], wq_a, wq_b, wkv)
        y_mla = latent_attention(q, kn, kr, v, bsz, seq)
        xf, xb = merge_branches(y_gdn, y_mlstm, y_mla, h, gate_bias[l], w_br_gdn[l].astype(BF16),
                                w_br_mlstm[l].astype(BF16), w_br_mla[l].astype(BF16),
                                w_out[l].astype(BF16), xf, ln1_g[l], ln1_b[l])
        xf, xb = routed_experts(xf, router_w, router_bias, moe_w1[l], moe_w3[l], moe_w2[l],
                                ln2_g[l], ln2_b[l])
    return xf.reshape(bsz, seq, d)
```

```python
import functools

import jax
import jax.numpy as jnp
from jax import lax
from jax.experimental import pallas as pl
from jax.experimental.pallas import tpu as pltpu

F32 = jnp.float32
BF16 = jnp.bfloat16

D_MODEL = 1024
DEPTH = 2
HEADS = 4
HEAD_DIM = 128
CHUNK = 64
CONV_WIDTH = 4
GATE_CAP = 15.0
MLA_ROPE = 64
MLA_Q_LORA = 384
MLA_KV_LORA = 256
ROPE_THETA = 10000.0
N_EXPERTS = 16
N_GROUPS = 4
EXPERTS_PER_GROUP = 4
TOP_K = 2
D_EXPERT = 512
EXPERT_BLOCK = 256
LN_EPS = 1e-5
RMS_EPS = 1e-6
DEEPNORM_ALPHA = (2 * DEPTH) ** 0.25
LOG2_E = 1.4426950408889634

LANES = 128
QK_WIDTH = HEADS * HEAD_DIM
H_WIDTH = 8192
COL_MLA = 0
COL_GDN = 8
COL_MLSTM = 24
COL_GATE = 5
SEQ_BLOCK = 512
GROUP = 256
VMEM_LIMIT = 48 * 1024 * 1024


def _cparams(sem):
    return pltpu.CompilerParams(dimension_semantics=sem, vmem_limit_bytes=VMEM_LIMIT)


def _sigmoid(x):
    return 1.0 / (1.0 + jnp.exp(-x))


def _layer_norm(x, g, b):
    mu = jnp.mean(x, axis=-1, keepdims=True)
    xc = x - mu
    var = jnp.mean(xc * xc, axis=-1, keepdims=True)
    return xc * lax.rsqrt(var + LN_EPS) * g + b


def _dot(a, b):
    return jnp.dot(a, b, preferred_element_type=F32)


def _dot_nt(a, b, precision=None):
    return lax.dot_general(a, b, (((1,), (1,)), ((), ())), preferred_element_type=F32,
                           precision=precision)


def _dot_tn(a, b, precision=None):
    return lax.dot_general(a, b, (((0,), (0,)), ((), ())), preferred_element_type=F32,
                           precision=precision)


def _ln_body(x_ref, g_ref, b_ref, of_ref, ob_ref):
    y = _layer_norm(x_ref[...], g_ref[...], b_ref[...])
    of_ref[...] = y
    ob_ref[...] = y.astype(BF16)


def layer_norm_entry(x, g, b, tm=512):
    t, d = x.shape
    return pl.pallas_call(
        _ln_body,
        grid=(t // tm,),
        in_specs=[pl.BlockSpec((tm, d), lambda i: (i, 0)),
                  pl.BlockSpec((1, d), lambda i: (0, 0)),
                  pl.BlockSpec((1, d), lambda i: (0, 0))],
        out_specs=[pl.BlockSpec((tm, d), lambda i: (i, 0)),
                   pl.BlockSpec((tm, d), lambda i: (i, 0))],
        out_shape=[jax.ShapeDtypeStruct((t, d), F32), jax.ShapeDtypeStruct((t, d), BF16)],
        compiler_params=_cparams(("parallel",)),
        name="ln_entry",
    )(x, g.reshape(1, d), b.reshape(1, d))


def _inproj_body(x_ref, w_ref, ws_ref, h_ref, hs_ref):
    x = x_ref[...]
    h_ref[...] = _dot(x, w_ref[...]).astype(BF16)

    @pl.when(pl.program_id(1) == 0)
    def _():
        hs_ref[...] = _dot(x, ws_ref[...])


def in_projection(xb, w_main, w_small, tm=1024, tn=512):
    t, d = xb.shape
    n = w_main.shape[1]
    return pl.pallas_call(
        _inproj_body,
        grid=(t // tm, n // tn),
        in_specs=[pl.BlockSpec((tm, d), lambda i, j: (i, 0)),
                  pl.BlockSpec((d, tn), lambda i, j: (0, j)),
                  pl.BlockSpec((d, LANES), lambda i, j: (0, 0))],
        out_specs=[pl.BlockSpec((tm, tn), lambda i, j: (i, j)),
                   pl.BlockSpec((tm, LANES), lambda i, j: (i, 0))],
        out_shape=[jax.ShapeDtypeStruct((t, n), BF16), jax.ShapeDtypeStruct((t, LANES), F32)],
        compiler_params=_cparams(("parallel", "arbitrary")),
        name="in_proj",
    )(xb, w_main, w_small)


def _lane_pick(x, lane):
    idx = lax.broadcasted_iota(jnp.int32, x.shape, 1)
    return jnp.sum(jnp.where(idx == lane, x, 0.0), axis=1, keepdims=True)


def _softplus(x):
    return jnp.maximum(x, 0.0) + jnp.log1p(jnp.exp(-jnp.abs(x)))


def _group_masks():
    r = lax.broadcasted_iota(jnp.int32, (GROUP, GROUP), 0)
    c = lax.broadcasted_iota(jnp.int32, (GROUP, GROUP), 1)
    same = (r // CHUNK) == (c // CHUNK)
    causal = jnp.logical_and(same, r >= c)
    strict = jnp.logical_and(same, r > c)
    upper = jnp.logical_and(same, r <= c)
    return causal, strict, upper, r == c


def _split_bf16(x):
    hi = x.astype(BF16)
    return hi, (x - hi.astype(F32)).astype(BF16)


def _group_cumsum_col(col, low_b):
    hi, lo = _split_bf16(jnp.broadcast_to(col, (GROUP, LANES)))
    return (_dot(low_b, hi) + _dot(low_b, lo))[:, :1]


def _group_cumsum_row(row, up_b):
    hi, lo = _split_bf16(jnp.broadcast_to(row, (16, GROUP)))
    return (_dot(hi, up_b) + _dot(lo, up_b))[0:1, :]


def _rms_norm(x, g):
    return x * lax.rsqrt(jnp.mean(x * x, axis=-1, keepdims=True) + RMS_EPS) * g


def _gdn_body(alog_ref, dtb_ref, q_ref, k_ref, v_ref, z_ref, cq_ref, ck_ref, cv_ref,
              sm_ref, smt_ref, ng_ref, o_ref, state_ref, eq_ref, ek_ref, ev_ref):
    hh = pl.program_id(1)
    lb = q_ref.shape[0]

    @pl.when(pl.program_id(2) == 0)
    def _():
        state_ref[...] = jnp.zeros_like(state_ref)
        for e_ref in (eq_ref, ek_ref, ev_ref):
            e_ref[0:8, :] = jnp.zeros((8, HEAD_DIM), F32)

    def conv_silu(x_ref, w_ref, e_ref):
        e_ref[8:, :] = x_ref[...].astype(F32)
        w = w_ref[...]
        y = w[0:1, :] * e_ref[pl.ds(8 - CONV_WIDTH + 1, lb), :]
        for j in range(1, CONV_WIDTH):
            y = y + w[j:j + 1, :] * e_ref[pl.ds(8 - CONV_WIDTH + 1 + j, lb), :]
        e_ref[0:8, :] = e_ref[lb:lb + 8, :]
        return y * _sigmoid(y)

    q = conv_silu(q_ref, cq_ref, eq_ref)
    k = conv_silu(k_ref, ck_ref, ek_ref)
    v = conv_silu(v_ref, cv_ref, ev_ref)
    q = q * lax.rsqrt(jnp.sum(q * q, axis=-1, keepdims=True) + RMS_EPS) * (HEAD_DIM ** -0.5)
    k = k * lax.rsqrt(jnp.sum(k * k, axis=-1, keepdims=True) + RMS_EPS)

    neg_a = -jnp.exp(jnp.full((1, 1), alog_ref[hh], F32))
    dtb = dtb_ref[hh]
    sm = sm_ref[...]
    g_col = neg_a * _softplus(_lane_pick(sm, hh) + dtb)
    beta_col = _sigmoid(_lane_pick(sm, HEADS + hh))
    g_row = neg_a * _softplus(smt_ref[pl.ds(hh, 1), :] + dtb)

    causal, strict, upper, diag = _group_masks()
    low_b = jnp.where(causal, 1.0, 0.0).astype(BF16)
    up_b = jnp.where(upper, 1.0, 0.0).astype(BF16)
    eye = jnp.where(diag, 1.0, 0.0).astype(F32)
    state = state_ref[...]
    z = z_ref[...].astype(F32)
    ng = ng_ref[...]
    for gi in range(lb // GROUP):
        gs = slice(gi * GROUP, (gi + 1) * GROUP)
        qg, kg, bg = q[gs], k[gs], beta_col[gs]
        gc_col = _group_cumsum_col(g_col[gs], low_b)
        gc_row = _group_cumsum_row(g_row[:, gs], up_b)
        decay = jnp.exp(jnp.where(causal, gc_col - gc_row, -jnp.inf))
        kg_b = kg.astype(BF16)
        kb = kg * bg
        a_mat = jnp.where(strict, _dot_nt(kb.astype(BF16), kg_b) * decay, 0.0)
        pw = -a_mat
        t_mat = eye + pw
        for _ in range(5):
            pw_b = pw.astype(BF16)
            pw = _dot(pw_b, pw_b)
            t_mat = t_mat + _dot(t_mat.astype(BF16), pw.astype(BF16))
        e_gc = jnp.exp(gc_col)
        rhs = jnp.concatenate([(v[gs] * bg).astype(BF16), (kb * e_gc).astype(BF16)], axis=1)
        uw = _dot(t_mat.astype(BF16), rhs).astype(BF16)
        qk = (_dot_nt(qg.astype(BF16), kg_b) * decay).astype(BF16)
        qk_uw = _dot(qk, uw)
        o_intra = qk_uw[:, :HEAD_DIM]
        q_eff = qg * e_gc - qk_uw[:, HEAD_DIM:]
        for c in range(GROUP // CHUNK):
            sl = slice(c * CHUNK, (c + 1) * CHUNK)
            gc_last = gc_col[(c + 1) * CHUNK - 1:(c + 1) * CHUNK, :]
            k_tail = (kg[sl] * jnp.exp(gc_last - gc_col[sl])).astype(BF16)
            kt_uw = _dot_tn(k_tail, uw[sl])
            lhs = jnp.concatenate([q_eff[sl], kt_uw[:, HEAD_DIM:]], axis=0).astype(BF16)
            res = _dot(lhs, state.astype(BF16))
            out = res[:CHUNK] + o_intra[sl]
            state = state * jnp.exp(gc_last) - res[CHUNK:] + kt_uw[:, :HEAD_DIM]
            rows = slice(gi * GROUP + c * CHUNK, gi * GROUP + (c + 1) * CHUNK)
            zc = z[rows]
            o_ref[rows, :] = (_rms_norm(out, ng) * (zc * _sigmoid(zc))).astype(o_ref.dtype)
    state_ref[...] = state


def gated_deltanet(h, small, small_t, conv_w, a_log, dt_bias, norm_g, bsz, seq):
    t = h.shape[0]
    lb = SEQ_BLOCK
    nb = seq // lb

    def col(off):
        return pl.BlockSpec((lb, HEAD_DIM), lambda b, hh, s, *_: (b * nb + s, COL_GDN + off + hh))

    def conv(off):
        return pl.BlockSpec((CONV_WIDTH, HEAD_DIM), lambda b, hh, s, *_: (0, off + hh))

    grid_spec = pltpu.PrefetchScalarGridSpec(
        num_scalar_prefetch=2,
        grid=(bsz, HEADS, nb),
        in_specs=[col(0), col(HEADS), col(2 * HEADS), col(3 * HEADS),
                  conv(0), conv(HEADS), conv(2 * HEADS),
                  pl.BlockSpec((lb, LANES), lambda b, hh, s, *_: (b * nb + s, 0)),
                  pl.BlockSpec((4 * HEADS, lb), lambda b, hh, s, *_: (0, b * nb + s)),
                  pl.BlockSpec((1, HEAD_DIM), lambda b, hh, s, *_: (0, 0))],
        out_specs=pl.BlockSpec((lb, HEAD_DIM), lambda b, hh, s, *_: (b * nb + s, hh)),
        scratch_shapes=[pltpu.VMEM((HEAD_DIM, HEAD_DIM), F32),
                        pltpu.VMEM((lb + 8, HEAD_DIM), F32),
                        pltpu.VMEM((lb + 8, HEAD_DIM), F32),
                        pltpu.VMEM((lb + 8, HEAD_DIM), F32)])
    return pl.pallas_call(
        _gdn_body,
        grid_spec=grid_spec,
        out_shape=jax.ShapeDtypeStruct((t, QK_WIDTH), BF16),
        compiler_params=_cparams(("parallel", "parallel", "arbitrary")),
        name="gated_deltanet",
    )(a_log, dt_bias, h, h, h, h, conv_w, conv_w, conv_w, small, small_t, norm_g.reshape(1, HEAD_DIM))


def _soft_cap(x):
    return GATE_CAP * jnp.tanh(x / GATE_CAP)


def _log_sigmoid(x):
    return jnp.minimum(x, 0.0) - jnp.log1p(jnp.exp(-jnp.abs(x)))


def _mlstm_body(gb_ref, q_ref, k_ref, v_ref, o_ref, sm_ref, smt_ref, ng_ref,
                y_ref, c_ref, n_ref, m_ref):
    hh = pl.program_id(1)
    lb = q_ref.shape[0]
    n_groups = lb // GROUP
    per_group = GROUP // CHUNK

    @pl.when(pl.program_id(2) == 0)
    def _():
        c_ref[...] = jnp.zeros_like(c_ref)
        n_ref[...] = jnp.zeros_like(n_ref)
        m_ref[...] = jnp.zeros_like(m_ref)

    q = q_ref[...]
    k = k_ref[...].astype(F32) * (HEAD_DIM ** -0.5)
    v = v_ref[...]
    gb_i = gb_ref[hh]
    gb_f = gb_ref[HEADS + hh]
    sm = sm_ref[...]
    i_col = _soft_cap(_lane_pick(sm, 2 * HEADS + hh) + gb_i)
    f_col = _log_sigmoid(_soft_cap(_lane_pick(sm, 3 * HEADS + hh) + gb_f))
    i_row = _soft_cap(smt_ref[pl.ds(2 * HEADS + hh, 1), :] + gb_i)
    f_row = _log_sigmoid(_soft_cap(smt_ref[pl.ds(3 * HEADS + hh, 1), :] + gb_f))

    causal, _, upper, _ = _group_masks()
    low_b = jnp.where(causal, 1.0, 0.0).astype(BF16)
    up_b = jnp.where(upper, 1.0, 0.0).astype(BF16)

    bc_cols, lkws, b_lasts, lkw_maxes = [], [], [], []
    for gi in range(n_groups):
        gs = slice(gi * GROUP, (gi + 1) * GROUP)
        bc_col = _group_cumsum_col(f_col[gs], low_b)
        bc_cols.append(bc_col)
        for c in range(per_group):
            sl = slice(c * CHUNK, (c + 1) * CHUNK)
            b_last = bc_col[(c + 1) * CHUNK - 1:(c + 1) * CHUNK, :]
            lkw = b_last - bc_col[sl] + i_col[gi * GROUP + c * CHUNK:gi * GROUP + (c + 1) * CHUNK]
            b_lasts.append(b_last)
            lkws.append(lkw)
            lkw_maxes.append(jnp.max(lkw, axis=0, keepdims=True))
    m_st = m_ref[...][:, :1]
    m_prev, m_next, carry_decay = [], [], []
    for ci in range(n_groups * per_group):
        m_new = jnp.maximum(b_lasts[ci] + m_st, lkw_maxes[ci])
        m_prev.append(m_st)
        m_next.append(m_new)
        carry_decay.append(jnp.exp(b_lasts[ci] + m_st - m_new))
        m_st = m_new

    c_st = c_ref[...]
    n_st = n_ref[...]
    o_pre = o_ref[...].astype(F32)
    ng = ng_ref[...]
    for gi in range(n_groups):
        gs = slice(gi * GROUP, (gi + 1) * GROUP)
        qg, vg = q[gs], v[gs]
        kg_b = k[gs].astype(BF16)
        bc_col = bc_cols[gi]
        bc_row = _group_cumsum_row(f_row[:, gs], up_b)
        log_d = jnp.where(causal, bc_col - bc_row + i_row[:, gs], -jnp.inf)
        m_prev_col = jnp.concatenate(
            [jnp.broadcast_to(m_prev[gi * per_group + c], (CHUNK, 1)) for c in range(per_group)], axis=0)
        log_inter = bc_col + m_prev_col
        m_t = jnp.maximum(log_inter, jnp.max(log_d, axis=-1, keepdims=True))
        w_inter = jnp.exp(log_inter - m_t)
        s = _dot_nt(qg, kg_b) * jnp.exp(log_d - m_t)
        s_v = _dot(s.astype(BF16), vg)
        s_sum = jnp.sum(s, axis=-1, keepdims=True)
        floor = jnp.exp(-m_t)
        for c in range(per_group):
            ci = gi * per_group + c
            sl = slice(c * CHUNK, (c + 1) * CHUNK)
            rows = slice(gi * GROUP + c * CHUNK, gi * GROUP + (c + 1) * CHUNK)
            qc = qg[sl]
            num = w_inter[sl] * _dot(qc, c_st.astype(BF16)) + s_v[sl]
            den = w_inter[sl] * jnp.sum(qc.astype(F32) * n_st, axis=-1, keepdims=True) + s_sum[sl]
            hid = num / jnp.maximum(jnp.abs(den), floor[sl])
            kw = k[rows] * jnp.exp(lkws[ci] - m_next[ci])
            c_st = carry_decay[ci] * c_st + _dot_tn(kw.astype(BF16), vg[sl])
            n_st = carry_decay[ci] * n_st + jnp.sum(kw, axis=0, keepdims=True)
            y_ref[rows, :] = (_sigmoid(o_pre[rows]) * _rms_norm(hid, ng)).astype(y_ref.dtype)
    c_ref[...] = c_st
    n_ref[...] = n_st
    m_ref[...] = jnp.broadcast_to(m_st, m_ref.shape)


def mlstm(h, small, small_t, gate_bias, norm_g, bsz, seq):
    t = h.shape[0]
    lb = SEQ_BLOCK
    nb = seq // lb

    def col(off):
        return pl.BlockSpec((lb, HEAD_DIM), lambda b, hh, s, *_: (b * nb + s, COL_MLSTM + off + hh))

    grid_spec = pltpu.PrefetchScalarGridSpec(
        num_scalar_prefetch=1,
        grid=(bsz, HEADS, nb),
        in_specs=[col(0), col(HEADS), col(2 * HEADS), col(3 * HEADS),
                  pl.BlockSpec((lb, LANES), lambda b, hh, s, *_: (b * nb + s, 0)),
                  pl.BlockSpec((4 * HEADS, lb), lambda b, hh, s, *_: (0, b * nb + s)),
                  pl.BlockSpec((1, HEAD_DIM), lambda b, hh, s, *_: (0, hh))],
        out_specs=pl.BlockSpec((lb, HEAD_DIM), lambda b, hh, s, *_: (b * nb + s, hh)),
        scratch_shapes=[pltpu.VMEM((HEAD_DIM, HEAD_DIM), F32),
                        pltpu.VMEM((1, HEAD_DIM), F32),
                        pltpu.VMEM((1, LANES), F32)])
    return pl.pallas_call(
        _mlstm_body,
        grid_spec=grid_spec,
        out_shape=jax.ShapeDtypeStruct((t, QK_WIDTH), BF16),
        compiler_params=_cparams(("parallel", "parallel", "arbitrary")),
        name="mlstm",
    )(gate_bias, h, h, h, h, small, small_t, norm_g.reshape(1, QK_WIDTH))


def _group_constants():
    r = jnp.arange(GROUP, dtype=jnp.int32)[:, None]
    c = jnp.arange(GROUP, dtype=jnp.int32)[None, :]
    same = (r // CHUNK) == (c // CHUNK)
    neg = jnp.where(same & (r >= c), 0.0, -jnp.inf).astype(F32)
    strict = (same & (r > c)).astype(F32)
    eye = (r == c).astype(F32)
    upper = (same & (r <= c)).astype(BF16)
    last = (r == (c // CHUNK) * CHUNK + CHUNK - 1).astype(BF16)
    return jnp.stack([neg, strict, eye]), jnp.stack([upper, last, eye.astype(BF16)])


def _split3(x):
    hi = x.astype(BF16)
    r1 = x - hi.astype(F32)
    mid = r1.astype(BF16)
    return hi, mid, (r1 - mid.astype(F32)).astype(BF16)


def _rows_times(rows8, mat_b, terms):
    rows = jnp.concatenate([rows8, jnp.zeros_like(rows8)], axis=0)
    parts = _split3(rows)[:terms]
    out = _dot(parts[0], mat_b)
    for p in parts[1:]:
        out = out + _dot(p, mat_b)
    return out[0:8]


def _rows_to_cols(stack, eye_b):
    parts = _split3(stack)
    out = _dot_nt(eye_b, parts[0])
    for p in parts[1:]:
        out = out + _dot_nt(eye_b, p)
    return out


def _conv_silu(x_ref, w, e_ref):
    lb = x_ref.shape[0]
    e_ref[8:, :] = x_ref[...].astype(F32)
    y = w[0:1, :] * e_ref[pl.ds(8 - CONV_WIDTH + 1, lb), :]
    for j in range(1, CONV_WIDTH):
        y = y + w[j:j + 1, :] * e_ref[pl.ds(8 - CONV_WIDTH + 1 + j, lb), :]
    e_ref[0:8, :] = e_ref[lb:lb + 8, :]
    return y * _sigmoid(y)


def _gdn_heads_body(q_ref, k_ref, v_ref, z_ref, cw_ref, smt_ref, gp_ref, ng_ref, fm_ref, bm_ref,
                    o_ref, state_ref, eq_ref, ek_ref, ev_ref):
    lb = q_ref.shape[0]

    @pl.when(pl.program_id(1) == 0)
    def _():
        state_ref[...] = jnp.zeros_like(state_ref)
        for e_ref in (eq_ref, ek_ref, ev_ref):
            e_ref[0:8, :] = jnp.zeros((8, QK_WIDTH), F32)

    cw = cw_ref[...]
    q_all = _conv_silu(q_ref, cw[:, 0:QK_WIDTH], eq_ref)
    k_all = _conv_silu(k_ref, cw[:, QK_WIDTH:2 * QK_WIDTH], ek_ref)
    v_all = _conv_silu(v_ref, cw[:, 2 * QK_WIDTH:3 * QK_WIDTH], ev_ref)
    z_all = z_ref[...].astype(F32)
    ng = ng_ref[...]
    neg, strict01, eye = fm_ref[0], fm_ref[1], fm_ref[2]
    up_b, last_b, eye_b = bm_ref[0], bm_ref[1], bm_ref[2]

    gp = gp_ref[...]
    neg_a8 = -jnp.exp(gp[0:8, 0:1])
    dtb8 = gp[8:16, 0:1]
    g8 = neg_a8 * _softplus(smt_ref[0:8, :] + dtb8)
    beta8 = _sigmoid(smt_ref[HEADS:HEADS + 8, :])
    gc8 = _rows_times(g8, up_b, 2)
    gl8 = _rows_times(gc8, last_b, 3)
    egc8 = jnp.exp(gc8)
    tail8 = jnp.exp(gl8 - gc8)
    elast8 = jnp.exp(gl8)
    stack = jnp.concatenate([gc8, beta8, egc8, tail8, beta8 * egc8,
                             jnp.zeros((LANES - 40, lb), F32)], axis=0)
    cols = _rows_to_cols(stack, eye_b)

    hds = range(HEADS)
    lanes = [slice(hd * HEAD_DIM, (hd + 1) * HEAD_DIM) for hd in hds]
    col = lambda j, hd: cols[:, 8 * j + hd:8 * j + hd + 1]
    qg = [q_all[:, lanes[hd]] for hd in hds]
    kg = [k_all[:, lanes[hd]] for hd in hds]
    qg = [x * lax.rsqrt(jnp.sum(x * x, axis=-1, keepdims=True) + RMS_EPS) * (HEAD_DIM ** -0.5) for x in qg]
    kg = [x * lax.rsqrt(jnp.sum(x * x, axis=-1, keepdims=True) + RMS_EPS) for x in kg]
    kg_b = [x.astype(BF16) for x in kg]
    decay = [jnp.exp(col(0, hd) - gc8[hd:hd + 1, :] + neg) for hd in hds]
    a_mat = [_dot_nt((kg[hd] * col(1, hd)).astype(BF16), kg_b[hd]) * decay[hd] * strict01 for hd in hds]
    pw = [-a for a in a_mat]
    t_mat = [eye + p for p in pw]
    for _ in range(5):
        pw_b = [p.astype(BF16) for p in pw]
        pw = [_dot(p, p) for p in pw_b]
        t_mat = [t_mat[hd] + _dot(t_mat[hd].astype(BF16), pw[hd].astype(BF16)) for hd in hds]
    rhs = [jnp.concatenate([(v_all[:, lanes[hd]] * col(1, hd)).astype(BF16),
                            (kg[hd] * col(4, hd)).astype(BF16)], axis=1) for hd in hds]
    uw = [_dot(t_mat[hd].astype(BF16), rhs[hd]).astype(BF16) for hd in hds]
    qk = [(_dot_nt(qg[hd].astype(BF16), kg_b[hd]) * decay[hd]).astype(BF16) for hd in hds]
    qk_uw = [_dot(qk[hd], uw[hd]) for hd in hds]
    q_eff = [qg[hd] * col(2, hd) - qk_uw[hd][:, HEAD_DIM:] for hd in hds]
    k_tail = [(kg[hd] * col(3, hd)).astype(BF16) for hd in hds]
    state = [state_ref[hd] for hd in hds]
    for c in range(lb // CHUNK):
        sl = slice(c * CHUNK, (c + 1) * CHUNK)
        kt_uw = [_dot_tn(k_tail[hd][sl], uw[hd][sl]) for hd in hds]
        lhs = [jnp.concatenate([q_eff[hd][sl], kt_uw[hd][:, HEAD_DIM:]], axis=0).astype(BF16) for hd in hds]
        res = [_dot(lhs[hd], state[hd].astype(BF16)) for hd in hds]
        state = [state[hd] * elast8[hd:hd + 1, c * CHUNK:c * CHUNK + 1] - res[hd][CHUNK:]
                 + kt_uw[hd][:, :HEAD_DIM] for hd in hds]
        for hd in hds:
            out = res[hd][:CHUNK] + qk_uw[hd][sl, :HEAD_DIM]
            zc = z_all[sl, lanes[hd]]
            o_ref[sl, lanes[hd]] = (_rms_norm(out, ng) * (zc * _sigmoid(zc))).astype(o_ref.dtype)
    for hd in hds:
        state_ref[hd] = state[hd]


def _gate_params(first, second):
    out = jnp.zeros((16, LANES), F32)
    out = out.at[0:HEADS, :].set(jnp.broadcast_to(first.astype(F32)[:, None], (HEADS, LANES)))
    return out.at[8:8 + HEADS, :].set(jnp.broadcast_to(second.astype(F32)[:, None], (HEADS, LANES)))


def gated_deltanet_heads(h, small_t, conv_w, a_log, dt_bias, norm_g, fmasks, bmasks, bsz, seq):
    t = h.shape[0]
    lb = GROUP
    nb = seq // lb
    wide = QK_WIDTH
    first = COL_GDN * LANES // wide

    def col(j):
        return pl.BlockSpec((lb, wide), lambda b, s: (b * nb + s, first + j))

    fixed2 = lambda b, s: (0, 0)
    fixed3 = lambda b, s: (0, 0, 0)
    return pl.pallas_call(
        _gdn_heads_body,
        grid=(bsz, nb),
        in_specs=[col(0), col(1), col(2), col(3),
                  pl.BlockSpec((CONV_WIDTH, 3 * wide), fixed2),
                  pl.BlockSpec((4 * HEADS, lb), lambda b, s: (0, b * nb + s)),
                  pl.BlockSpec((16, LANES), fixed2),
                  pl.BlockSpec((1, HEAD_DIM), fixed2),
                  pl.BlockSpec((3, GROUP, GROUP), fixed3),
                  pl.BlockSpec((3, GROUP, GROUP), fixed3)],
        out_specs=pl.BlockSpec((lb, wide), lambda b, s: (b * nb + s, 0)),
        out_shape=jax.ShapeDtypeStruct((t, wide), BF16),
        scratch_shapes=[pltpu.VMEM((HEADS, HEAD_DIM, HEAD_DIM), F32),
                        pltpu.VMEM((lb + 8, wide), F32),
                        pltpu.VMEM((lb + 8, wide), F32),
                        pltpu.VMEM((lb + 8, wide), F32)],
        compiler_params=_cparams(("parallel", "arbitrary")),
        name="gated_deltanet",
    )(h, h, h, h, conv_w, small_t, _gate_params(a_log, dt_bias), norm_g.reshape(1, HEAD_DIM), fmasks, bmasks)


def _mlstm_heads_body(q_ref, k_ref, v_ref, o_ref, smt_ref, gp_ref, ng_ref, fm_ref, bm_ref,
                      y_ref, c_ref, n_ref, m_ref):
    lb = q_ref.shape[0]
    n_chunks = lb // CHUNK

    @pl.when(pl.program_id(1) == 0)
    def _():
        c_ref[...] = jnp.zeros_like(c_ref)
        n_ref[...] = jnp.zeros_like(n_ref)
        m_ref[...] = jnp.zeros_like(m_ref)

    hds = range(HEADS)
    lanes = [slice(hd * HEAD_DIM, (hd + 1) * HEAD_DIM) for hd in hds]
    q_all = q_ref[...]
    k_all = k_ref[...].astype(F32) * (HEAD_DIM ** -0.5)
    qg = [q_all[:, lanes[hd]] for hd in hds]
    kg = [k_all[:, lanes[hd]] for hd in hds]
    qk = [_dot_nt(qg[hd], kg[hd].astype(BF16)) for hd in hds]

    neg = fm_ref[0]
    up_b, last_b, eye_b = bm_ref[0], bm_ref[1], bm_ref[2]
    capped = _soft_cap(smt_ref[2 * HEADS:4 * HEADS, :] + gp_ref[...][0:8, 0:1])
    i8 = capped
    f8 = pltpu.roll(_log_sigmoid(capped), HEADS, axis=0)
    bc8 = _rows_times(f8, up_b, 2)
    bl8 = _rows_times(bc8, last_b, 3)
    lkw8 = bl8 - bc8 + i8
    chunk_id = lax.broadcasted_iota(jnp.int32, (8, lb), 1) // CHUNK
    m_st = m_ref[...][:, 0:1]
    m_prev_row = jnp.zeros((8, lb), F32)
    m_next_row = jnp.zeros((8, lb), F32)
    carry_decay = []
    for c in range(n_chunks):
        in_c = chunk_id == c
        b_last = bl8[:, c * CHUNK:c * CHUNK + 1]
        m_new = jnp.maximum(b_last + m_st, jnp.max(jnp.where(in_c, lkw8, -jnp.inf), axis=1, keepdims=True))
        carry_decay.append(jnp.exp(b_last + m_st - m_new))
        m_prev_row = jnp.where(in_c, m_st, m_prev_row)
        m_next_row = jnp.where(in_c, m_new, m_next_row)
        m_st = m_new
    m_ref[...] = jnp.broadcast_to(m_st, m_ref.shape)
    stack = jnp.concatenate([bc8, bc8 + m_prev_row, jnp.exp(lkw8 - m_next_row),
                             jnp.zeros((LANES - 24, lb), F32)], axis=0)
    cols = _rows_to_cols(stack, eye_b)

    v_all = v_ref[...]
    o_all = o_ref[...].astype(F32)
    ng_all = ng_ref[...]
    n_all = n_ref[...]
    col = lambda j, hd: cols[:, 8 * j + hd:8 * j + hd + 1]
    vg = [v_all[:, lanes[hd]] for hd in hds]
    log_d = [col(0, hd) - bc8[hd:hd + 1, :] + i8[hd:hd + 1, :] + neg for hd in hds]
    m_t = [jnp.maximum(col(1, hd), jnp.max(log_d[hd], axis=-1, keepdims=True)) for hd in hds]
    w_inter = [jnp.exp(col(1, hd) - m_t[hd]) for hd in hds]
    s = [qk[hd] * jnp.exp(log_d[hd] - m_t[hd]) for hd in hds]
    ones = jnp.ones((lb, HEAD_DIM), BF16)
    s_vx = [_dot(s[hd].astype(BF16), jnp.concatenate([vg[hd], ones], axis=1)) for hd in hds]
    s_v = [x[:, :HEAD_DIM] for x in s_vx]
    s_sum = [x[:, HEAD_DIM:HEAD_DIM + 1] for x in s_vx]
    floor = [jnp.exp(-m_t[hd]) for hd in hds]
    kw = [kg[hd] * col(2, hd) for hd in hds]
    kw_b = [x.astype(BF16) for x in kw]
    c_st = [c_ref[hd] for hd in hds]
    n_st = [n_all[hd:hd + 1, :] for hd in hds]
    for c in range(n_chunks):
        sl = slice(c * CHUNK, (c + 1) * CHUNK)
        q_c = [_dot(qg[hd][sl], c_st[hd].astype(BF16)) for hd in hds]
        q_n = [jnp.sum(qg[hd][sl].astype(F32) * n_st[hd], axis=-1, keepdims=True) for hd in hds]
        cd = [carry_decay[c][hd:hd + 1, :] for hd in hds]
        c_st = [cd[hd] * c_st[hd] + _dot_tn(kw_b[hd][sl], vg[hd][sl]) for hd in hds]
        n_st = [cd[hd] * n_st[hd] + jnp.sum(kw[hd][sl], axis=0, keepdims=True) for hd in hds]
        for hd in hds:
            num = w_inter[hd][sl] * q_c[hd] + s_v[hd][sl]
            den = w_inter[hd][sl] * q_n[hd] + s_sum[hd][sl]
            hid = num / jnp.maximum(jnp.abs(den), floor[hd][sl])
            y_ref[sl, lanes[hd]] = (_sigmoid(o_all[sl, lanes[hd]])
                                    * _rms_norm(hid, ng_all[:, lanes[hd]])).astype(y_ref.dtype)
    for hd in hds:
        c_ref[hd] = c_st[hd]
        n_ref[hd:hd + 1, :] = n_st[hd]


def mlstm_heads(h, small_t, gate_bias, norm_g, fmasks, bmasks, bsz, seq):
    t = h.shape[0]
    lb = GROUP
    nb = seq // lb
    wide = QK_WIDTH
    first = COL_MLSTM * LANES // wide

    def col(j):
        return pl.BlockSpec((lb, wide), lambda b, s: (b * nb + s, first + j))

    fixed2 = lambda b, s: (0, 0)
    fixed3 = lambda b, s: (0, 0, 0)
    gp = jnp.zeros((16, LANES), F32).at[0:2 * HEADS, :].set(
        jnp.broadcast_to(gate_bias.astype(F32)[:, None], (2 * HEADS, LANES)))
    return pl.pallas_call(
        _mlstm_heads_body,
        grid=(bsz, nb),
        in_specs=[col(0), col(1), col(2), col(3),
                  pl.BlockSpec((4 * HEADS, lb), lambda b, s: (0, b * nb + s)),
                  pl.BlockSpec((16, LANES), fixed2),
                  pl.BlockSpec((1, wide), fixed2),
                  pl.BlockSpec((3, GROUP, GROUP), fixed3),
                  pl.BlockSpec((3, GROUP, GROUP), fixed3)],
        out_specs=pl.BlockSpec((lb, wide), lambda b, s: (b * nb + s, 0)),
        out_shape=jax.ShapeDtypeStruct((t, wide), BF16),
        scratch_shapes=[pltpu.VMEM((HEADS, HEAD_DIM, HEAD_DIM), F32),
                        pltpu.VMEM((8, HEAD_DIM), F32),
                        pltpu.VMEM((8, LANES), F32)],
        compiler_params=_cparams(("parallel", "arbitrary")),
        name="mlstm",
    )(h, h, h, h, small_t, gp, norm_g.reshape(1, wide), fmasks, bmasks)


def _rope_table_body(pos_ref, freq_ref, sign_ref, cc_ref, ss_ref):
    ang = pos_ref[...] * freq_ref[...]
    sign = sign_ref[...]
    cc_ref[...] = jnp.cos(ang) * jnp.abs(sign)
    ss_ref[...] = jnp.sin(ang) * sign


def rope_tables(positions, tm=512):
    t = positions.size
    half = MLA_ROPE // 2
    inv_freq = 1.0 / (ROPE_THETA ** (jnp.arange(0, MLA_ROPE, 2, dtype=F32) / MLA_ROPE))
    zeros = jnp.zeros((LANES - MLA_ROPE,), F32)
    freq = jnp.concatenate([inv_freq, inv_freq, zeros]).reshape(1, LANES)
    sign = jnp.concatenate([-jnp.ones((half,), F32), jnp.ones((half,), F32), zeros]).reshape(1, LANES)
    return pl.pallas_call(
        _rope_table_body,
        grid=(t // tm,),
        in_specs=[pl.BlockSpec((tm, 1), lambda i: (i, 0)),
                  pl.BlockSpec((1, LANES), lambda i: (0, 0)),
                  pl.BlockSpec((1, LANES), lambda i: (0, 0))],
        out_specs=[pl.BlockSpec((tm, LANES), lambda i: (i, 0)),
                   pl.BlockSpec((tm, LANES), lambda i: (i, 0))],
        out_shape=[jax.ShapeDtypeStruct((t, LANES), F32), jax.ShapeDtypeStruct((t, LANES), F32)],
        compiler_params=_cparams(("parallel",)),
        name="rope_tables",
    )(positions.astype(F32).reshape(t, 1), freq, sign)


def _mla_pre_body(h_ref, cc_ref, ss_ref, qg_ref, kvg_ref, wqa_ref, wqb_ref, wkv_ref,
                  q_ref, kn_ref, kr_ref, v_ref):
    hblk = h_ref[...].astype(F32)
    cc = cc_ref[...]
    ss = ss_ref[...]
    cq = _rms_norm(hblk[:, :MLA_Q_LORA], qg_ref[...]).astype(BF16)
    ckv = _rms_norm(hblk[:, MLA_Q_LORA:MLA_Q_LORA + MLA_KV_LORA], kvg_ref[...]).astype(BF16)
    off = MLA_Q_LORA + MLA_KV_LORA
    kr_ref[...] = (hblk[:, off:off + LANES] * cc + hblk[:, off + LANES:off + 2 * LANES] * ss).astype(BF16)
    kv = _dot(ckv, wkv_ref[...])
    kn_ref[...] = kv[:, :QK_WIDTH].astype(BF16)
    v_ref[...] = kv[:, QK_WIDTH:].astype(BF16)
    qa = _dot(cq, wqa_ref[...])
    qb = _dot(cq, wqb_ref[...])
    scale = (HEAD_DIM + MLA_ROPE) ** -0.5 * LOG2_E
    for hh in range(HEADS):
        base = 2 * HEAD_DIM * hh
        q_ref[:, base:base + HEAD_DIM] = (qa[:, base:base + HEAD_DIM] * scale).astype(BF16)
        rope = qa[:, base + HEAD_DIM:base + 2 * HEAD_DIM] * cc + qb[:, hh * LANES:(hh + 1) * LANES] * ss
        q_ref[:, base + HEAD_DIM:base + 2 * HEAD_DIM] = (rope * scale).astype(BF16)


def mla_prepare(h, cc, ss, q_norm_g, kv_norm_g, wq_a, wq_b, wkv, tm=512):
    t = h.shape[0]
    row = lambda i: (i, 0)
    fixed = lambda i: (0, 0)
    return pl.pallas_call(
        _mla_pre_body,
        grid=(t // tm,),
        in_specs=[pl.BlockSpec((tm, 1024), row),
                  pl.BlockSpec((tm, LANES), row),
                  pl.BlockSpec((tm, LANES), row),
                  pl.BlockSpec((1, MLA_Q_LORA), fixed),
                  pl.BlockSpec((1, MLA_KV_LORA), fixed),
                  pl.BlockSpec(wq_a.shape, fixed),
                  pl.BlockSpec(wq_b.shape, fixed),
                  pl.BlockSpec(wkv.shape, fixed)],
        out_specs=[pl.BlockSpec((tm, 2 * QK_WIDTH), row),
                   pl.BlockSpec((tm, QK_WIDTH), row),
                   pl.BlockSpec((tm, LANES), row),
                   pl.BlockSpec((tm, QK_WIDTH), row)],
        out_shape=[jax.ShapeDtypeStruct((t, 2 * QK_WIDTH), BF16),
                   jax.ShapeDtypeStruct((t, QK_WIDTH), BF16),
                   jax.ShapeDtypeStruct((t, LANES), BF16),
                   jax.ShapeDtypeStruct((t, QK_WIDTH), BF16)],
        compiler_params=_cparams(("parallel",)),
        name="mla_prepare",
    )(h, cc, ss, q_norm_g.reshape(1, -1), kv_norm_g.reshape(1, -1), wq_a, wq_b, wkv)


def _attn_body(q_ref, kn_ref, kr_ref, v_ref, o_ref, *, tk, n_heads):
    qi = pl.program_id(2)
    tq = q_ref.shape[0]
    hds = range(n_heads)
    lanes = [slice(hd * HEAD_DIM, (hd + 1) * HEAD_DIM) for hd in hds]
    q = [q_ref[:, 2 * hd * HEAD_DIM:2 * (hd + 1) * HEAD_DIM] for hd in hds]

    def step(j, carry, masked):
        m, l, acc = carry
        start = pl.multiple_of(j * tk, tk)
        kr = kr_ref[pl.ds(start, tk), :]
        s = [_dot_nt(q[hd], jnp.concatenate([kn_ref[pl.ds(start, tk), lanes[hd]], kr], axis=1)) for hd in hds]
        if masked:
            r = lax.broadcasted_iota(jnp.int32, (tq, tk), 0)
            c = lax.broadcasted_iota(jnp.int32, (tq, tk), 1)
            s = [jnp.where(r >= c, x, -jnp.inf) for x in s]
        m_new = [jnp.maximum(m[hd], jnp.max(s[hd], axis=-1, keepdims=True)) for hd in hds]
        alpha = [jnp.exp2(m[hd] - m_new[hd]) for hd in hds]
        p = [jnp.exp2(s[hd] - m_new[hd]) for hd in hds]
        l_new = [alpha[hd] * l[hd] + jnp.sum(p[hd], axis=-1, keepdims=True) for hd in hds]
        p_b = [x.astype(BF16) for x in p]
        acc = [alpha[hd] * acc[hd] + _dot(p_b[hd], v_ref[pl.ds(start, tk), lanes[hd]]) for hd in hds]
        return tuple(m_new), tuple(l_new), tuple(acc)

    init = (tuple(jnp.full((tq, 1), -jnp.inf, F32) for _ in hds),
            tuple(jnp.zeros((tq, 1), F32) for _ in hds),
            tuple(jnp.zeros((tq, HEAD_DIM), F32) for _ in hds))
    carry = lax.fori_loop(0, qi, lambda j, cr: step(j, cr, False), init)
    _, l, acc = step(qi, carry, True)
    for hd in hds:
        o_ref[:, lanes[hd]] = (acc[hd] / l[hd]).astype(o_ref.dtype)


def latent_attention(q, kn, kr, v, bsz, seq, tq=512, n_heads=2):
    t = q.shape[0]
    nq = seq // tq
    wide = n_heads * HEAD_DIM
    return pl.pallas_call(
        functools.partial(_attn_body, tk=tq, n_heads=n_heads),
        grid=(bsz, HEADS // n_heads, nq),
        in_specs=[pl.BlockSpec((tq, 2 * wide), lambda b, hh, i: (b * nq + i, hh)),
                  pl.BlockSpec((seq, wide), lambda b, hh, i: (b, hh)),
                  pl.BlockSpec((seq, LANES), lambda b, hh, i: (b, 0)),
                  pl.BlockSpec((seq, wide), lambda b, hh, i: (b, hh))],
        out_specs=pl.BlockSpec((tq, wide), lambda b, hh, i: (b * nq + i, hh)),
        out_shape=jax.ShapeDtypeStruct((t, QK_WIDTH), BF16),
        compiler_params=_cparams(("parallel", "parallel", "arbitrary")),
        name="latent_attention",
    )(q, kn, kr, v)


def _merge_body(yg_ref, ym_ref, ya_ref, g0_ref, g1_ref, g2_ref, gb_ref, pg_ref, pm_ref, pa_ref,
                wo_ref, x_ref, ln_g_ref, ln_b_ref, of_ref, ob_ref):
    gb = gb_ref[...]

    def branch(y_ref, p_ref, g_ref, idx):
        gate = _sigmoid(g_ref[...].astype(F32) + gb[:, idx * D_MODEL:(idx + 1) * D_MODEL])
        return gate * _dot(y_ref[...], p_ref[...])

    merged = branch(yg_ref, pg_ref, g0_ref, 0) + branch(ym_ref, pm_ref, g1_ref, 1) \
        + branch(ya_ref, pa_ref, g2_ref, 2)
    mix = _dot(merged.astype(BF16), wo_ref[...])
    y = _layer_norm(DEEPNORM_ALPHA * x_ref[...] + mix, ln_g_ref[...], ln_b_ref[...])
    of_ref[...] = y
    ob_ref[...] = y.astype(BF16)


def merge_branches(y_gdn, y_mlstm, y_mla, h, gate_bias, p_gdn, p_mlstm, p_mla, w_out, x, ln_g, ln_b, tm=512):
    t, d = x.shape
    row = lambda i: (i, 0)
    fixed = lambda i: (0, 0)
    ybs = pl.BlockSpec((tm, QK_WIDTH), row)
    pbs = pl.BlockSpec((QK_WIDTH, d), fixed)
    return pl.pallas_call(
        _merge_body,
        grid=(t // tm,),
        in_specs=[ybs, ybs, ybs,
                  pl.BlockSpec((tm, d), lambda i: (i, COL_GATE)),
                  pl.BlockSpec((tm, d), lambda i: (i, COL_GATE + 1)),
                  pl.BlockSpec((tm, d), lambda i: (i, COL_GATE + 2)),
                  pl.BlockSpec((1, 3 * d), fixed),
                  pbs, pbs, pbs,
                  pl.BlockSpec((d, d), fixed),
                  pl.BlockSpec((tm, d), row),
                  pl.BlockSpec((1, d), fixed),
                  pl.BlockSpec((1, d), fixed)],
        out_specs=[pl.BlockSpec((tm, d), row), pl.BlockSpec((tm, d), row)],
        out_shape=[jax.ShapeDtypeStruct((t, d), F32), jax.ShapeDtypeStruct((t, d), BF16)],
        compiler_params=_cparams(("parallel",)),
        name="merge_branches",
    )(y_gdn, y_mlstm, y_mla, h, h, h, gate_bias.reshape(1, 3 * d), p_gdn, p_mlstm, p_mla, w_out, x,
      ln_g.reshape(1, d), ln_b.reshape(1, d))


def _top2_sum(a, b, c, d):
    hi1, lo1 = jnp.maximum(a, b), jnp.minimum(a, b)
    hi2, lo2 = jnp.maximum(c, d), jnp.minimum(c, d)
    return jnp.maximum(hi1, hi2) + jnp.maximum(jnp.minimum(hi1, hi2), jnp.maximum(lo1, lo2))


def _router_body(x_ref, rwt_ref, rb_ref, su_ref, e_ref, w_ref, rank_ref, cnt_ref, carry_ref):
    @pl.when(pl.program_id(0) == 0)
    def _():
        carry_ref[...] = jnp.zeros_like(carry_ref)

    logits = _dot_nt(rwt_ref[...], x_ref[...], lax.Precision.HIGHEST)
    scores = _sigmoid(logits)
    biased = scores + rb_ref[...][:, :1]
    tm = logits.shape[1]
    brow = [biased[e:e + 1, :] for e in range(N_EXPERTS)]
    srow = [scores[e:e + 1, :] for e in range(N_EXPERTS)]
    best = _top2_sum(*brow[0:EXPERTS_PER_GROUP])
    grp = jnp.zeros((1, tm), jnp.int32)
    for g in range(1, N_GROUPS):
        gs = _top2_sum(*brow[g * EXPERTS_PER_GROUP:(g + 1) * EXPERTS_PER_GROUP])
        upd = gs > best
        best = jnp.where(upd, gs, best)
        grp = jnp.where(upd, g, grp)

    def in_group(rows, j):
        out = rows[j]
        for g in range(1, N_GROUPS):
            out = jnp.where(grp == g, rows[g * EXPERTS_PER_GROUP + j], out)
        return out

    ib = [in_group(brow, j) for j in range(EXPERTS_PER_GROUP)]
    isc = [in_group(srow, j) for j in range(EXPERTS_PER_GROUP)]
    v1, i1, s1 = ib[0], jnp.zeros((1, tm), jnp.int32), isc[0]
    for j in range(1, EXPERTS_PER_GROUP):
        upd = ib[j] > v1
        v1 = jnp.where(upd, ib[j], v1)
        i1 = jnp.where(upd, j, i1)
        s1 = jnp.where(upd, isc[j], s1)
    v2 = jnp.full((1, tm), -jnp.inf, F32)
    i2 = jnp.zeros((1, tm), jnp.int32)
    s2 = jnp.zeros((1, tm), F32)
    for j in range(EXPERTS_PER_GROUP):
        upd = jnp.logical_and(i1 != j, ib[j] > v2)
        v2 = jnp.where(upd, ib[j], v2)
        i2 = jnp.where(upd, j, i2)
        s2 = jnp.where(upd, isc[j], s2)
    e1 = grp * EXPERTS_PER_GROUP + i1
    e2 = grp * EXPERTS_PER_GROUP + i2
    total = s1 + s2
    e_ref[0:1, :] = e1
    e_ref[1:2, :] = e2
    w_ref[0:1, :] = s1 / total
    w_ref[1:2, :] = s2 / total
    erow = lax.broadcasted_iota(jnp.int32, (N_EXPERTS, tm), 0)
    oh1 = jnp.where(erow == e1, 1.0, 0.0).astype(F32)
    oh2 = jnp.where(erow == e2, 1.0, 0.0).astype(F32)
    both = oh1 + oh2
    before = _dot(both.astype(BF16), su_ref[...]) + carry_ref[...][:, :1]
    rank_ref[0:1, :] = jnp.sum(oh1 * before, axis=0, keepdims=True).astype(jnp.int32)
    rank_ref[1:2, :] = jnp.sum(oh2 * before, axis=0, keepdims=True).astype(jnp.int32)
    carry = carry_ref[...] + jnp.sum(both, axis=1, keepdims=True)
    carry_ref[...] = carry
    cnt_ref[...] = carry.astype(jnp.int32)


def route_tokens(x, router_w, router_bias, tm=512):
    t, d = x.shape
    strict_upper = jnp.triu(jnp.ones((tm, tm), BF16), k=1)
    fixed = lambda i: (0, 0)
    tok = lambda i: (0, i)
    return pl.pallas_call(
        _router_body,
        grid=(t // tm,),
        in_specs=[pl.BlockSpec((tm, d), lambda i: (i, 0)),
                  pl.BlockSpec((N_EXPERTS, d), fixed),
                  pl.BlockSpec((N_EXPERTS, LANES), fixed),
                  pl.BlockSpec((tm, tm), fixed)],
        out_specs=[pl.BlockSpec((TOP_K, tm), tok), pl.BlockSpec((TOP_K, tm), tok),
                   pl.BlockSpec((TOP_K, tm), tok), pl.BlockSpec((N_EXPERTS, LANES), fixed)],
        out_shape=[jax.ShapeDtypeStruct((TOP_K, t), jnp.int32), jax.ShapeDtypeStruct((TOP_K, t), F32),
                   jax.ShapeDtypeStruct((TOP_K, t), jnp.int32),
                   jax.ShapeDtypeStruct((N_EXPERTS, LANES), jnp.int32)],
        scratch_shapes=[pltpu.VMEM((N_EXPERTS, LANES), F32)],
        compiler_params=_cparams(("arbitrary",)),
        name="route_tokens",
    )(x, router_w.T, jnp.broadcast_to(router_bias.reshape(N_EXPERTS, 1), (N_EXPERTS, LANES)), strict_upper)


def _row_copy(x_hbm, buf, sem, slot, row, tok):
    return pltpu.make_async_copy(x_hbm.at[pl.ds(tok, 1), :], buf.at[slot, pl.ds(row, 1), :], sem.at[slot])


def _expert_body(be_ref, rt_ref, x_hbm, w1_ref, w3_ref, w2_ref, y_ref, buf, sem):
    i = pl.program_id(0)
    n = pl.num_programs(0)

    def wait_slot(slot):
        def wait_one(r, carry):
            _row_copy(x_hbm, buf, sem, slot, r, 0).wait()
            return carry

        lax.fori_loop(0, EXPERT_BLOCK, wait_one, 0, unroll=8)

    @pl.when(i == 0)
    def _():
        def one(r, carry):
            _row_copy(x_hbm, buf, sem, 0, r, rt_ref[r]).start()
            return carry

        lax.fori_loop(0, EXPERT_BLOCK, one, 0, unroll=8)

    slot = i % 2
    wait_slot(slot)
    nxt = jnp.minimum(i + 1, n - 1) * EXPERT_BLOCK
    for r in range(EXPERT_BLOCK):
        _row_copy(x_hbm, buf, sem, 1 - slot, r, rt_ref[nxt + r]).start()
    x = buf[slot].astype(BF16)
    h1 = _dot(x, w1_ref[...])
    h3 = _dot(x, w3_ref[...])
    act = (h1 * _sigmoid(h1) * h3).astype(BF16)
    y_ref[...] = _dot(act, w2_ref[...])

    @pl.when(i == n - 1)
    def _():
        wait_slot(1 - slot)


def expert_ffn(block_expert, row_tok, x, w1, w3, w2):
    t, d = x.shape
    n_rows = row_tok.shape[0]
    n_blocks = n_rows // EXPERT_BLOCK
    grid_spec = pltpu.PrefetchScalarGridSpec(
        num_scalar_prefetch=2,
        grid=(n_blocks,),
        in_specs=[pl.BlockSpec(memory_space=pl.ANY),
                  pl.BlockSpec((None, d, D_EXPERT), lambda i, be, rt: (be[i], 0, 0)),
                  pl.BlockSpec((None, d, D_EXPERT), lambda i, be, rt: (be[i], 0, 0)),
                  pl.BlockSpec((None, D_EXPERT, d), lambda i, be, rt: (be[i], 0, 0))],
        out_specs=pl.BlockSpec((EXPERT_BLOCK, d), lambda i, be, rt: (i, 0)),
        scratch_shapes=[pltpu.VMEM((2, EXPERT_BLOCK, d), F32),
                        pltpu.SemaphoreType.DMA((2,))])
    return pl.pallas_call(
        _expert_body,
        grid_spec=grid_spec,
        out_shape=jax.ShapeDtypeStruct((n_rows, d), F32),
        compiler_params=_cparams(("arbitrary",)),
        name="expert_ffn",
    )(block_expert, row_tok, x, w1, w3, w2)


def _pair_copy(y_hbm, buf, sem, slot, k, row, src):
    return pltpu.make_async_copy(y_hbm.at[pl.ds(src, 1), :], buf.at[slot, k, pl.ds(row, 1), :], sem.at[slot])


def _combine_body(dest_ref, y_hbm, w_ref, x_ref, ln_g_ref, ln_b_ref, of_ref, ob_ref, buf, sem, *, n_tok):
    i = pl.program_id(0)
    n = pl.num_programs(0)
    tm = x_ref.shape[0]

    def issue(blk, slot):
        base = blk * tm

        def one(r, carry):
            for k in range(TOP_K):
                _pair_copy(y_hbm, buf, sem, slot, k, r, dest_ref[k * n_tok + base + r]).start()
            return carry

        lax.fori_loop(0, tm, one, 0, unroll=8)

    @pl.when(i == 0)
    def _():
        issue(0, 0)

    @pl.when(i + 1 < n)
    def _():
        issue(i + 1, (i + 1) % 2)

    slot = i % 2

    def wait_one(r, carry):
        for k in range(TOP_K):
            _pair_copy(y_hbm, buf, sem, slot, k, r, 0).wait()
        return carry

    lax.fori_loop(0, tm, wait_one, 0, unroll=8)
    w = w_ref[...]
    ffn = w[:, 0:1] * buf[slot, 0] + w[:, 1:2] * buf[slot, 1]
    y = _layer_norm(DEEPNORM_ALPHA * x_ref[...] + ffn, ln_g_ref[...], ln_b_ref[...])
    of_ref[...] = y
    ob_ref[...] = y.astype(BF16)


def combine_experts(dest, y_rows, w_pad, x, ln_g, ln_b, tm=256):
    t, d = x.shape
    row = lambda i, *_: (i, 0)
    fixed = lambda i, *_: (0, 0)
    grid_spec = pltpu.PrefetchScalarGridSpec(
        num_scalar_prefetch=1,
        grid=(t // tm,),
        in_specs=[pl.BlockSpec(memory_space=pl.ANY),
                  pl.BlockSpec((tm, LANES), row),
                  pl.BlockSpec((tm, d), row),
                  pl.BlockSpec((1, d), fixed),
                  pl.BlockSpec((1, d), fixed)],
        out_specs=[pl.BlockSpec((tm, d), row), pl.BlockSpec((tm, d), row)],
        scratch_shapes=[pltpu.VMEM((2, TOP_K, tm, d), F32),
                        pltpu.SemaphoreType.DMA((2,))])
    return pl.pallas_call(
        functools.partial(_combine_body, n_tok=t),
        grid_spec=grid_spec,
        out_shape=[jax.ShapeDtypeStruct((t, d), F32), jax.ShapeDtypeStruct((t, d), BF16)],
        compiler_params=_cparams(("arbitrary",)),
        name="combine_experts",
    )(dest, y_rows, w_pad, x, ln_g.reshape(1, d), ln_b.reshape(1, d))


def routed_experts(xf, router_w, router_bias, w1, w3, w2, ln_g, ln_b):
    t, d = xf.shape
    expert, weight, rank, counts = route_tokens(xf, router_w, router_bias)
    counts = counts[:, 0]
    padded = (counts + EXPERT_BLOCK - 1) // EXPERT_BLOCK * EXPERT_BLOCK
    padded_ends = jnp.cumsum(padded)
    padded_starts = padded_ends - padded
    start_of = jnp.zeros_like(expert)
    for e in range(N_EXPERTS):
        start_of = jnp.where(expert == e, padded_starts[e], start_of)
    dest = (start_of + rank).reshape(TOP_K * t)
    n_rows = TOP_K * t + N_EXPERTS * EXPERT_BLOCK
    tok = jnp.tile(jnp.arange(t, dtype=jnp.int32), TOP_K)
    row_tok = jnp.zeros((n_rows,), jnp.int32).at[dest].set(tok)
    block_start = jnp.arange(n_rows // EXPERT_BLOCK, dtype=jnp.int32) * EXPERT_BLOCK
    block_expert = jnp.minimum(jnp.searchsorted(padded_ends, block_start, side='right'),
                               N_EXPERTS - 1).astype(jnp.int32)
    y_rows = expert_ffn(block_expert, row_tok, xf, w1.astype(BF16), w3.astype(BF16), w2.astype(BF16))
    w_pad = jnp.pad(weight.T, ((0, 0), (0, LANES - TOP_K)))
    return combine_experts(dest, y_rows, w_pad, xf, ln_g, ln_b)


def _split_w_in(w_in):
    sizes = (QK_WIDTH, QK_WIDTH, QK_WIDTH, QK_WIDTH, HEADS, HEADS,
             QK_WIDTH, QK_WIDTH, QK_WIDTH, QK_WIDTH, HEADS, HEADS,
             MLA_Q_LORA, MLA_KV_LORA, MLA_ROPE, 3 * D_MODEL)
    parts, acc = [], 0
    for size in sizes:
        parts.append(w_in[:, acc:acc + size])
        acc += size
    return parts


def _arrange_w_in(w_in):
    (g_q, g_k, g_v, g_z, g_a, g_b, m_q, m_k, m_v, m_o, m_i, m_f, c_q, c_kv, k_rope, gates) = _split_w_in(w_in)
    d = w_in.shape[0]
    half = MLA_ROPE // 2
    pad64 = jnp.zeros((d, LANES - MLA_ROPE), w_in.dtype)
    rope_sw = jnp.concatenate([k_rope[:, half:], k_rope[:, :half]], axis=1)
    main = jnp.concatenate([c_q, c_kv, k_rope, pad64, rope_sw, pad64, jnp.zeros((d, LANES), w_in.dtype),
                            g_q, g_k, g_v, g_z, m_q, m_k, m_v, m_o, gates], axis=1)
    small = jnp.concatenate([g_a, g_b, m_i, m_f, jnp.zeros((d, LANES - 4 * HEADS), w_in.dtype)], axis=1)
    return main.astype(BF16), small.astype(BF16)


def _arrange_mla(w_uq, w_ukv):
    half = MLA_ROPE // 2
    wq = w_uq.reshape(MLA_Q_LORA, HEADS, HEAD_DIM + MLA_ROPE)
    nope, rope = wq[:, :, :HEAD_DIM], wq[:, :, HEAD_DIM:]
    pad = jnp.zeros((MLA_Q_LORA, HEADS, LANES - MLA_ROPE), w_uq.dtype)
    wq_a = jnp.concatenate([nope, rope, pad], axis=2).reshape(MLA_Q_LORA, HEADS * 2 * HEAD_DIM)
    rope_sw = jnp.concatenate([rope[:, :, half:], rope[:, :, :half]], axis=2)
    wq_b = jnp.concatenate([rope_sw, pad], axis=2).reshape(MLA_Q_LORA, HEADS * LANES)
    wkv = w_ukv.reshape(MLA_KV_LORA, HEADS, 2 * HEAD_DIM)
    wkv = jnp.concatenate([wkv[:, :, :HEAD_DIM].reshape(MLA_KV_LORA, QK_WIDTH),
                           wkv[:, :, HEAD_DIM:].reshape(MLA_KV_LORA, QK_WIDTH)], axis=1)
    return wq_a.astype(BF16), wq_b.astype(BF16), wkv.astype(BF16)


def kernel(x, positions, ln_in_g, ln_in_b, w_in, gdn_conv, gdn_a_log, gdn_dt_bias, gdn_norm, mlstm_gate_bias, mlstm_norm, mla_q_norm, mla_kv_norm, mla_w_uq, mla_w_ukv, w_br_gdn, w_br_mlstm, w_br_mla, gate_bias, w_out, ln1_g, ln1_b, router_w, router_bias, moe_w1, moe_w3, moe_w2, ln2_g, ln2_b):
    bsz, seq, d = x.shape
    t = bsz * seq
    xf, xb = layer_norm_entry(x.reshape(t, d), ln_in_g, ln_in_b)
    cc, ss = rope_tables(positions)
    fmasks, bmasks = _group_constants()
    for l in range(DEPTH):
        w_main, w_small = _arrange_w_in(w_in[l])
        h, small = in_projection(xb, w_main, w_small)
        small_t = small[:, :4 * HEADS].T
        y_gdn = gated_deltanet_heads(h, small_t, gdn_conv[l], gdn_a_log[l], gdn_dt_bias[l], gdn_norm[l],
                                     fmasks, bmasks, bsz, seq)
        y_mlstm = mlstm_heads(h, small_t, mlstm_gate_bias[l], mlstm_norm[l], fmasks, bmasks, bsz, seq)
        wq_a, wq_b, wkv = _arrange_mla(mla_w_uq[l], mla_w_ukv[l])
        q, kn, kr, v = mla_prepare(h, cc, ss, mla_q_norm[l], mla_kv_norm[l], wq_a, wq_b, wkv)
        y_mla = latent_attention(q, kn, kr, v, bsz, seq)
        xf, xb = merge_branches(y_gdn, y_mlstm, y_mla, h, gate_bias[l], w_br_gdn[l].astype(BF16),
                                w_br_mlstm[l].astype(BF16), w_br_mla[l].astype(BF16),
                                w_out[l].astype(BF16), xf, ln1_g[l], ln1_b[l])
        xf, xb = routed_experts(xf, router_w, router_bias, moe_w1[l], moe_w3[l], moe_w2[l],
                                ln2_g[l], ln2_b[l])
    return xf.reshape(bsz, seq, d)
```

```python
import functools

import jax
import jax.numpy as jnp
from jax import lax
from jax.experimental import pallas as pl
from jax.experimental.pallas import tpu as pltpu

F32 = jnp.float32
BF16 = jnp.bfloat16

D_MODEL = 1024
DEPTH = 2
HEADS = 4
HEAD_DIM = 128
CHUNK = 64
CONV_WIDTH = 4
GATE_CAP = 15.0
MLA_ROPE = 64
MLA_Q_LORA = 384
MLA_KV_LORA = 256
ROPE_THETA = 10000.0
N_EXPERTS = 16
N_GROUPS = 4
EXPERTS_PER_GROUP = 4
TOP_K = 2
D_EXPERT = 512
EXPERT_BLOCK = 256
LN_EPS = 1e-5
RMS_EPS = 1e-6
DEEPNORM_ALPHA = (2 * DEPTH) ** 0.25
LOG2_E = 1.4426950408889634

LANES = 128
QK_WIDTH = HEADS * HEAD_DIM
H_WIDTH = 8192
COL_MLA = 0
COL_GDN = 8
COL_MLSTM = 24
COL_GATE = 5
SEQ_BLOCK = 512
GROUP = 256
VMEM_LIMIT = 48 * 1024 * 1024
MOE_BLOCK = 512
MOE_VMEM_LIMIT = 56 * 1024 * 1024


def _cparams(sem):
    return pltpu.CompilerParams(dimension_semantics=sem, vmem_limit_bytes=VMEM_LIMIT)


def _sigmoid(x):
    return 1.0 / (1.0 + jnp.exp(-x))


def _layer_norm(x, g, b):
    mu = jnp.mean(x, axis=-1, keepdims=True)
    xc = x - mu
    var = jnp.mean(xc * xc, axis=-1, keepdims=True)
    return xc * lax.rsqrt(var + LN_EPS) * g + b


def _dot(a, b):
    return jnp.dot(a, b, preferred_element_type=F32)


def _dot_nt(a, b, precision=None):
    return lax.dot_general(a, b, (((1,), (1,)), ((), ())), preferred_element_type=F32,
                           precision=precision)


def _dot_tn(a, b, precision=None):
    return lax.dot_general(a, b, (((0,), (0,)), ((), ())), preferred_element_type=F32,
                           precision=precision)


def _ln_body(x_ref, g_ref, b_ref, of_ref, ob_ref):
    y = _layer_norm(x_ref[...], g_ref[...], b_ref[...])
    of_ref[...] = y
    ob_ref[...] = y.astype(BF16)


def layer_norm_entry(x, g, b, tm=512):
    t, d = x.shape
    return pl.pallas_call(
        _ln_body,
        grid=(t // tm,),
        in_specs=[pl.BlockSpec((tm, d), lambda i: (i, 0)),
                  pl.BlockSpec((1, d), lambda i: (0, 0)),
                  pl.BlockSpec((1, d), lambda i: (0, 0))],
        out_specs=[pl.BlockSpec((tm, d), lambda i: (i, 0)),
                   pl.BlockSpec((tm, d), lambda i: (i, 0))],
        out_shape=[jax.ShapeDtypeStruct((t, d), F32), jax.ShapeDtypeStruct((t, d), BF16)],
        compiler_params=_cparams(("parallel",)),
        name="ln_entry",
    )(x, g.reshape(1, d), b.reshape(1, d))


def _inproj_body(x_ref, w_ref, ws_ref, h_ref, hs_ref):
    x = x_ref[...]
    h_ref[...] = _dot(x, w_ref[...]).astype(BF16)

    @pl.when(pl.program_id(1) == 0)
    def _():
        hs_ref[...] = _dot(x, ws_ref[...])


def in_projection(xb, w_main, w_small, tm=1024, tn=512):
    t, d = xb.shape
    n = w_main.shape[1]
    return pl.pallas_call(
        _inproj_body,
        grid=(t // tm, n // tn),
        in_specs=[pl.BlockSpec((tm, d), lambda i, j: (i, 0)),
                  pl.BlockSpec((d, tn), lambda i, j: (0, j)),
                  pl.BlockSpec((d, LANES), lambda i, j: (0, 0))],
        out_specs=[pl.BlockSpec((tm, tn), lambda i, j: (i, j)),
                   pl.BlockSpec((tm, LANES), lambda i, j: (i, 0))],
        out_shape=[jax.ShapeDtypeStruct((t, n), BF16), jax.ShapeDtypeStruct((t, LANES), F32)],
        compiler_params=_cparams(("parallel", "arbitrary")),
        name="in_proj",
    )(xb, w_main, w_small)


def _lane_pick(x, lane):
    idx = lax.broadcasted_iota(jnp.int32, x.shape, 1)
    return jnp.sum(jnp.where(idx == lane, x, 0.0), axis=1, keepdims=True)


def _softplus(x):
    return jnp.maximum(x, 0.0) + jnp.log1p(jnp.exp(-jnp.abs(x)))


def _group_masks():
    r = lax.broadcasted_iota(jnp.int32, (GROUP, GROUP), 0)
    c = lax.broadcasted_iota(jnp.int32, (GROUP, GROUP), 1)
    same = (r // CHUNK) == (c // CHUNK)
    causal = jnp.logical_and(same, r >= c)
    strict = jnp.logical_and(same, r > c)
    upper = jnp.logical_and(same, r <= c)
    return causal, strict, upper, r == c


def _split_bf16(x):
    hi = x.astype(BF16)
    return hi, (x - hi.astype(F32)).astype(BF16)


def _group_cumsum_col(col, low_b):
    hi, lo = _split_bf16(jnp.broadcast_to(col, (GROUP, LANES)))
    return (_dot(low_b, hi) + _dot(low_b, lo))[:, :1]


def _group_cumsum_row(row, up_b):
    hi, lo = _split_bf16(jnp.broadcast_to(row, (16, GROUP)))
    return (_dot(hi, up_b) + _dot(lo, up_b))[0:1, :]


def _rms_norm(x, g):
    return x * lax.rsqrt(jnp.mean(x * x, axis=-1, keepdims=True) + RMS_EPS) * g


def _gdn_body(alog_ref, dtb_ref, q_ref, k_ref, v_ref, z_ref, cq_ref, ck_ref, cv_ref,
              sm_ref, smt_ref, ng_ref, o_ref, state_ref, eq_ref, ek_ref, ev_ref):
    hh = pl.program_id(1)
    lb = q_ref.shape[0]

    @pl.when(pl.program_id(2) == 0)
    def _():
        state_ref[...] = jnp.zeros_like(state_ref)
        for e_ref in (eq_ref, ek_ref, ev_ref):
            e_ref[0:8, :] = jnp.zeros((8, HEAD_DIM), F32)

    def conv_silu(x_ref, w_ref, e_ref):
        e_ref[8:, :] = x_ref[...].astype(F32)
        w = w_ref[...]
        y = w[0:1, :] * e_ref[pl.ds(8 - CONV_WIDTH + 1, lb), :]
        for j in range(1, CONV_WIDTH):
            y = y + w[j:j + 1, :] * e_ref[pl.ds(8 - CONV_WIDTH + 1 + j, lb), :]
        e_ref[0:8, :] = e_ref[lb:lb + 8, :]
        return y * _sigmoid(y)

    q = conv_silu(q_ref, cq_ref, eq_ref)
    k = conv_silu(k_ref, ck_ref, ek_ref)
    v = conv_silu(v_ref, cv_ref, ev_ref)
    q = q * lax.rsqrt(jnp.sum(q * q, axis=-1, keepdims=True) + RMS_EPS) * (HEAD_DIM ** -0.5)
    k = k * lax.rsqrt(jnp.sum(k * k, axis=-1, keepdims=True) + RMS_EPS)

    neg_a = -jnp.exp(jnp.full((1, 1), alog_ref[hh], F32))
    dtb = dtb_ref[hh]
    sm = sm_ref[...]
    g_col = neg_a * _softplus(_lane_pick(sm, hh) + dtb)
    beta_col = _sigmoid(_lane_pick(sm, HEADS + hh))
    g_row = neg_a * _softplus(smt_ref[pl.ds(hh, 1), :] + dtb)

    causal, strict, upper, diag = _group_masks()
    low_b = jnp.where(causal, 1.0, 0.0).astype(BF16)
    up_b = jnp.where(upper, 1.0, 0.0).astype(BF16)
    eye = jnp.where(diag, 1.0, 0.0).astype(F32)
    state = state_ref[...]
    z = z_ref[...].astype(F32)
    ng = ng_ref[...]
    for gi in range(lb // GROUP):
        gs = slice(gi * GROUP, (gi + 1) * GROUP)
        qg, kg, bg = q[gs], k[gs], beta_col[gs]
        gc_col = _group_cumsum_col(g_col[gs], low_b)
        gc_row = _group_cumsum_row(g_row[:, gs], up_b)
        decay = jnp.exp(jnp.where(causal, gc_col - gc_row, -jnp.inf))
        kg_b = kg.astype(BF16)
        kb = kg * bg
        a_mat = jnp.where(strict, _dot_nt(kb.astype(BF16), kg_b) * decay, 0.0)
        pw = -a_mat
        t_mat = eye + pw
        for _ in range(5):
            pw_b = pw.astype(BF16)
            pw = _dot(pw_b, pw_b)
            t_mat = t_mat + _dot(t_mat.astype(BF16), pw.astype(BF16))
        e_gc = jnp.exp(gc_col)
        rhs = jnp.concatenate([(v[gs] * bg).astype(BF16), (kb * e_gc).astype(BF16)], axis=1)
        uw = _dot(t_mat.astype(BF16), rhs).astype(BF16)
        qk = (_dot_nt(qg.astype(BF16), kg_b) * decay).astype(BF16)
        qk_uw = _dot(qk, uw)
        o_intra = qk_uw[:, :HEAD_DIM]
        q_eff = qg * e_gc - qk_uw[:, HEAD_DIM:]
        for c in range(GROUP // CHUNK):
            sl = slice(c * CHUNK, (c + 1) * CHUNK)
            gc_last = gc_col[(c + 1) * CHUNK - 1:(c + 1) * CHUNK, :]
            k_tail = (kg[sl] * jnp.exp(gc_last - gc_col[sl])).astype(BF16)
            kt_uw = _dot_tn(k_tail, uw[sl])
            lhs = jnp.concatenate([q_eff[sl], kt_uw[:, HEAD_DIM:]], axis=0).astype(BF16)
            res = _dot(lhs, state.astype(BF16))
            out = res[:CHUNK] + o_intra[sl]
            state = state * jnp.exp(gc_last) - res[CHUNK:] + kt_uw[:, :HEAD_DIM]
            rows = slice(gi * GROUP + c * CHUNK, gi * GROUP + (c + 1) * CHUNK)
            zc = z[rows]
            o_ref[rows, :] = (_rms_norm(out, ng) * (zc * _sigmoid(zc))).astype(o_ref.dtype)
    state_ref[...] = state


def gated_deltanet(h, small, small_t, conv_w, a_log, dt_bias, norm_g, bsz, seq):
    t = h.shape[0]
    lb = SEQ_BLOCK
    nb = seq // lb

    def col(off):
        return pl.BlockSpec((lb, HEAD_DIM), lambda b, hh, s, *_: (b * nb + s, COL_GDN + off + hh))

    def conv(off):
        return pl.BlockSpec((CONV_WIDTH, HEAD_DIM), lambda b, hh, s, *_: (0, off + hh))

    grid_spec = pltpu.PrefetchScalarGridSpec(
        num_scalar_prefetch=2,
        grid=(bsz, HEADS, nb),
        in_specs=[col(0), col(HEADS), col(2 * HEADS), col(3 * HEADS),
                  conv(0), conv(HEADS), conv(2 * HEADS),
                  pl.BlockSpec((lb, LANES), lambda b, hh, s, *_: (b * nb + s, 0)),
                  pl.BlockSpec((4 * HEADS, lb), lambda b, hh, s, *_: (0, b * nb + s)),
                  pl.BlockSpec((1, HEAD_DIM), lambda b, hh, s, *_: (0, 0))],
        out_specs=pl.BlockSpec((lb, HEAD_DIM), lambda b, hh, s, *_: (b * nb + s, hh)),
        scratch_shapes=[pltpu.VMEM((HEAD_DIM, HEAD_DIM), F32),
                        pltpu.VMEM((lb + 8, HEAD_DIM), F32),
                        pltpu.VMEM((lb + 8, HEAD_DIM), F32),
                        pltpu.VMEM((lb + 8, HEAD_DIM), F32)])
    return pl.pallas_call(
        _gdn_body,
        grid_spec=grid_spec,
        out_shape=jax.ShapeDtypeStruct((t, QK_WIDTH), BF16),
        compiler_params=_cparams(("parallel", "parallel", "arbitrary")),
        name="gated_deltanet",
    )(a_log, dt_bias, h, h, h, h, conv_w, conv_w, conv_w, small, small_t, norm_g.reshape(1, HEAD_DIM))


def _soft_cap(x):
    return GATE_CAP * jnp.tanh(x / GATE_CAP)


def _log_sigmoid(x):
    return jnp.minimum(x, 0.0) - jnp.log1p(jnp.exp(-jnp.abs(x)))


def _mlstm_body(gb_ref, q_ref, k_ref, v_ref, o_ref, sm_ref, smt_ref, ng_ref,
                y_ref, c_ref, n_ref, m_ref):
    hh = pl.program_id(1)
    lb = q_ref.shape[0]
    n_groups = lb // GROUP
    per_group = GROUP // CHUNK

    @pl.when(pl.program_id(2) == 0)
    def _():
        c_ref[...] = jnp.zeros_like(c_ref)
        n_ref[...] = jnp.zeros_like(n_ref)
        m_ref[...] = jnp.zeros_like(m_ref)

    q = q_ref[...]
    k = k_ref[...].astype(F32) * (HEAD_DIM ** -0.5)
    v = v_ref[...]
    gb_i = gb_ref[hh]
    gb_f = gb_ref[HEADS + hh]
    sm = sm_ref[...]
    i_col = _soft_cap(_lane_pick(sm, 2 * HEADS + hh) + gb_i)
    f_col = _log_sigmoid(_soft_cap(_lane_pick(sm, 3 * HEADS + hh) + gb_f))
    i_row = _soft_cap(smt_ref[pl.ds(2 * HEADS + hh, 1), :] + gb_i)
    f_row = _log_sigmoid(_soft_cap(smt_ref[pl.ds(3 * HEADS + hh, 1), :] + gb_f))

    causal, _, upper, _ = _group_masks()
    low_b = jnp.where(causal, 1.0, 0.0).astype(BF16)
    up_b = jnp.where(upper, 1.0, 0.0).astype(BF16)

    bc_cols, lkws, b_lasts, lkw_maxes = [], [], [], []
    for gi in range(n_groups):
        gs = slice(gi * GROUP, (gi + 1) * GROUP)
        bc_col = _group_cumsum_col(f_col[gs], low_b)
        bc_cols.append(bc_col)
        for c in range(per_group):
            sl = slice(c * CHUNK, (c + 1) * CHUNK)
            b_last = bc_col[(c + 1) * CHUNK - 1:(c + 1) * CHUNK, :]
            lkw = b_last - bc_col[sl] + i_col[gi * GROUP + c * CHUNK:gi * GROUP + (c + 1) * CHUNK]
            b_lasts.append(b_last)
            lkws.append(lkw)
            lkw_maxes.append(jnp.max(lkw, axis=0, keepdims=True))
    m_st = m_ref[...][:, :1]
    m_prev, m_next, carry_decay = [], [], []
    for ci in range(n_groups * per_group):
        m_new = jnp.maximum(b_lasts[ci] + m_st, lkw_maxes[ci])
        m_prev.append(m_st)
        m_next.append(m_new)
        carry_decay.append(jnp.exp(b_lasts[ci] + m_st - m_new))
        m_st = m_new

    c_st = c_ref[...]
    n_st = n_ref[...]
    o_pre = o_ref[...].astype(F32)
    ng = ng_ref[...]
    for gi in range(n_groups):
        gs = slice(gi * GROUP, (gi + 1) * GROUP)
        qg, vg = q[gs], v[gs]
        kg_b = k[gs].astype(BF16)
        bc_col = bc_cols[gi]
        bc_row = _group_cumsum_row(f_row[:, gs], up_b)
        log_d = jnp.where(causal, bc_col - bc_row + i_row[:, gs], -jnp.inf)
        m_prev_col = jnp.concatenate(
            [jnp.broadcast_to(m_prev[gi * per_group + c], (CHUNK, 1)) for c in range(per_group)], axis=0)
        log_inter = bc_col + m_prev_col
        m_t = jnp.maximum(log_inter, jnp.max(log_d, axis=-1, keepdims=True))
        w_inter = jnp.exp(log_inter - m_t)
        s = _dot_nt(qg, kg_b) * jnp.exp(log_d - m_t)
        s_v = _dot(s.astype(BF16), vg)
        s_sum = jnp.sum(s, axis=-1, keepdims=True)
        floor = jnp.exp(-m_t)
        for c in range(per_group):
            ci = gi * per_group + c
            sl = slice(c * CHUNK, (c + 1) * CHUNK)
            rows = slice(gi * GROUP + c * CHUNK, gi * GROUP + (c + 1) * CHUNK)
            qc = qg[sl]
            num = w_inter[sl] * _dot(qc, c_st.astype(BF16)) + s_v[sl]
            den = w_inter[sl] * jnp.sum(qc.astype(F32) * n_st, axis=-1, keepdims=True) + s_sum[sl]
            hid = num / jnp.maximum(jnp.abs(den), floor[sl])
            kw = k[rows] * jnp.exp(lkws[ci] - m_next[ci])
            c_st = carry_decay[ci] * c_st + _dot_tn(kw.astype(BF16), vg[sl])
            n_st = carry_decay[ci] * n_st + jnp.sum(kw, axis=0, keepdims=True)
            y_ref[rows, :] = (_sigmoid(o_pre[rows]) * _rms_norm(hid, ng)).astype(y_ref.dtype)
    c_ref[...] = c_st
    n_ref[...] = n_st
    m_ref[...] = jnp.broadcast_to(m_st, m_ref.shape)


def mlstm(h, small, small_t, gate_bias, norm_g, bsz, seq):
    t = h.shape[0]
    lb = SEQ_BLOCK
    nb = seq // lb

    def col(off):
        return pl.BlockSpec((lb, HEAD_DIM), lambda b, hh, s, *_: (b * nb + s, COL_MLSTM + off + hh))

    grid_spec = pltpu.PrefetchScalarGridSpec(
        num_scalar_prefetch=1,
        grid=(bsz, HEADS, nb),
        in_specs=[col(0), col(HEADS), col(2 * HEADS), col(3 * HEADS),
                  pl.BlockSpec((lb, LANES), lambda b, hh, s, *_: (b * nb + s, 0)),
                  pl.BlockSpec((4 * HEADS, lb), lambda b, hh, s, *_: (0, b * nb + s)),
                  pl.BlockSpec((1, HEAD_DIM), lambda b, hh, s, *_: (0, hh))],
        out_specs=pl.BlockSpec((lb, HEAD_DIM), lambda b, hh, s, *_: (b * nb + s, hh)),
        scratch_shapes=[pltpu.VMEM((HEAD_DIM, HEAD_DIM), F32),
                        pltpu.VMEM((1, HEAD_DIM), F32),
                        pltpu.VMEM((1, LANES), F32)])
    return pl.pallas_call(
        _mlstm_body,
        grid_spec=grid_spec,
        out_shape=jax.ShapeDtypeStruct((t, QK_WIDTH), BF16),
        compiler_params=_cparams(("parallel", "parallel", "arbitrary")),
        name="mlstm",
    )(gate_bias, h, h, h, h, small, small_t, norm_g.reshape(1, QK_WIDTH))


def _group_constants():
    r = jnp.arange(GROUP, dtype=jnp.int32)[:, None]
    c = jnp.arange(GROUP, dtype=jnp.int32)[None, :]
    same = (r // CHUNK) == (c // CHUNK)
    neg = jnp.where(same & (r >= c), 0.0, -jnp.inf).astype(F32)
    strict = (same & (r > c)).astype(F32)
    eye = (r == c).astype(F32)
    upper = (same & (r <= c)).astype(BF16)
    last = (r == (c // CHUNK) * CHUNK + CHUNK - 1).astype(BF16)
    return jnp.stack([neg, strict, eye]), jnp.stack([upper, last, eye.astype(BF16)])


def _split3(x):
    hi = x.astype(BF16)
    r1 = x - hi.astype(F32)
    mid = r1.astype(BF16)
    return hi, mid, (r1 - mid.astype(F32)).astype(BF16)


def _rows_times(rows8, mat_b, terms):
    rows = jnp.concatenate([rows8, jnp.zeros_like(rows8)], axis=0)
    parts = _split3(rows)[:terms]
    out = _dot(parts[0], mat_b)
    for p in parts[1:]:
        out = out + _dot(p, mat_b)
    return out[0:8]


def _rows_to_cols(stack, eye_b):
    parts = _split3(stack)
    out = _dot_nt(eye_b, parts[0])
    for p in parts[1:]:
        out = out + _dot_nt(eye_b, p)
    return out


def _conv_silu(x_ref, w, e_ref):
    lb = x_ref.shape[0]
    e_ref[8:, :] = x_ref[...].astype(F32)
    y = w[0:1, :] * e_ref[pl.ds(8 - CONV_WIDTH + 1, lb), :]
    for j in range(1, CONV_WIDTH):
        y = y + w[j:j + 1, :] * e_ref[pl.ds(8 - CONV_WIDTH + 1 + j, lb), :]
    e_ref[0:8, :] = e_ref[lb:lb + 8, :]
    return y * _sigmoid(y)


def _gdn_heads_body(q_ref, k_ref, v_ref, z_ref, cw_ref, smt_ref, gp_ref, ng_ref, fm_ref, bm_ref,
                    o_ref, state_ref, eq_ref, ek_ref, ev_ref):
    lb = q_ref.shape[0]

    @pl.when(pl.program_id(1) == 0)
    def _():
        state_ref[...] = jnp.zeros_like(state_ref)
        for e_ref in (eq_ref, ek_ref, ev_ref):
            e_ref[0:8, :] = jnp.zeros((8, QK_WIDTH), F32)

    cw = cw_ref[...]
    q_all = _conv_silu(q_ref, cw[:, 0:QK_WIDTH], eq_ref)
    k_all = _conv_silu(k_ref, cw[:, QK_WIDTH:2 * QK_WIDTH], ek_ref)
    v_all = _conv_silu(v_ref, cw[:, 2 * QK_WIDTH:3 * QK_WIDTH], ev_ref)
    z_all = z_ref[...].astype(F32)
    ng = ng_ref[...]
    neg, strict01, eye = fm_ref[0], fm_ref[1], fm_ref[2]
    up_b, last_b, eye_b = bm_ref[0], bm_ref[1], bm_ref[2]

    gp = gp_ref[...]
    neg_a8 = -jnp.exp(gp[0:8, 0:1])
    dtb8 = gp[8:16, 0:1]
    g8 = neg_a8 * _softplus(smt_ref[0:8, :] + dtb8)
    beta8 = _sigmoid(smt_ref[HEADS:HEADS + 8, :])
    gc8 = _rows_times(g8, up_b, 2)
    gl8 = _rows_times(gc8, last_b, 3)
    egc8 = jnp.exp(gc8)
    tail8 = jnp.exp(gl8 - gc8)
    elast8 = jnp.exp(gl8)
    stack = jnp.concatenate([gc8, beta8, egc8, tail8, beta8 * egc8,
                             jnp.zeros((LANES - 40, lb), F32)], axis=0)
    cols = _rows_to_cols(stack, eye_b)

    hds = range(HEADS)
    lanes = [slice(hd * HEAD_DIM, (hd + 1) * HEAD_DIM) for hd in hds]
    col = lambda j, hd: cols[:, 8 * j + hd:8 * j + hd + 1]
    qg = [q_all[:, lanes[hd]] for hd in hds]
    kg = [k_all[:, lanes[hd]] for hd in hds]
    qg = [x * lax.rsqrt(jnp.sum(x * x, axis=-1, keepdims=True) + RMS_EPS) * (HEAD_DIM ** -0.5) for x in qg]
    kg = [x * lax.rsqrt(jnp.sum(x * x, axis=-1, keepdims=True) + RMS_EPS) for x in kg]
    kg_b = [x.astype(BF16) for x in kg]
    decay = [jnp.exp(col(0, hd) - gc8[hd:hd + 1, :] + neg) for hd in hds]
    a_mat = [_dot_nt((kg[hd] * col(1, hd)).astype(BF16), kg_b[hd]) * decay[hd] * strict01 for hd in hds]
    pw = [-a for a in a_mat]
    t_mat = [eye + p for p in pw]
    for _ in range(5):
        pw_b = [p.astype(BF16) for p in pw]
        pw = [_dot(p, p) for p in pw_b]
        t_mat = [t_mat[hd] + _dot(t_mat[hd].astype(BF16), pw[hd].astype(BF16)) for hd in hds]
    rhs = [jnp.concatenate([(v_all[:, lanes[hd]] * col(1, hd)).astype(BF16),
                            (kg[hd] * col(4, hd)).astype(BF16)], axis=1) for hd in hds]
    uw = [_dot(t_mat[hd].astype(BF16), rhs[hd]).astype(BF16) for hd in hds]
    qk = [(_dot_nt(qg[hd].astype(BF16), kg_b[hd]) * decay[hd]).astype(BF16) for hd in hds]
    qk_uw = [_dot(qk[hd], uw[hd]) for hd in hds]
    q_eff = [qg[hd] * col(2, hd) - qk_uw[hd][:, HEAD_DIM:] for hd in hds]
    k_tail = [(kg[hd] * col(3, hd)).astype(BF16) for hd in hds]
    state = [state_ref[hd] for hd in hds]
    for c in range(lb // CHUNK):
        sl = slice(c * CHUNK, (c + 1) * CHUNK)
        kt_uw = [_dot_tn(k_tail[hd][sl], uw[hd][sl]) for hd in hds]
        lhs = [jnp.concatenate([q_eff[hd][sl], kt_uw[hd][:, HEAD_DIM:]], axis=0).astype(BF16) for hd in hds]
        res = [_dot(lhs[hd], state[hd].astype(BF16)) for hd in hds]
        state = [state[hd] * elast8[hd:hd + 1, c * CHUNK:c * CHUNK + 1] - res[hd][CHUNK:]
                 + kt_uw[hd][:, :HEAD_DIM] for hd in hds]
        for hd in hds:
            out = res[hd][:CHUNK] + qk_uw[hd][sl, :HEAD_DIM]
            zc = z_all[sl, lanes[hd]]
            o_ref[sl, lanes[hd]] = (_rms_norm(out, ng) * (zc * _sigmoid(zc))).astype(o_ref.dtype)
    for hd in hds:
        state_ref[hd] = state[hd]


def _gate_params(first, second):
    out = jnp.zeros((16, LANES), F32)
    out = out.at[0:HEADS, :].set(jnp.broadcast_to(first.astype(F32)[:, None], (HEADS, LANES)))
    return out.at[8:8 + HEADS, :].set(jnp.broadcast_to(second.astype(F32)[:, None], (HEADS, LANES)))


def gated_deltanet_heads(h, small_t, conv_w, a_log, dt_bias, norm_g, fmasks, bmasks, bsz, seq):
    t = h.shape[0]
    lb = GROUP
    nb = seq // lb
    wide = QK_WIDTH
    first = COL_GDN * LANES // wide

    def col(j):
        return pl.BlockSpec((lb, wide), lambda b, s: (b * nb + s, first + j))

    fixed2 = lambda b, s: (0, 0)
    fixed3 = lambda b, s: (0, 0, 0)
    return pl.pallas_call(
        _gdn_heads_body,
        grid=(bsz, nb),
        in_specs=[col(0), col(1), col(2), col(3),
                  pl.BlockSpec((CONV_WIDTH, 3 * wide), fixed2),
                  pl.BlockSpec((4 * HEADS, lb), lambda b, s: (0, b * nb + s)),
                  pl.BlockSpec((16, LANES), fixed2),
                  pl.BlockSpec((1, HEAD_DIM), fixed2),
                  pl.BlockSpec((3, GROUP, GROUP), fixed3),
                  pl.BlockSpec((3, GROUP, GROUP), fixed3)],
        out_specs=pl.BlockSpec((lb, wide), lambda b, s: (b * nb + s, 0)),
        out_shape=jax.ShapeDtypeStruct((t, wide), BF16),
        scratch_shapes=[pltpu.VMEM((HEADS, HEAD_DIM, HEAD_DIM), F32),
                        pltpu.VMEM((lb + 8, wide), F32),
                        pltpu.VMEM((lb + 8, wide), F32),
                        pltpu.VMEM((lb + 8, wide), F32)],
        compiler_params=_cparams(("parallel", "arbitrary")),
        name="gated_deltanet",
    )(h, h, h, h, conv_w, small_t, _gate_params(a_log, dt_bias), norm_g.reshape(1, HEAD_DIM), fmasks, bmasks)


def _mlstm_heads_body(q_ref, k_ref, v_ref, o_ref, smt_ref, gp_ref, ng_ref, fm_ref, bm_ref,
                      y_ref, c_ref, n_ref, m_ref):
    lb = q_ref.shape[0]
    n_chunks = lb // CHUNK

    @pl.when(pl.program_id(1) == 0)
    def _():
        c_ref[...] = jnp.zeros_like(c_ref)
        n_ref[...] = jnp.zeros_like(n_ref)
        m_ref[...] = jnp.zeros_like(m_ref)

    hds = range(HEADS)
    lanes = [slice(hd * HEAD_DIM, (hd + 1) * HEAD_DIM) for hd in hds]
    q_all = q_ref[...]
    k_all = k_ref[...].astype(F32) * (HEAD_DIM ** -0.5)
    qg = [q_all[:, lanes[hd]] for hd in hds]
    kg = [k_all[:, lanes[hd]] for hd in hds]
    qk = [_dot_nt(qg[hd], kg[hd].astype(BF16)) for hd in hds]

    neg = fm_ref[0]
    up_b, last_b, eye_b = bm_ref[0], bm_ref[1], bm_ref[2]
    capped = _soft_cap(smt_ref[2 * HEADS:4 * HEADS, :] + gp_ref[...][0:8, 0:1])
    i8 = capped
    f8 = pltpu.roll(_log_sigmoid(capped), HEADS, axis=0)
    bc8 = _rows_times(f8, up_b, 2)
    bl8 = _rows_times(bc8, last_b, 3)
    lkw8 = bl8 - bc8 + i8
    chunk_id = lax.broadcasted_iota(jnp.int32, (8, lb), 1) // CHUNK
    m_st = m_ref[...][:, 0:1]
    m_prev_row = jnp.zeros((8, lb), F32)
    m_next_row = jnp.zeros((8, lb), F32)
    carry_decay = []
    for c in range(n_chunks):
        in_c = chunk_id == c
        b_last = bl8[:, c * CHUNK:c * CHUNK + 1]
        m_new = jnp.maximum(b_last + m_st, jnp.max(jnp.where(in_c, lkw8, -jnp.inf), axis=1, keepdims=True))
        carry_decay.append(jnp.exp(b_last + m_st - m_new))
        m_prev_row = jnp.where(in_c, m_st, m_prev_row)
        m_next_row = jnp.where(in_c, m_new, m_next_row)
        m_st = m_new
    m_ref[...] = jnp.broadcast_to(m_st, m_ref.shape)
    stack = jnp.concatenate([bc8, bc8 + m_prev_row, jnp.exp(lkw8 - m_next_row),
                             jnp.zeros((LANES - 24, lb), F32)], axis=0)
    cols = _rows_to_cols(stack, eye_b)

    v_all = v_ref[...]
    o_all = o_ref[...].astype(F32)
    ng_all = ng_ref[...]
    n_all = n_ref[...]
    col = lambda j, hd: cols[:, 8 * j + hd:8 * j + hd + 1]
    vg = [v_all[:, lanes[hd]] for hd in hds]
    log_d = [col(0, hd) - bc8[hd:hd + 1, :] + i8[hd:hd + 1, :] + neg for hd in hds]
    m_t = [jnp.maximum(col(1, hd), jnp.max(log_d[hd], axis=-1, keepdims=True)) for hd in hds]
    w_inter = [jnp.exp(col(1, hd) - m_t[hd]) for hd in hds]
    s = [qk[hd] * jnp.exp(log_d[hd] - m_t[hd]) for hd in hds]
    ones = jnp.ones((lb, HEAD_DIM), BF16)
    s_vx = [_dot(s[hd].astype(BF16), jnp.concatenate([vg[hd], ones], axis=1)) for hd in hds]
    s_v = [x[:, :HEAD_DIM] for x in s_vx]
    s_sum = [x[:, HEAD_DIM:HEAD_DIM + 1] for x in s_vx]
    floor = [jnp.exp(-m_t[hd]) for hd in hds]
    kw = [kg[hd] * col(2, hd) for hd in hds]
    kw_b = [x.astype(BF16) for x in kw]
    c_st = [c_ref[hd] for hd in hds]
    n_st = [n_all[hd:hd + 1, :] for hd in hds]
    for c in range(n_chunks):
        sl = slice(c * CHUNK, (c + 1) * CHUNK)
        q_c = [_dot(qg[hd][sl], c_st[hd].astype(BF16)) for hd in hds]
        q_n = [jnp.sum(qg[hd][sl].astype(F32) * n_st[hd], axis=-1, keepdims=True) for hd in hds]
        cd = [carry_decay[c][hd:hd + 1, :] for hd in hds]
        c_st = [cd[hd] * c_st[hd] + _dot_tn(kw_b[hd][sl], vg[hd][sl]) for hd in hds]
        n_st = [cd[hd] * n_st[hd] + jnp.sum(kw[hd][sl], axis=0, keepdims=True) for hd in hds]
        for hd in hds:
            num = w_inter[hd][sl] * q_c[hd] + s_v[hd][sl]
            den = w_inter[hd][sl] * q_n[hd] + s_sum[hd][sl]
            hid = num / jnp.maximum(jnp.abs(den), floor[hd][sl])
            y_ref[sl, lanes[hd]] = (_sigmoid(o_all[sl, lanes[hd]])
                                    * _rms_norm(hid, ng_all[:, lanes[hd]])).astype(y_ref.dtype)
    for hd in hds:
        c_ref[hd] = c_st[hd]
        n_ref[hd:hd + 1, :] = n_st[hd]


def mlstm_heads(h, small_t, gate_bias, norm_g, fmasks, bmasks, bsz, seq):
    t = h.shape[0]
    lb = GROUP
    nb = seq // lb
    wide = QK_WIDTH
    first = COL_MLSTM * LANES // wide

    def col(j):
        return pl.BlockSpec((lb, wide), lambda b, s: (b * nb + s, first + j))

    fixed2 = lambda b, s: (0, 0)
    fixed3 = lambda b, s: (0, 0, 0)
    gp = jnp.zeros((16, LANES), F32).at[0:2 * HEADS, :].set(
        jnp.broadcast_to(gate_bias.astype(F32)[:, None], (2 * HEADS, LANES)))
    return pl.pallas_call(
        _mlstm_heads_body,
        grid=(bsz, nb),
        in_specs=[col(0), col(1), col(2), col(3),
                  pl.BlockSpec((4 * HEADS, lb), lambda b, s: (0, b * nb + s)),
                  pl.BlockSpec((16, LANES), fixed2),
                  pl.BlockSpec((1, wide), fixed2),
                  pl.BlockSpec((3, GROUP, GROUP), fixed3),
                  pl.BlockSpec((3, GROUP, GROUP), fixed3)],
        out_specs=pl.BlockSpec((lb, wide), lambda b, s: (b * nb + s, 0)),
        out_shape=jax.ShapeDtypeStruct((t, wide), BF16),
        scratch_shapes=[pltpu.VMEM((HEADS, HEAD_DIM, HEAD_DIM), F32),
                        pltpu.VMEM((8, HEAD_DIM), F32),
                        pltpu.VMEM((8, LANES), F32)],
        compiler_params=_cparams(("parallel", "arbitrary")),
        name="mlstm",
    )(h, h, h, h, small_t, gp, norm_g.reshape(1, wide), fmasks, bmasks)


def _rope_table_body(pos_ref, freq_ref, sign_ref, cc_ref, ss_ref):
    ang = pos_ref[...] * freq_ref[...]
    sign = sign_ref[...]
    cc_ref[...] = jnp.cos(ang) * jnp.abs(sign)
    ss_ref[...] = jnp.sin(ang) * sign


def rope_tables(positions, tm=512):
    t = positions.size
    half = MLA_ROPE // 2
    inv_freq = 1.0 / (ROPE_THETA ** (jnp.arange(0, MLA_ROPE, 2, dtype=F32) / MLA_ROPE))
    zeros = jnp.zeros((LANES - MLA_ROPE,), F32)
    freq = jnp.concatenate([inv_freq, inv_freq, zeros]).reshape(1, LANES)
    sign = jnp.concatenate([-jnp.ones((half,), F32), jnp.ones((half,), F32), zeros]).reshape(1, LANES)
    return pl.pallas_call(
        _rope_table_body,
        grid=(t // tm,),
        in_specs=[pl.BlockSpec((tm, 1), lambda i: (i, 0)),
                  pl.BlockSpec((1, LANES), lambda i: (0, 0)),
                  pl.BlockSpec((1, LANES), lambda i: (0, 0))],
        out_specs=[pl.BlockSpec((tm, LANES), lambda i: (i, 0)),
                   pl.BlockSpec((tm, LANES), lambda i: (i, 0))],
        out_shape=[jax.ShapeDtypeStruct((t, LANES), F32), jax.ShapeDtypeStruct((t, LANES), F32)],
        compiler_params=_cparams(("parallel",)),
        name="rope_tables",
    )(positions.astype(F32).reshape(t, 1), freq, sign)


def _mla_pre_body(h_ref, cc_ref, ss_ref, qg_ref, kvg_ref, wqa_ref, wqb_ref, wkv_ref,
                  q_ref, kn_ref, kr_ref, v_ref):
    hblk = h_ref[...].astype(F32)
    cc = cc_ref[...]
    ss = ss_ref[...]
    cq = _rms_norm(hblk[:, :MLA_Q_LORA], qg_ref[...]).astype(BF16)
    ckv = _rms_norm(hblk[:, MLA_Q_LORA:MLA_Q_LORA + MLA_KV_LORA], kvg_ref[...]).astype(BF16)
    off = MLA_Q_LORA + MLA_KV_LORA
    kr_ref[...] = (hblk[:, off:off + LANES] * cc + hblk[:, off + LANES:off + 2 * LANES] * ss).astype(BF16)
    kv = _dot(ckv, wkv_ref[...])
    kn_ref[...] = kv[:, :QK_WIDTH].astype(BF16)
    v_ref[...] = kv[:, QK_WIDTH:].astype(BF16)
    qa = _dot(cq, wqa_ref[...])
    qb = _dot(cq, wqb_ref[...])
    scale = (HEAD_DIM + MLA_ROPE) ** -0.5 * LOG2_E
    for hh in range(HEADS):
        base = 2 * HEAD_DIM * hh
        q_ref[:, base:base + HEAD_DIM] = (qa[:, base:base + HEAD_DIM] * scale).astype(BF16)
        rope = qa[:, base + HEAD_DIM:base + 2 * HEAD_DIM] * cc + qb[:, hh * LANES:(hh + 1) * LANES] * ss
        q_ref[:, base + HEAD_DIM:base + 2 * HEAD_DIM] = (rope * scale).astype(BF16)


def mla_prepare(h, cc, ss, q_norm_g, kv_norm_g, wq_a, wq_b, wkv, tm=512):
    t = h.shape[0]
    row = lambda i: (i, 0)
    fixed = lambda i: (0, 0)
    return pl.pallas_call(
        _mla_pre_body,
        grid=(t // tm,),
        in_specs=[pl.BlockSpec((tm, 1024), row),
                  pl.BlockSpec((tm, LANES), row),
                  pl.BlockSpec((tm, LANES), row),
                  pl.BlockSpec((1, MLA_Q_LORA), fixed),
                  pl.BlockSpec((1, MLA_KV_LORA), fixed),
                  pl.BlockSpec(wq_a.shape, fixed),
                  pl.BlockSpec(wq_b.shape, fixed),
                  pl.BlockSpec(wkv.shape, fixed)],
        out_specs=[pl.BlockSpec((tm, 2 * QK_WIDTH), row),
                   pl.BlockSpec((tm, QK_WIDTH), row),
                   pl.BlockSpec((tm, LANES), row),
                   pl.BlockSpec((tm, QK_WIDTH), row)],
        out_shape=[jax.ShapeDtypeStruct((t, 2 * QK_WIDTH), BF16),
                   jax.ShapeDtypeStruct((t, QK_WIDTH), BF16),
                   jax.ShapeDtypeStruct((t, LANES), BF16),
                   jax.ShapeDtypeStruct((t, QK_WIDTH), BF16)],
        compiler_params=_cparams(("parallel",)),
        name="mla_prepare",
    )(h, cc, ss, q_norm_g.reshape(1, -1), kv_norm_g.reshape(1, -1), wq_a, wq_b, wkv)


def _attn_body(q_ref, kn_ref, kr_ref, v_ref, o_ref, *, tk, n_heads):
    qi = pl.program_id(2)
    tq = q_ref.shape[0]
    hds = range(n_heads)
    lanes = [slice(hd * HEAD_DIM, (hd + 1) * HEAD_DIM) for hd in hds]
    q = [q_ref[:, 2 * hd * HEAD_DIM:2 * (hd + 1) * HEAD_DIM] for hd in hds]

    def step(j, carry, masked):
        m, l, acc = carry
        start = pl.multiple_of(j * tk, tk)
        kr = kr_ref[pl.ds(start, tk), :]
        s = [_dot_nt(q[hd], jnp.concatenate([kn_ref[pl.ds(start, tk), lanes[hd]], kr], axis=1)) for hd in hds]
        if masked:
            r = lax.broadcasted_iota(jnp.int32, (tq, tk), 0)
            c = lax.broadcasted_iota(jnp.int32, (tq, tk), 1)
            s = [jnp.where(r >= c, x, -jnp.inf) for x in s]
        m_new = [jnp.maximum(m[hd], jnp.max(s[hd], axis=-1, keepdims=True)) for hd in hds]
        alpha = [jnp.exp2(m[hd] - m_new[hd]) for hd in hds]
        p = [jnp.exp2(s[hd] - m_new[hd]) for hd in hds]
        l_new = [alpha[hd] * l[hd] + jnp.sum(p[hd], axis=-1, keepdims=True) for hd in hds]
        p_b = [x.astype(BF16) for x in p]
        acc = [alpha[hd] * acc[hd] + _dot(p_b[hd], v_ref[pl.ds(start, tk), lanes[hd]]) for hd in hds]
        return tuple(m_new), tuple(l_new), tuple(acc)

    init = (tuple(jnp.full((tq, 1), -jnp.inf, F32) for _ in hds),
            tuple(jnp.zeros((tq, 1), F32) for _ in hds),
            tuple(jnp.zeros((tq, HEAD_DIM), F32) for _ in hds))
    carry = lax.fori_loop(0, qi, lambda j, cr: step(j, cr, False), init)
    _, l, acc = step(qi, carry, True)
    for hd in hds:
        o_ref[:, lanes[hd]] = (acc[hd] / l[hd]).astype(o_ref.dtype)


def latent_attention(q, kn, kr, v, bsz, seq, tq=512, n_heads=2):
    t = q.shape[0]
    nq = seq // tq
    wide = n_heads * HEAD_DIM
    return pl.pallas_call(
        functools.partial(_attn_body, tk=tq, n_heads=n_heads),
        grid=(bsz, HEADS // n_heads, nq),
        in_specs=[pl.BlockSpec((tq, 2 * wide), lambda b, hh, i: (b * nq + i, hh)),
                  pl.BlockSpec((seq, wide), lambda b, hh, i: (b, hh)),
                  pl.BlockSpec((seq, LANES), lambda b, hh, i: (b, 0)),
                  pl.BlockSpec((seq, wide), lambda b, hh, i: (b, hh))],
        out_specs=pl.BlockSpec((tq, wide), lambda b, hh, i: (b * nq + i, hh)),
        out_shape=jax.ShapeDtypeStruct((t, QK_WIDTH), BF16),
        compiler_params=_cparams(("parallel", "parallel", "arbitrary")),
        name="latent_attention",
    )(q, kn, kr, v)


def _merge_body(yg_ref, ym_ref, ya_ref, g0_ref, g1_ref, g2_ref, gb_ref, pg_ref, pm_ref, pa_ref,
                wo_ref, x_ref, ln_g_ref, ln_b_ref, of_ref, ob_ref):
    gb = gb_ref[...]

    def branch(y_ref, p_ref, g_ref, idx):
        gate = _sigmoid(g_ref[...].astype(F32) + gb[:, idx * D_MODEL:(idx + 1) * D_MODEL])
        return gate * _dot(y_ref[...], p_ref[...])

    merged = branch(yg_ref, pg_ref, g0_ref, 0) + branch(ym_ref, pm_ref, g1_ref, 1) \
        + branch(ya_ref, pa_ref, g2_ref, 2)
    mix = _dot(merged.astype(BF16), wo_ref[...])
    y = _layer_norm(DEEPNORM_ALPHA * x_ref[...] + mix, ln_g_ref[...], ln_b_ref[...])
    of_ref[...] = y
    ob_ref[...] = y.astype(BF16)


def merge_branches(y_gdn, y_mlstm, y_mla, h, gate_bias, p_gdn, p_mlstm, p_mla, w_out, x, ln_g, ln_b, tm=512):
    t, d = x.shape
    row = lambda i: (i, 0)
    fixed = lambda i: (0, 0)
    ybs = pl.BlockSpec((tm, QK_WIDTH), row)
    pbs = pl.BlockSpec((QK_WIDTH, d), fixed)
    return pl.pallas_call(
        _merge_body,
        grid=(t // tm,),
        in_specs=[ybs, ybs, ybs,
                  pl.BlockSpec((tm, d), lambda i: (i, COL_GATE)),
                  pl.BlockSpec((tm, d), lambda i: (i, COL_GATE + 1)),
                  pl.BlockSpec((tm, d), lambda i: (i, COL_GATE + 2)),
                  pl.BlockSpec((1, 3 * d), fixed),
                  pbs, pbs, pbs,
                  pl.BlockSpec((d, d), fixed),
                  pl.BlockSpec((tm, d), row),
                  pl.BlockSpec((1, d), fixed),
                  pl.BlockSpec((1, d), fixed)],
        out_specs=[pl.BlockSpec((tm, d), row), pl.BlockSpec((tm, d), row)],
        out_shape=[jax.ShapeDtypeStruct((t, d), F32), jax.ShapeDtypeStruct((t, d), BF16)],
        compiler_params=_cparams(("parallel",)),
        name="merge_branches",
    )(y_gdn, y_mlstm, y_mla, h, h, h, gate_bias.reshape(1, 3 * d), p_gdn, p_mlstm, p_mla, w_out, x,
      ln_g.reshape(1, d), ln_b.reshape(1, d))


def _top2_sum(a, b, c, d):
    hi1, lo1 = jnp.maximum(a, b), jnp.minimum(a, b)
    hi2, lo2 = jnp.maximum(c, d), jnp.minimum(c, d)
    return jnp.maximum(hi1, hi2) + jnp.maximum(jnp.minimum(hi1, hi2), jnp.maximum(lo1, lo2))


def _router_body(x_ref, rwt_ref, rb_ref, su_ref, e_ref, w_ref, rank_ref, cnt_ref, carry_ref):
    @pl.when(pl.program_id(0) == 0)
    def _():
        carry_ref[...] = jnp.zeros_like(carry_ref)

    logits = _dot_nt(rwt_ref[...], x_ref[...], lax.Precision.HIGHEST)
    scores = _sigmoid(logits)
    biased = scores + rb_ref[...][:, :1]
    tm = logits.shape[1]
    brow = [biased[e:e + 1, :] for e in range(N_EXPERTS)]
    srow = [scores[e:e + 1, :] for e in range(N_EXPERTS)]
    best = _top2_sum(*brow[0:EXPERTS_PER_GROUP])
    grp = jnp.zeros((1, tm), jnp.int32)
    for g in range(1, N_GROUPS):
        gs = _top2_sum(*brow[g * EXPERTS_PER_GROUP:(g + 1) * EXPERTS_PER_GROUP])
        upd = gs > best
        best = jnp.where(upd, gs, best)
        grp = jnp.where(upd, g, grp)

    def in_group(rows, j):
        out = rows[j]
        for g in range(1, N_GROUPS):
            out = jnp.where(grp == g, rows[g * EXPERTS_PER_GROUP + j], out)
        return out

    ib = [in_group(brow, j) for j in range(EXPERTS_PER_GROUP)]
    isc = [in_group(srow, j) for j in range(EXPERTS_PER_GROUP)]
    v1, i1, s1 = ib[0], jnp.zeros((1, tm), jnp.int32), isc[0]
    for j in range(1, EXPERTS_PER_GROUP):
        upd = ib[j] > v1
        v1 = jnp.where(upd, ib[j], v1)
        i1 = jnp.where(upd, j, i1)
        s1 = jnp.where(upd, isc[j], s1)
    v2 = jnp.full((1, tm), -jnp.inf, F32)
    i2 = jnp.zeros((1, tm), jnp.int32)
    s2 = jnp.zeros((1, tm), F32)
    for j in range(EXPERTS_PER_GROUP):
        upd = jnp.logical_and(i1 != j, ib[j] > v2)
        v2 = jnp.where(upd, ib[j], v2)
        i2 = jnp.where(upd, j, i2)
        s2 = jnp.where(upd, isc[j], s2)
    e1 = grp * EXPERTS_PER_GROUP + i1
    e2 = grp * EXPERTS_PER_GROUP + i2
    total = s1 + s2
    e_ref[0:1, :] = e1
    e_ref[1:2, :] = e2
    w_ref[0:1, :] = s1 / total
    w_ref[1:2, :] = s2 / total
    erow = lax.broadcasted_iota(jnp.int32, (N_EXPERTS, tm), 0)
    oh1 = jnp.where(erow == e1, 1.0, 0.0).astype(F32)
    oh2 = jnp.where(erow == e2, 1.0, 0.0).astype(F32)
    both = oh1 + oh2
    before = _dot(both.astype(BF16), su_ref[...]) + carry_ref[...][:, :1]
    rank_ref[0:1, :] = jnp.sum(oh1 * before, axis=0, keepdims=True).astype(jnp.int32)
    rank_ref[1:2, :] = jnp.sum(oh2 * before, axis=0, keepdims=True).astype(jnp.int32)
    carry = carry_ref[...] + jnp.sum(both, axis=1, keepdims=True)
    carry_ref[...] = carry
    cnt_ref[...] = carry.astype(jnp.int32)


def route_tokens(x, router_w, router_bias, tm=512):
    t, d = x.shape
    strict_upper = jnp.triu(jnp.ones((tm, tm), BF16), k=1)
    fixed = lambda i: (0, 0)
    tok = lambda i: (0, i)
    return pl.pallas_call(
        _router_body,
        grid=(t // tm,),
        in_specs=[pl.BlockSpec((tm, d), lambda i: (i, 0)),
                  pl.BlockSpec((N_EXPERTS, d), fixed),
                  pl.BlockSpec((N_EXPERTS, LANES), fixed),
                  pl.BlockSpec((tm, tm), fixed)],
        out_specs=[pl.BlockSpec((TOP_K, tm), tok), pl.BlockSpec((TOP_K, tm), tok),
                   pl.BlockSpec((TOP_K, tm), tok), pl.BlockSpec((N_EXPERTS, LANES), fixed)],
        out_shape=[jax.ShapeDtypeStruct((TOP_K, t), jnp.int32), jax.ShapeDtypeStruct((TOP_K, t), F32),
                   jax.ShapeDtypeStruct((TOP_K, t), jnp.int32),
                   jax.ShapeDtypeStruct((N_EXPERTS, LANES), jnp.int32)],
        scratch_shapes=[pltpu.VMEM((N_EXPERTS, LANES), F32)],
        compiler_params=_cparams(("arbitrary",)),
        name="route_tokens",
    )(x, router_w.T, jnp.broadcast_to(router_bias.reshape(N_EXPERTS, 1), (N_EXPERTS, LANES)), strict_upper)


def _row_copy(x_hbm, buf, sem, slot, row, tok):
    return pltpu.make_async_copy(x_hbm.at[pl.ds(tok, 1), :], buf.at[slot, pl.ds(row, 1), :], sem.at[slot])


def _expert_body(be_ref, rt_ref, x_hbm, w1_ref, w3_ref, w2_ref, y_ref, buf, sem):
    i = pl.program_id(0)
    n = pl.num_programs(0)

    def wait_slot(slot):
        def wait_one(r, carry):
            _row_copy(x_hbm, buf, sem, slot, r, 0).wait()
            return carry

        lax.fori_loop(0, EXPERT_BLOCK, wait_one, 0, unroll=8)

    @pl.when(i == 0)
    def _():
        def one(r, carry):
            _row_copy(x_hbm, buf, sem, 0, r, rt_ref[r]).start()
            return carry

        lax.fori_loop(0, EXPERT_BLOCK, one, 0, unroll=8)

    slot = i % 2
    wait_slot(slot)
    nxt = jnp.minimum(i + 1, n - 1) * EXPERT_BLOCK
    for r in range(EXPERT_BLOCK):
        _row_copy(x_hbm, buf, sem, 1 - slot, r, rt_ref[nxt + r]).start()
    x = buf[slot].astype(BF16)
    h1 = _dot(x, w1_ref[...])
    h3 = _dot(x, w3_ref[...])
    act = (h1 * _sigmoid(h1) * h3).astype(BF16)
    y_ref[...] = _dot(act, w2_ref[...])

    @pl.when(i == n - 1)
    def _():
        wait_slot(1 - slot)


def expert_ffn(block_expert, row_tok, x, w1, w3, w2):
    t, d = x.shape
    n_rows = row_tok.shape[0]
    n_blocks = n_rows // EXPERT_BLOCK
    grid_spec = pltpu.PrefetchScalarGridSpec(
        num_scalar_prefetch=2,
        grid=(n_blocks,),
        in_specs=[pl.BlockSpec(memory_space=pl.ANY),
                  pl.BlockSpec((None, d, D_EXPERT), lambda i, be, rt: (be[i], 0, 0)),
                  pl.BlockSpec((None, d, D_EXPERT), lambda i, be, rt: (be[i], 0, 0)),
                  pl.BlockSpec((None, D_EXPERT, d), lambda i, be, rt: (be[i], 0, 0))],
        out_specs=pl.BlockSpec((EXPERT_BLOCK, d), lambda i, be, rt: (i, 0)),
        scratch_shapes=[pltpu.VMEM((2, EXPERT_BLOCK, d), F32),
                        pltpu.SemaphoreType.DMA((2,))])
    return pl.pallas_call(
        _expert_body,
        grid_spec=grid_spec,
        out_shape=jax.ShapeDtypeStruct((n_rows, d), F32),
        compiler_params=_cparams(("arbitrary",)),
        name="expert_ffn",
    )(block_expert, row_tok, x, w1, w3, w2)


def _pair_copy(y_hbm, buf, sem, slot, k, row, src):
    return pltpu.make_async_copy(y_hbm.at[pl.ds(src, 1), :], buf.at[slot, k, pl.ds(row, 1), :], sem.at[slot])


def _combine_body(dest_ref, y_hbm, w_ref, x_ref, ln_g_ref, ln_b_ref, of_ref, ob_ref, buf, sem, *, n_tok):
    i = pl.program_id(0)
    n = pl.num_programs(0)
    tm = x_ref.shape[0]

    def issue(blk, slot):
        base = blk * tm

        def one(r, carry):
            for k in range(TOP_K):
                _pair_copy(y_hbm, buf, sem, slot, k, r, dest_ref[k * n_tok + base + r]).start()
            return carry

        lax.fori_loop(0, tm, one, 0, unroll=8)

    @pl.when(i == 0)
    def _():
        issue(0, 0)

    @pl.when(i + 1 < n)
    def _():
        issue(i + 1, (i + 1) % 2)

    slot = i % 2

    def wait_one(r, carry):
        for k in range(TOP_K):
            _pair_copy(y_hbm, buf, sem, slot, k, r, 0).wait()
        return carry

    lax.fori_loop(0, tm, wait_one, 0, unroll=8)
    w = w_ref[...]
    ffn = w[:, 0:1] * buf[slot, 0] + w[:, 1:2] * buf[slot, 1]
    y = _layer_norm(DEEPNORM_ALPHA * x_ref[...] + ffn, ln_g_ref[...], ln_b_ref[...])
    of_ref[...] = y
    ob_ref[...] = y.astype(BF16)


def combine_experts(dest, y_rows, w_pad, x, ln_g, ln_b, tm=256):
    t, d = x.shape
    row = lambda i, *_: (i, 0)
    fixed = lambda i, *_: (0, 0)
    grid_spec = pltpu.PrefetchScalarGridSpec(
        num_scalar_prefetch=1,
        grid=(t // tm,),
        in_specs=[pl.BlockSpec(memory_space=pl.ANY),
                  pl.BlockSpec((tm, LANES), row),
                  pl.BlockSpec((tm, d), row),
                  pl.BlockSpec((1, d), fixed),
                  pl.BlockSpec((1, d), fixed)],
        out_specs=[pl.BlockSpec((tm, d), row), pl.BlockSpec((tm, d), row)],
        scratch_shapes=[pltpu.VMEM((2, TOP_K, tm, d), F32),
                        pltpu.SemaphoreType.DMA((2,))])
    return pl.pallas_call(
        functools.partial(_combine_body, n_tok=t),
        grid_spec=grid_spec,
        out_shape=[jax.ShapeDtypeStruct((t, d), F32), jax.ShapeDtypeStruct((t, d), BF16)],
        compiler_params=_cparams(("arbitrary",)),
        name="combine_experts",
    )(dest, y_rows, w_pad, x, ln_g.reshape(1, d), ln_b.reshape(1, d))


def routed_experts(xf, router_w, router_bias, w1, w3, w2, ln_g, ln_b):
    t, d = xf.shape
    expert, weight, rank, counts = route_tokens(xf, router_w, router_bias)
    counts = counts[:, 0]
    padded = (counts + EXPERT_BLOCK - 1) // EXPERT_BLOCK * EXPERT_BLOCK
    padded_ends = jnp.cumsum(padded)
    padded_starts = padded_ends - padded
    start_of = jnp.zeros_like(expert)
    for e in range(N_EXPERTS):
        start_of = jnp.where(expert == e, padded_starts[e], start_of)
    dest = (start_of + rank).reshape(TOP_K * t)
    n_rows = TOP_K * t + N_EXPERTS * EXPERT_BLOCK
    tok = jnp.tile(jnp.arange(t, dtype=jnp.int32), TOP_K)
    row_tok = jnp.zeros((n_rows,), jnp.int32).at[dest].set(tok)
    block_start = jnp.arange(n_rows // EXPERT_BLOCK, dtype=jnp.int32) * EXPERT_BLOCK
    block_expert = jnp.minimum(jnp.searchsorted(padded_ends, block_start, side='right'),
                               N_EXPERTS - 1).astype(jnp.int32)
    y_rows = expert_ffn(block_expert, row_tok, xf, w1.astype(BF16), w3.astype(BF16), w2.astype(BF16))
    w_pad = jnp.pad(weight.T, ((0, 0), (0, LANES - TOP_K)))
    return combine_experts(dest, y_rows, w_pad, xf, ln_g, ln_b)


def _merge_route_body(yg_ref, ym_ref, ya_ref, g0_ref, g1_ref, g2_ref, gb_ref, pg_ref, pm_ref, pa_ref,
                      wo_ref, x_ref, ln_g_ref, ln_b_ref, rw_ref, rb_ref, su_ref, eye_ref,
                      xw_ref, grp_ref, rank_ref, cnt_ref, carry_ref):
    @pl.when(pl.program_id(0) == 0)
    def _():
        carry_ref[...] = jnp.zeros_like(carry_ref)

    gb = gb_ref[...]

    def branch(y_ref, p_ref, g_ref, idx):
        gate = _sigmoid(g_ref[...].astype(F32) + gb[:, idx * D_MODEL:(idx + 1) * D_MODEL])
        return gate * _dot(y_ref[...], p_ref[...])

    merged = branch(yg_ref, pg_ref, g0_ref, 0) + branch(ym_ref, pm_ref, g1_ref, 1) \
        + branch(ya_ref, pa_ref, g2_ref, 2)
    mix = _dot(merged.astype(BF16), wo_ref[...])
    y = _layer_norm(DEEPNORM_ALPHA * x_ref[...] + mix, ln_g_ref[...], ln_b_ref[...])
    xw_ref[:, 0:D_MODEL] = y

    y_hi = y.astype(BF16)
    y_lo = (y - y_hi.astype(F32)).astype(BF16)
    rw_hi, rw_lo = rw_ref[0], rw_ref[1]
    logits_tok = _dot(y_hi, rw_hi) + (_dot(y_lo, rw_hi) + _dot(y_hi, rw_lo))
    logits = logits_tok.T[0:N_EXPERTS, :]
    scores = _sigmoid(logits)
    biased = scores + rb_ref[...][:, :1]
    tm = logits.shape[1]
    brow = [biased[e:e + 1, :] for e in range(N_EXPERTS)]
    srow = [scores[e:e + 1, :] for e in range(N_EXPERTS)]
    best = _top2_sum(*brow[0:EXPERTS_PER_GROUP])
    grp = jnp.zeros((1, tm), jnp.int32)
    for g in range(1, N_GROUPS):
        gs = _top2_sum(*brow[g * EXPERTS_PER_GROUP:(g + 1) * EXPERTS_PER_GROUP])
        upd = gs > best
        best = jnp.where(upd, gs, best)
        grp = jnp.where(upd, g, grp)

    def in_group(rows, j):
        out = rows[j]
        for g in range(1, N_GROUPS):
            out = jnp.where(grp == g, rows[g * EXPERTS_PER_GROUP + j], out)
        return out

    ib = [in_group(brow, j) for j in range(EXPERTS_PER_GROUP)]
    isc = [in_group(srow, j) for j in range(EXPERTS_PER_GROUP)]
    v1, i1, s1 = ib[0], jnp.zeros((1, tm), jnp.int32), isc[0]
    for j in range(1, EXPERTS_PER_GROUP):
        upd = ib[j] > v1
        v1 = jnp.where(upd, ib[j], v1)
        i1 = jnp.where(upd, j, i1)
        s1 = jnp.where(upd, isc[j], s1)
    v2 = jnp.full((1, tm), -jnp.inf, F32)
    i2 = jnp.zeros((1, tm), jnp.int32)
    s2 = jnp.zeros((1, tm), F32)
    for j in range(EXPERTS_PER_GROUP):
        upd = jnp.logical_and(i1 != j, ib[j] > v2)
        v2 = jnp.where(upd, ib[j], v2)
        i2 = jnp.where(upd, j, i2)
        s2 = jnp.where(upd, isc[j], s2)
    total = s1 + s2
    wrow = [jnp.where(i1 == j, s1 / total, jnp.where(i2 == j, s2 / total, 0.0)) for j in range(EXPERTS_PER_GROUP)]
    row_id = lax.broadcasted_iota(jnp.int32, (8, tm), 0)
    w8 = jnp.zeros((8, tm), F32)
    for j in range(EXPERTS_PER_GROUP):
        w8 = jnp.where(row_id == j, wrow[j], w8)
    stack = jnp.concatenate([w8, jnp.zeros((LANES - 8, tm), F32)], axis=0)
    xw_ref[:, D_MODEL:D_MODEL + LANES] = _rows_to_cols(stack, eye_ref[...])
    grp_ref[...] = grp
    grow = lax.broadcasted_iota(jnp.int32, (N_EXPERTS, tm), 0)
    onehot = jnp.where(grow == grp, 1.0, 0.0).astype(F32)
    before = _dot(onehot.astype(BF16), su_ref[...]) + carry_ref[...][:, :1]
    rank_ref[...] = jnp.sum(onehot * before, axis=0, keepdims=True).astype(jnp.int32)
    carry = carry_ref[...] + jnp.sum(onehot, axis=1, keepdims=True)
    carry_ref[...] = carry
    cnt_ref[...] = carry.astype(jnp.int32)


def merge_and_route(y_gdn, y_mlstm, y_mla, h, gate_bias, p_gdn, p_mlstm, p_mla, w_out, x, ln_g, ln_b,
                    router_w, router_bias, tm=512):
    t, d = x.shape
    row = lambda i: (i, 0)
    fixed = lambda i: (0, 0)
    tok = lambda i: (0, i)
    ybs = pl.BlockSpec((tm, QK_WIDTH), row)
    pbs = pl.BlockSpec((QK_WIDTH, d), fixed)
    strict_upper = jnp.triu(jnp.ones((tm, tm), BF16), k=1)
    eye_b = jnp.eye(tm, dtype=BF16)
    rw = jnp.pad(router_w.astype(F32), ((0, 0), (0, LANES - N_EXPERTS)))
    rw_hi = rw.astype(BF16)
    rw_split = jnp.stack([rw_hi, (rw - rw_hi.astype(F32)).astype(BF16)])
    return pl.pallas_call(
        _merge_route_body,
        grid=(t // tm,),
        in_specs=[ybs, ybs, ybs,
                  pl.BlockSpec((tm, d), lambda i: (i, COL_GATE)),
                  pl.BlockSpec((tm, d), lambda i: (i, COL_GATE + 1)),
                  pl.BlockSpec((tm, d), lambda i: (i, COL_GATE + 2)),
                  pl.BlockSpec((1, 3 * d), fixed),
                  pbs, pbs, pbs,
                  pl.BlockSpec((d, d), fixed),
                  pl.BlockSpec((tm, d), row),
                  pl.BlockSpec((1, d), fixed),
                  pl.BlockSpec((1, d), fixed),
                  pl.BlockSpec((2, d, LANES), lambda i: (0, 0, 0)),
                  pl.BlockSpec((N_EXPERTS, LANES), fixed),
                  pl.BlockSpec((tm, tm), fixed),
                  pl.BlockSpec((tm, tm), fixed)],
        out_specs=[pl.BlockSpec((tm, d + LANES), row),
                   pl.BlockSpec((1, tm), tok), pl.BlockSpec((1, tm), tok),
                   pl.BlockSpec((N_EXPERTS, LANES), fixed)],
        out_shape=[jax.ShapeDtypeStruct((t, d + LANES), F32),
                   jax.ShapeDtypeStruct((1, t), jnp.int32), jax.ShapeDtypeStruct((1, t), jnp.int32),
                   jax.ShapeDtypeStruct((N_EXPERTS, LANES), jnp.int32)],
        scratch_shapes=[pltpu.VMEM((N_EXPERTS, LANES), F32)],
        compiler_params=_cparams(("arbitrary",)),
        name="merge_route",
    )(y_gdn, y_mlstm, y_mla, h, h, h, gate_bias.reshape(1, 3 * d), p_gdn, p_mlstm, p_mla, w_out, x,
      ln_g.reshape(1, d), ln_b.reshape(1, d), rw_split,
      jnp.broadcast_to(router_bias.reshape(N_EXPERTS, 1), (N_EXPERTS, LANES)), strict_upper, eye_b)


def _gather_copy(x_hbm, buf, sem, slot, row, tok):
    return pltpu.make_async_copy(x_hbm.at[pl.ds(tok, 1), :], buf.at[slot, pl.ds(row, 1), :], sem.at[slot])


def _scatter_copy(ybuf, y_hbm, sem, slot, row, dst):
    return pltpu.make_async_copy(ybuf.at[slot, pl.ds(row, 1), :], y_hbm.at[pl.ds(dst, 1), :], sem.at[slot])


def _group_ffn_body(bg_ref, src_ref, dst_ref, x_hbm, w1_ref, w3_ref, w2_ref, y_hbm,
                    buf, ybuf, gsem, ssem, *, spare_row):
    i = pl.program_id(0)
    n = pl.num_programs(0)
    slot = i % 2
    other = 1 - slot

    def wait_rows(make):
        def one(r, carry):
            make(r).wait()
            return carry

        lax.fori_loop(0, MOE_BLOCK, one, 0, unroll=8)

    @pl.when(i == 0)
    def _():
        def one(r, carry):
            _gather_copy(x_hbm, buf, gsem, 0, r, src_ref[r]).start()
            return carry

        lax.fori_loop(0, MOE_BLOCK, one, 0, unroll=8)
        ybuf[1] = jnp.zeros((MOE_BLOCK, D_MODEL), F32)

    wait_rows(lambda r: _gather_copy(x_hbm, buf, gsem, slot, r, 0))
    nxt = jnp.minimum(i + 1, n - 1) * MOE_BLOCK
    prv = jnp.maximum(i - 1, 0) * MOE_BLOCK
    for r in range(MOE_BLOCK):
        _gather_copy(x_hbm, buf, gsem, other, r, src_ref[nxt + r]).start()
    for r in range(MOE_BLOCK):
        dst = jnp.where(i > 0, dst_ref[prv + r], spare_row + r)
        _scatter_copy(ybuf, y_hbm, ssem, other, r, dst).start()
    xw = buf[slot]
    xb = xw[:, :D_MODEL].astype(BF16)
    acts = []
    for e in range(EXPERTS_PER_GROUP):
        cols = slice(e * D_EXPERT, (e + 1) * D_EXPERT)
        h1 = _dot(xb, w1_ref[:, cols])
        h3 = _dot(xb, w3_ref[:, cols])
        acts.append((h1 * _sigmoid(h1) * h3 * xw[:, D_MODEL + e:D_MODEL + e + 1]).astype(BF16))
    ybuf[slot] = _dot(jnp.concatenate(acts, axis=1), w2_ref[...])
    wait_rows(lambda r: _scatter_copy(ybuf, y_hbm, ssem, other, r, 0))

    @pl.when(i == n - 1)
    def _():
        wait_rows(lambda r: _gather_copy(x_hbm, buf, gsem, other, r, 0))

        def one(r, carry):
            _scatter_copy(ybuf, y_hbm, ssem, slot, r, dst_ref[i * MOE_BLOCK + r]).start()
            return carry

        lax.fori_loop(0, MOE_BLOCK, one, 0, unroll=8)
        wait_rows(lambda r: _scatter_copy(ybuf, y_hbm, ssem, slot, r, 0))


def group_ffn(block_group, row_src, row_dst, xw, w1g, w3g, w2g, n_out_rows, spare_row):
    d = D_MODEL
    n_rows = row_src.shape[0]
    n_blocks = n_rows // MOE_BLOCK
    wide = EXPERTS_PER_GROUP * D_EXPERT
    grid_spec = pltpu.PrefetchScalarGridSpec(
        num_scalar_prefetch=3,
        grid=(n_blocks,),
        in_specs=[pl.BlockSpec(memory_space=pl.ANY),
                  pl.BlockSpec((None, d, wide), lambda i, bg, rs, rd: (bg[i], 0, 0)),
                  pl.BlockSpec((None, d, wide), lambda i, bg, rs, rd: (bg[i], 0, 0)),
                  pl.BlockSpec((None, wide, d), lambda i, bg, rs, rd: (bg[i], 0, 0))],
        out_specs=pl.BlockSpec(memory_space=pl.ANY),
        scratch_shapes=[pltpu.VMEM((2, MOE_BLOCK, d + LANES), F32),
                        pltpu.VMEM((2, MOE_BLOCK, d), F32),
                        pltpu.SemaphoreType.DMA((2,)),
                        pltpu.SemaphoreType.DMA((2,))])
    return pl.pallas_call(
        functools.partial(_group_ffn_body, spare_row=spare_row),
        grid_spec=grid_spec,
        out_shape=jax.ShapeDtypeStruct((n_out_rows, d), F32),
        compiler_params=pltpu.CompilerParams(dimension_semantics=("arbitrary",),
                                             vmem_limit_bytes=MOE_VMEM_LIMIT),
        name="group_ffn",
    )(block_group, row_src, row_dst, xw, w1g, w3g, w2g)


def _residual_norm_body(x_ref, y_ref, ln_g_ref, ln_b_ref, of_ref, ob_ref):
    y = _layer_norm(DEEPNORM_ALPHA * x_ref[...] + y_ref[...], ln_g_ref[...], ln_b_ref[...])
    of_ref[...] = y
    ob_ref[...] = y.astype(BF16)


def residual_norm(xw, y_tok, ln_g, ln_b, tm=512):
    t = xw.shape[0]
    d = D_MODEL
    row = lambda i: (i, 0)
    fixed = lambda i: (0, 0)
    return pl.pallas_call(
        _residual_norm_body,
        grid=(t // tm,),
        in_specs=[pl.BlockSpec((tm, d), row), pl.BlockSpec((tm, d), row),
                  pl.BlockSpec((1, d), fixed), pl.BlockSpec((1, d), fixed)],
        out_specs=[pl.BlockSpec((tm, d), row), pl.BlockSpec((tm, d), row)],
        out_shape=[jax.ShapeDtypeStruct((t, d), F32), jax.ShapeDtypeStruct((t, d), BF16)],
        compiler_params=_cparams(("parallel",)),
        name="residual_norm",
    )(xw, y_tok, ln_g.reshape(1, d), ln_b.reshape(1, d))


def grouped_experts(xw, grp, rank, counts, w1, w3, w2, ln_g, ln_b):
    t = xw.shape[0]
    grp = grp.reshape(t)
    counts = counts[:N_GROUPS, 0]
    padded = (counts + MOE_BLOCK - 1) // MOE_BLOCK * MOE_BLOCK
    padded_ends = jnp.cumsum(padded)
    padded_starts = padded_ends - padded
    start_of = jnp.zeros_like(grp)
    for g in range(N_GROUPS):
        start_of = jnp.where(grp == g, padded_starts[g], start_of)
    dest = start_of + rank.reshape(t)
    n_rows = t + N_GROUPS * MOE_BLOCK
    slot_tok = jnp.full((n_rows,), -1, jnp.int32).at[dest].set(jnp.arange(t, dtype=jnp.int32))
    row_ids = jnp.arange(n_rows, dtype=jnp.int32)
    block_start = jnp.arange(n_rows // MOE_BLOCK, dtype=jnp.int32) * MOE_BLOCK
    block_group = jnp.minimum(jnp.sum(block_start[:, None] >= padded_ends[None, :], axis=1),
                              N_GROUPS - 1).astype(jnp.int32)
    pads = padded - counts
    pad_base = jnp.cumsum(pads) - pads - (padded_starts + counts)
    row_group = jnp.repeat(block_group, MOE_BLOCK)
    pad_off = jnp.zeros_like(row_ids)
    for g in range(N_GROUPS):
        pad_off = jnp.where(row_group == g, pad_base[g], pad_off)
    row_src = jnp.maximum(slot_tok, 0)
    row_dst = jnp.where(slot_tok < 0, t + row_ids + pad_off, slot_tok)
    d = D_MODEL
    wide = EXPERTS_PER_GROUP * D_EXPERT
    w1g = w1.reshape(N_GROUPS, EXPERTS_PER_GROUP, d, D_EXPERT).transpose(0, 2, 1, 3).reshape(N_GROUPS, d, wide)
    w3g = w3.reshape(N_GROUPS, EXPERTS_PER_GROUP, d, D_EXPERT).transpose(0, 2, 1, 3).reshape(N_GROUPS, d, wide)
    w2g = w2.reshape(N_GROUPS, wide, d)
    spare_row = n_rows
    y_tok = group_ffn(block_group, row_src, row_dst, xw, w1g.astype(BF16), w3g.astype(BF16), w2g.astype(BF16),
                      spare_row + MOE_BLOCK, spare_row)
    return residual_norm(xw, y_tok, ln_g, ln_b)


def _split_w_in(w_in):
    sizes = (QK_WIDTH, QK_WIDTH, QK_WIDTH, QK_WIDTH, HEADS, HEADS,
             QK_WIDTH, QK_WIDTH, QK_WIDTH, QK_WIDTH, HEADS, HEADS,
             MLA_Q_LORA, MLA_KV_LORA, MLA_ROPE, 3 * D_MODEL)
    parts, acc = [], 0
    for size in sizes:
        parts.append(w_in[:, acc:acc + size])
        acc += size
    return parts


def _arrange_w_in(w_in):
    (g_q, g_k, g_v, g_z, g_a, g_b, m_q, m_k, m_v, m_o, m_i, m_f, c_q, c_kv, k_rope, gates) = _split_w_in(w_in)
    d = w_in.shape[0]
    half = MLA_ROPE // 2
    pad64 = jnp.zeros((d, LANES - MLA_ROPE), w_in.dtype)
    rope_sw = jnp.concatenate([k_rope[:, half:], k_rope[:, :half]], axis=1)
    main = jnp.concatenate([c_q, c_kv, k_rope, pad64, rope_sw, pad64, jnp.zeros((d, LANES), w_in.dtype),
                            g_q, g_k, g_v, g_z, m_q, m_k, m_v, m_o, gates], axis=1)
    small = jnp.concatenate([g_a, g_b, m_i, m_f, jnp.zeros((d, LANES - 4 * HEADS), w_in.dtype)], axis=1)
    return main.astype(BF16), small.astype(BF16)


def _arrange_mla(w_uq, w_ukv):
    half = MLA_ROPE // 2
    wq = w_uq.reshape(MLA_Q_LORA, HEADS, HEAD_DIM + MLA_ROPE)
    nope, rope = wq[:, :, :HEAD_DIM], wq[:, :, HEAD_DIM:]
    pad = jnp.zeros((MLA_Q_LORA, HEADS, LANES - MLA_ROPE), w_uq.dtype)
    wq_a = jnp.concatenate([nope, rope, pad], axis=2).reshape(MLA_Q_LORA, HEADS * 2 * HEAD_DIM)
    rope_sw = jnp.concatenate([rope[:, :, half:], rope[:, :, :half]], axis=2)
    wq_b = jnp.concatenate([rope_sw, pad], axis=2).reshape(MLA_Q_LORA, HEADS * LANES)
    wkv = w_ukv.reshape(MLA_KV_LORA, HEADS, 2 * HEAD_DIM)
    wkv = jnp.concatenate([wkv[:, :, :HEAD_DIM].reshape(MLA_KV_LORA, QK_WIDTH),
                           wkv[:, :, HEAD_DIM:].reshape(MLA_KV_LORA, QK_WIDTH)], axis=1)
    return wq_a.astype(BF16), wq_b.astype(BF16), wkv.astype(BF16)


def kernel(x, positions, ln_in_g, ln_in_b, w_in, gdn_conv, gdn_a_log, gdn_dt_bias, gdn_norm, mlstm_gate_bias, mlstm_norm, mla_q_norm, mla_kv_norm, mla_w_uq, mla_w_ukv, w_br_gdn, w_br_mlstm, w_br_mla, gate_bias, w_out, ln1_g, ln1_b, router_w, router_bias, moe_w1, moe_w3, moe_w2, ln2_g, ln2_b):
    bsz, seq, d = x.shape
    t = bsz * seq
    xf, xb = layer_norm_entry(x.reshape(t, d), ln_in_g, ln_in_b)
    cc, ss = rope_tables(positions)
    fmasks, bmasks = _group_constants()
    for l in range(DEPTH):
        w_main, w_small = _arrange_w_in(w_in[l])
        h, small = in_projection(xb, w_main, w_small)
        small_t = small[:, :4 * HEADS].T
        y_gdn = gated_deltanet_heads(h, small_t, gdn_conv[l], gdn_a_log[l], gdn_dt_bias[l], gdn_norm[l],
                                     fmasks, bmasks, bsz, seq)
        y_mlstm = mlstm_heads(h, small_t, mlstm_gate_bias[l], mlstm_norm[l], fmasks, bmasks, bsz, seq)
        wq_a, wq_b, wkv = _arrange_mla(mla_w_uq[l], mla_w_ukv[l])
        q, kn, kr, v = mla_prepare(h, cc, ss, mla_q_norm[l], mla_kv_norm[l], wq_a, wq_b, wkv)
        y_mla = latent_attention(q, kn, kr, v, bsz, seq)
        xw, grp, rank, counts = merge_and_route(
            y_gdn, y_mlstm, y_mla, h, gate_bias[l], w_br_gdn[l].astype(BF16), w_br_mlstm[l].astype(BF16),
            w_br_mla[l].astype(BF16), w_out[l].astype(BF16), xf, ln1_g[l], ln1_b[l], router_w, router_bias)
        xf, xb = grouped_experts(xw, grp, rank, counts, moe_w1[l], moe_w3[l], moe_w2[l], ln2_g[l], ln2_b[l])
    return xf.reshape(bsz, seq, d)
```

```python
import functools

import jax
import jax.numpy as jnp
from jax import lax
from jax.experimental import pallas as pl
from jax.experimental.pallas import tpu as pltpu

F32 = jnp.float32
BF16 = jnp.bfloat16

D_MODEL = 1024
DEPTH = 2
HEADS = 4
HEAD_DIM = 128
CHUNK = 64
CONV_WIDTH = 4
GATE_CAP = 15.0
MLA_ROPE = 64
MLA_Q_LORA = 384
MLA_KV_LORA = 256
ROPE_THETA = 10000.0
N_EXPERTS = 16
N_GROUPS = 4
EXPERTS_PER_GROUP = 4
TOP_K = 2
D_EXPERT = 512
EXPERT_BLOCK = 256
LN_EPS = 1e-5
RMS_EPS = 1e-6
DEEPNORM_ALPHA = (2 * DEPTH) ** 0.25
LOG2_E = 1.4426950408889634

LANES = 128
QK_WIDTH = HEADS * HEAD_DIM
H_WIDTH = 8192
COL_MLA = 0
COL_GDN = 8
COL_MLSTM = 24
COL_GATE = 5
SEQ_BLOCK = 512
GROUP = 256
VMEM_LIMIT = 48 * 1024 * 1024
MOE_BLOCK = 512
MOE_VMEM_LIMIT = 56 * 1024 * 1024


def _cparams(sem):
    return pltpu.CompilerParams(dimension_semantics=sem, vmem_limit_bytes=VMEM_LIMIT)


def _sigmoid(x):
    return 1.0 / (1.0 + jnp.exp(-x))


def _layer_norm(x, g, b):
    mu = jnp.mean(x, axis=-1, keepdims=True)
    xc = x - mu
    var = jnp.mean(xc * xc, axis=-1, keepdims=True)
    return xc * lax.rsqrt(var + LN_EPS) * g + b


def _dot(a, b):
    return jnp.dot(a, b, preferred_element_type=F32)


def _dot_nt(a, b, precision=None):
    return lax.dot_general(a, b, (((1,), (1,)), ((), ())), preferred_element_type=F32,
                           precision=precision)


def _dot_tn(a, b, precision=None):
    return lax.dot_general(a, b, (((0,), (0,)), ((), ())), preferred_element_type=F32,
                           precision=precision)


def _ln_body(x_ref, g_ref, b_ref, of_ref, ob_ref):
    y = _layer_norm(x_ref[...], g_ref[...], b_ref[...])
    of_ref[...] = y
    ob_ref[...] = y.astype(BF16)


def layer_norm_entry(x, g, b, tm=512):
    t, d = x.shape
    return pl.pallas_call(
        _ln_body,
        grid=(t // tm,),
        in_specs=[pl.BlockSpec((tm, d), lambda i: (i, 0)),
                  pl.BlockSpec((1, d), lambda i: (0, 0)),
                  pl.BlockSpec((1, d), lambda i: (0, 0))],
        out_specs=[pl.BlockSpec((tm, d), lambda i: (i, 0)),
                   pl.BlockSpec((tm, d), lambda i: (i, 0))],
        out_shape=[jax.ShapeDtypeStruct((t, d), F32), jax.ShapeDtypeStruct((t, d), BF16)],
        compiler_params=_cparams(("parallel",)),
        name="ln_entry",
    )(x, g.reshape(1, d), b.reshape(1, d))


def _inproj_body(x_ref, w_ref, ws_ref, h_ref, hs_ref):
    x = x_ref[...]
    h_ref[...] = _dot(x, w_ref[...]).astype(BF16)

    @pl.when(pl.program_id(1) == 0)
    def _():
        hs_ref[...] = _dot(x, ws_ref[...])


def in_projection(xb, w_main, w_small, tm=2048, tn=512):
    t, d = xb.shape
    n = w_main.shape[1]
    return pl.pallas_call(
        _inproj_body,
        grid=(t // tm, n // tn),
        in_specs=[pl.BlockSpec((tm, d), lambda i, j: (i, 0)),
                  pl.BlockSpec((d, tn), lambda i, j: (0, j)),
                  pl.BlockSpec((d, LANES), lambda i, j: (0, 0))],
        out_specs=[pl.BlockSpec((tm, tn), lambda i, j: (i, j)),
                   pl.BlockSpec((tm, LANES), lambda i, j: (i, 0))],
        out_shape=[jax.ShapeDtypeStruct((t, n), BF16), jax.ShapeDtypeStruct((t, LANES), F32)],
        compiler_params=_cparams(("parallel", "arbitrary")),
        name="in_proj",
    )(xb, w_main, w_small)


def _lane_pick(x, lane):
    idx = lax.broadcasted_iota(jnp.int32, x.shape, 1)
    return jnp.sum(jnp.where(idx == lane, x, 0.0), axis=1, keepdims=True)


def _softplus(x):
    return jnp.maximum(x, 0.0) + jnp.log1p(jnp.exp(-jnp.abs(x)))


def _group_masks():
    r = lax.broadcasted_iota(jnp.int32, (GROUP, GROUP), 0)
    c = lax.broadcasted_iota(jnp.int32, (GROUP, GROUP), 1)
    same = (r // CHUNK) == (c // CHUNK)
    causal = jnp.logical_and(same, r >= c)
    strict = jnp.logical_and(same, r > c)
    upper = jnp.logical_and(same, r <= c)
    return causal, strict, upper, r == c


def _split_bf16(x):
    hi = x.astype(BF16)
    return hi, (x - hi.astype(F32)).astype(BF16)


def _group_cumsum_col(col, low_b):
    hi, lo = _split_bf16(jnp.broadcast_to(col, (GROUP, LANES)))
    return (_dot(low_b, hi) + _dot(low_b, lo))[:, :1]


def _group_cumsum_row(row, up_b):
    hi, lo = _split_bf16(jnp.broadcast_to(row, (16, GROUP)))
    return (_dot(hi, up_b) + _dot(lo, up_b))[0:1, :]


def _rms_norm(x, g):
    return x * lax.rsqrt(jnp.mean(x * x, axis=-1, keepdims=True) + RMS_EPS) * g


def _gdn_body(alog_ref, dtb_ref, q_ref, k_ref, v_ref, z_ref, cq_ref, ck_ref, cv_ref,
              sm_ref, smt_ref, ng_ref, o_ref, state_ref, eq_ref, ek_ref, ev_ref):
    hh = pl.program_id(1)
    lb = q_ref.shape[0]

    @pl.when(pl.program_id(2) == 0)
    def _():
        state_ref[...] = jnp.zeros_like(state_ref)
        for e_ref in (eq_ref, ek_ref, ev_ref):
            e_ref[0:8, :] = jnp.zeros((8, HEAD_DIM), F32)

    def conv_silu(x_ref, w_ref, e_ref):
        e_ref[8:, :] = x_ref[...].astype(F32)
        w = w_ref[...]
        y = w[0:1, :] * e_ref[pl.ds(8 - CONV_WIDTH + 1, lb), :]
        for j in range(1, CONV_WIDTH):
            y = y + w[j:j + 1, :] * e_ref[pl.ds(8 - CONV_WIDTH + 1 + j, lb), :]
        e_ref[0:8, :] = e_ref[lb:lb + 8, :]
        return y * _sigmoid(y)

    q = conv_silu(q_ref, cq_ref, eq_ref)
    k = conv_silu(k_ref, ck_ref, ek_ref)
    v = conv_silu(v_ref, cv_ref, ev_ref)
    q = q * lax.rsqrt(jnp.sum(q * q, axis=-1, keepdims=True) + RMS_EPS) * (HEAD_DIM ** -0.5)
    k = k * lax.rsqrt(jnp.sum(k * k, axis=-1, keepdims=True) + RMS_EPS)

    neg_a = -jnp.exp(jnp.full((1, 1), alog_ref[hh], F32))
    dtb = dtb_ref[hh]
    sm = sm_ref[...]
    g_col = neg_a * _softplus(_lane_pick(sm, hh) + dtb)
    beta_col = _sigmoid(_lane_pick(sm, HEADS + hh))
    g_row = neg_a * _softplus(smt_ref[pl.ds(hh, 1), :] + dtb)

    causal, strict, upper, diag = _group_masks()
    low_b = jnp.where(causal, 1.0, 0.0).astype(BF16)
    up_b = jnp.where(upper, 1.0, 0.0).astype(BF16)
    eye = jnp.where(diag, 1.0, 0.0).astype(F32)
    state = state_ref[...]
    z = z_ref[...].astype(F32)
    ng = ng_ref[...]
    for gi in range(lb // GROUP):
        gs = slice(gi * GROUP, (gi + 1) * GROUP)
        qg, kg, bg = q[gs], k[gs], beta_col[gs]
        gc_col = _group_cumsum_col(g_col[gs], low_b)
        gc_row = _group_cumsum_row(g_row[:, gs], up_b)
        decay = jnp.exp(jnp.where(causal, gc_col - gc_row, -jnp.inf))
        kg_b = kg.astype(BF16)
        kb = kg * bg
        a_mat = jnp.where(strict, _dot_nt(kb.astype(BF16), kg_b) * decay, 0.0)
        pw = -a_mat
        t_mat = eye + pw
        for _ in range(5):
            pw_b = pw.astype(BF16)
            pw = _dot(pw_b, pw_b)
            t_mat = t_mat + _dot(t_mat.astype(BF16), pw.astype(BF16))
        e_gc = jnp.exp(gc_col)
        rhs = jnp.concatenate([(v[gs] * bg).astype(BF16), (kb * e_gc).astype(BF16)], axis=1)
        uw = _dot(t_mat.astype(BF16), rhs).astype(BF16)
        qk = (_dot_nt(qg.astype(BF16), kg_b) * decay).astype(BF16)
        qk_uw = _dot(qk, uw)
        o_intra = qk_uw[:, :HEAD_DIM]
        q_eff = qg * e_gc - qk_uw[:, HEAD_DIM:]
        for c in range(GROUP // CHUNK):
            sl = slice(c * CHUNK, (c + 1) * CHUNK)
            gc_last = gc_col[(c + 1) * CHUNK - 1:(c + 1) * CHUNK, :]
            k_tail = (kg[sl] * jnp.exp(gc_last - gc_col[sl])).astype(BF16)
            kt_uw = _dot_tn(k_tail, uw[sl])
            lhs = jnp.concatenate([q_eff[sl], kt_uw[:, HEAD_DIM:]], axis=0).astype(BF16)
            res = _dot(lhs, state.astype(BF16))
            out = res[:CHUNK] + o_intra[sl]
            state = state * jnp.exp(gc_last) - res[CHUNK:] + kt_uw[:, :HEAD_DIM]
            rows = slice(gi * GROUP + c * CHUNK, gi * GROUP + (c + 1) * CHUNK)
            zc = z[rows]
            o_ref[rows, :] = (_rms_norm(out, ng) * (zc * _sigmoid(zc))).astype(o_ref.dtype)
    state_ref[...] = state


def gated_deltanet(h, small, small_t, conv_w, a_log, dt_bias, norm_g, bsz, seq):
    t = h.shape[0]
    lb = SEQ_BLOCK
    nb = seq // lb

    def col(off):
        return pl.BlockSpec((lb, HEAD_DIM), lambda b, hh, s, *_: (b * nb + s, COL_GDN + off + hh))

    def conv(off):
        return pl.BlockSpec((CONV_WIDTH, HEAD_DIM), lambda b, hh, s, *_: (0, off + hh))

    grid_spec = pltpu.PrefetchScalarGridSpec(
        num_scalar_prefetch=2,
        grid=(bsz, HEADS, nb),
        in_specs=[col(0), col(HEADS), col(2 * HEADS), col(3 * HEADS),
                  conv(0), conv(HEADS), conv(2 * HEADS),
                  pl.BlockSpec((lb, LANES), lambda b, hh, s, *_: (b * nb + s, 0)),
                  pl.BlockSpec((4 * HEADS, lb), lambda b, hh, s, *_: (0, b * nb + s)),
                  pl.BlockSpec((1, HEAD_DIM), lambda b, hh, s, *_: (0, 0))],
        out_specs=pl.BlockSpec((lb, HEAD_DIM), lambda b, hh, s, *_: (b * nb + s, hh)),
        scratch_shapes=[pltpu.VMEM((HEAD_DIM, HEAD_DIM), F32),
                        pltpu.VMEM((lb + 8, HEAD_DIM), F32),
                        pltpu.VMEM((lb + 8, HEAD_DIM), F32),
                        pltpu.VMEM((lb + 8, HEAD_DIM), F32)])
    return pl.pallas_call(
        _gdn_body,
        grid_spec=grid_spec,
        out_shape=jax.ShapeDtypeStruct((t, QK_WIDTH), BF16),
        compiler_params=_cparams(("parallel", "parallel", "arbitrary")),
        name="gated_deltanet",
    )(a_log, dt_bias, h, h, h, h, conv_w, conv_w, conv_w, small, small_t, norm_g.reshape(1, HEAD_DIM))


def _soft_cap(x):
    return GATE_CAP * jnp.tanh(x / GATE_CAP)


def _log_sigmoid(x):
    return jnp.minimum(x, 0.0) - jnp.log1p(jnp.exp(-jnp.abs(x)))


def _mlstm_body(gb_ref, q_ref, k_ref, v_ref, o_ref, sm_ref, smt_ref, ng_ref,
                y_ref, c_ref, n_ref, m_ref):
    hh = pl.program_id(1)
    lb = q_ref.shape[0]
    n_groups = lb // GROUP
    per_group = GROUP // CHUNK

    @pl.when(pl.program_id(2) == 0)
    def _():
        c_ref[...] = jnp.zeros_like(c_ref)
        n_ref[...] = jnp.zeros_like(n_ref)
        m_ref[...] = jnp.zeros_like(m_ref)

    q = q_ref[...]
    k = k_ref[...].astype(F32) * (HEAD_DIM ** -0.5)
    v = v_ref[...]
    gb_i = gb_ref[hh]
    gb_f = gb_ref[HEADS + hh]
    sm = sm_ref[...]
    i_col = _soft_cap(_lane_pick(sm, 2 * HEADS + hh) + gb_i)
    f_col = _log_sigmoid(_soft_cap(_lane_pick(sm, 3 * HEADS + hh) + gb_f))
    i_row = _soft_cap(smt_ref[pl.ds(2 * HEADS + hh, 1), :] + gb_i)
    f_row = _log_sigmoid(_soft_cap(smt_ref[pl.ds(3 * HEADS + hh, 1), :] + gb_f))

    causal, _, upper, _ = _group_masks()
    low_b = jnp.where(causal, 1.0, 0.0).astype(BF16)
    up_b = jnp.where(upper, 1.0, 0.0).astype(BF16)

    bc_cols, lkws, b_lasts, lkw_maxes = [], [], [], []
    for gi in range(n_groups):
        gs = slice(gi * GROUP, (gi + 1) * GROUP)
        bc_col = _group_cumsum_col(f_col[gs], low_b)
        bc_cols.append(bc_col)
        for c in range(per_group):
            sl = slice(c * CHUNK, (c + 1) * CHUNK)
            b_last = bc_col[(c + 1) * CHUNK - 1:(c + 1) * CHUNK, :]
            lkw = b_last - bc_col[sl] + i_col[gi * GROUP + c * CHUNK:gi * GROUP + (c + 1) * CHUNK]
            b_lasts.append(b_last)
            lkws.append(lkw)
            lkw_maxes.append(jnp.max(lkw, axis=0, keepdims=True))
    m_st = m_ref[...][:, :1]
    m_prev, m_next, carry_decay = [], [], []
    for ci in range(n_groups * per_group):
        m_new = jnp.maximum(b_lasts[ci] + m_st, lkw_maxes[ci])
        m_prev.append(m_st)
        m_next.append(m_new)
        carry_decay.append(jnp.exp(b_lasts[ci] + m_st - m_new))
        m_st = m_new

    c_st = c_ref[...]
    n_st = n_ref[...]
    o_pre = o_ref[...].astype(F32)
    ng = ng_ref[...]
    for gi in range(n_groups):
        gs = slice(gi * GROUP, (gi + 1) * GROUP)
        qg, vg = q[gs], v[gs]
        kg_b = k[gs].astype(BF16)
        bc_col = bc_cols[gi]
        bc_row = _group_cumsum_row(f_row[:, gs], up_b)
        log_d = jnp.where(causal, bc_col - bc_row + i_row[:, gs], -jnp.inf)
        m_prev_col = jnp.concatenate(
            [jnp.broadcast_to(m_prev[gi * per_group + c], (CHUNK, 1)) for c in range(per_group)], axis=0)
        log_inter = bc_col + m_prev_col
        m_t = jnp.maximum(log_inter, jnp.max(log_d, axis=-1, keepdims=True))
        w_inter = jnp.exp(log_inter - m_t)
        s = _dot_nt(qg, kg_b) * jnp.exp(log_d - m_t)
        s_v = _dot(s.astype(BF16), vg)
        s_sum = jnp.sum(s, axis=-1, keepdims=True)
        floor = jnp.exp(-m_t)
        for c in range(per_group):
            ci = gi * per_group + c
            sl = slice(c * CHUNK, (c + 1) * CHUNK)
            rows = slice(gi * GROUP + c * CHUNK, gi * GROUP + (c + 1) * CHUNK)
            qc = qg[sl]
            num = w_inter[sl] * _dot(qc, c_st.astype(BF16)) + s_v[sl]
            den = w_inter[sl] * jnp.sum(qc.astype(F32) * n_st, axis=-1, keepdims=True) + s_sum[sl]
            hid = num / jnp.maximum(jnp.abs(den), floor[sl])
            kw = k[rows] * jnp.exp(lkws[ci] - m_next[ci])
            c_st = carry_decay[ci] * c_st + _dot_tn(kw.astype(BF16), vg[sl])
            n_st = carry_decay[ci] * n_st + jnp.sum(kw, axis=0, keepdims=True)
            y_ref[rows, :] = (_sigmoid(o_pre[rows]) * _rms_norm(hid, ng)).astype(y_ref.dtype)
    c_ref[...] = c_st
    n_ref[...] = n_st
    m_ref[...] = jnp.broadcast_to(m_st, m_ref.shape)


def mlstm(h, small, small_t, gate_bias, norm_g, bsz, seq):
    t = h.shape[0]
    lb = SEQ_BLOCK
    nb = seq // lb

    def col(off):
        return pl.BlockSpec((lb, HEAD_DIM), lambda b, hh, s, *_: (b * nb + s, COL_MLSTM + off + hh))

    grid_spec = pltpu.PrefetchScalarGridSpec(
        num_scalar_prefetch=1,
        grid=(bsz, HEADS, nb),
        in_specs=[col(0), col(HEADS), col(2 * HEADS), col(3 * HEADS),
                  pl.BlockSpec((lb, LANES), lambda b, hh, s, *_: (b * nb + s, 0)),
                  pl.BlockSpec((4 * HEADS, lb), lambda b, hh, s, *_: (0, b * nb + s)),
                  pl.BlockSpec((1, HEAD_DIM), lambda b, hh, s, *_: (0, hh))],
        out_specs=pl.BlockSpec((lb, HEAD_DIM), lambda b, hh, s, *_: (b * nb + s, hh)),
        scratch_shapes=[pltpu.VMEM((HEAD_DIM, HEAD_DIM), F32),
                        pltpu.VMEM((1, HEAD_DIM), F32),
                        pltpu.VMEM((1, LANES), F32)])
    return pl.pallas_call(
        _mlstm_body,
        grid_spec=grid_spec,
        out_shape=jax.ShapeDtypeStruct((t, QK_WIDTH), BF16),
        compiler_params=_cparams(("parallel", "parallel", "arbitrary")),
        name="mlstm",
    )(gate_bias, h, h, h, h, small, small_t, norm_g.reshape(1, QK_WIDTH))


def _group_constants():
    r = jnp.arange(GROUP, dtype=jnp.int32)[:, None]
    c = jnp.arange(GROUP, dtype=jnp.int32)[None, :]
    same = (r // CHUNK) == (c // CHUNK)
    neg = jnp.where(same & (r >= c), 0.0, -jnp.inf).astype(F32)
    strict = (same & (r > c)).astype(F32)
    eye = (r == c).astype(F32)
    upper = (same & (r <= c)).astype(BF16)
    last = (r == (c // CHUNK) * CHUNK + CHUNK - 1).astype(BF16)
    return jnp.stack([neg, strict, eye]), jnp.stack([upper, last, eye.astype(BF16)])


def _split3(x):
    hi = x.astype(BF16)
    r1 = x - hi.astype(F32)
    mid = r1.astype(BF16)
    return hi, mid, (r1 - mid.astype(F32)).astype(BF16)


def _rows_times(rows8, mat_b, terms):
    rows = jnp.concatenate([rows8, jnp.zeros_like(rows8)], axis=0)
    parts = _split3(rows)[:terms]
    out = _dot(parts[0], mat_b)
    for p in parts[1:]:
        out = out + _dot(p, mat_b)
    return out[0:8]


def _rows_to_cols(stack, eye_b):
    parts = _split3(stack)
    out = _dot_nt(eye_b, parts[0])
    for p in parts[1:]:
        out = out + _dot_nt(eye_b, p)
    return out


def _conv_silu(x_ref, w, e_ref):
    lb = x_ref.shape[0]
    e_ref[8:, :] = x_ref[...].astype(F32)
    y = w[0:1, :] * e_ref[pl.ds(8 - CONV_WIDTH + 1, lb), :]
    for j in range(1, CONV_WIDTH):
        y = y + w[j:j + 1, :] * e_ref[pl.ds(8 - CONV_WIDTH + 1 + j, lb), :]
    e_ref[0:8, :] = e_ref[lb:lb + 8, :]
    return y * _sigmoid(y)


def _gdn_heads_body(q_ref, k_ref, v_ref, z_ref, cw_ref, smt_ref, gp_ref, ng_ref, fm_ref, bm_ref,
                    o_ref, state_ref, eq_ref, ek_ref, ev_ref):
    lb = q_ref.shape[0]

    @pl.when(pl.program_id(1) == 0)
    def _():
        state_ref[...] = jnp.zeros_like(state_ref)
        for e_ref in (eq_ref, ek_ref, ev_ref):
            e_ref[0:8, :] = jnp.zeros((8, QK_WIDTH), F32)

    cw = cw_ref[...]
    q_all = _conv_silu(q_ref, cw[:, 0:QK_WIDTH], eq_ref)
    k_all = _conv_silu(k_ref, cw[:, QK_WIDTH:2 * QK_WIDTH], ek_ref)
    v_all = _conv_silu(v_ref, cw[:, 2 * QK_WIDTH:3 * QK_WIDTH], ev_ref)
    z_all = z_ref[...].astype(F32)
    ng = ng_ref[...]
    neg, strict01, eye = fm_ref[0], fm_ref[1], fm_ref[2]
    up_b, last_b, eye_b = bm_ref[0], bm_ref[1], bm_ref[2]

    gp = gp_ref[...]
    neg_a8 = -jnp.exp(gp[0:8, 0:1])
    dtb8 = gp[8:16, 0:1]
    g8 = neg_a8 * _softplus(smt_ref[0:8, :] + dtb8)
    beta8 = _sigmoid(smt_ref[HEADS:HEADS + 8, :])
    gc8 = _rows_times(g8, up_b, 2)
    gl8 = _rows_times(gc8, last_b, 3)
    egc8 = jnp.exp(gc8)
    tail8 = jnp.exp(gl8 - gc8)
    elast8 = jnp.exp(gl8)
    stack = jnp.concatenate([gc8, beta8, egc8, tail8, beta8 * egc8,
                             jnp.zeros((LANES - 40, lb), F32)], axis=0)
    cols = _rows_to_cols(stack, eye_b)

    hds = range(HEADS)
    lanes = [slice(hd * HEAD_DIM, (hd + 1) * HEAD_DIM) for hd in hds]
    col = lambda j, hd: cols[:, 8 * j + hd:8 * j + hd + 1]
    qg = [q_all[:, lanes[hd]] for hd in hds]
    kg = [k_all[:, lanes[hd]] for hd in hds]
    qg = [x * lax.rsqrt(jnp.sum(x * x, axis=-1, keepdims=True) + RMS_EPS) * (HEAD_DIM ** -0.5) for x in qg]
    kg = [x * lax.rsqrt(jnp.sum(x * x, axis=-1, keepdims=True) + RMS_EPS) for x in kg]
    kg_b = [x.astype(BF16) for x in kg]
    decay = [jnp.exp(col(0, hd) - gc8[hd:hd + 1, :] + neg) for hd in hds]
    a_mat = [_dot_nt((kg[hd] * col(1, hd)).astype(BF16), kg_b[hd]) * decay[hd] * strict01 for hd in hds]
    pw = [-a for a in a_mat]
    t_mat = [eye + p for p in pw]
    for _ in range(5):
        pw_b = [p.astype(BF16) for p in pw]
        pw = [_dot(p, p) for p in pw_b]
        t_mat = [t_mat[hd] + _dot(t_mat[hd].astype(BF16), pw[hd].astype(BF16)) for hd in hds]
    rhs = [jnp.concatenate([(v_all[:, lanes[hd]] * col(1, hd)).astype(BF16),
                            (kg[hd] * col(4, hd)).astype(BF16)], axis=1) for hd in hds]
    uw = [_dot(t_mat[hd].astype(BF16), rhs[hd]).astype(BF16) for hd in hds]
    qk = [(_dot_nt(qg[hd].astype(BF16), kg_b[hd]) * decay[hd]).astype(BF16) for hd in hds]
    qk_uw = [_dot(qk[hd], uw[hd]) for hd in hds]
    q_eff = [qg[hd] * col(2, hd) - qk_uw[hd][:, HEAD_DIM:] for hd in hds]
    k_tail = [(kg[hd] * col(3, hd)).astype(BF16) for hd in hds]
    state = [state_ref[hd] for hd in hds]
    for c in range(lb // CHUNK):
        sl = slice(c * CHUNK, (c + 1) * CHUNK)
        kt_uw = [_dot_tn(k_tail[hd][sl], uw[hd][sl]) for hd in hds]
        lhs = [jnp.concatenate([q_eff[hd][sl], kt_uw[hd][:, HEAD_DIM:]], axis=0).astype(BF16) for hd in hds]
        res = [_dot(lhs[hd], state[hd].astype(BF16)) for hd in hds]
        state = [state[hd] * elast8[hd:hd + 1, c * CHUNK:c * CHUNK + 1] - res[hd][CHUNK:]
                 + kt_uw[hd][:, :HEAD_DIM] for hd in hds]
        for hd in hds:
            out = res[hd][:CHUNK] + qk_uw[hd][sl, :HEAD_DIM]
            zc = z_all[sl, lanes[hd]]
            o_ref[sl, lanes[hd]] = (_rms_norm(out, ng) * (zc * _sigmoid(zc))).astype(o_ref.dtype)
    for hd in hds:
        state_ref[hd] = state[hd]


def _gate_params(first, second):
    out = jnp.zeros((16, LANES), F32)
    out = out.at[0:HEADS, :].set(jnp.broadcast_to(first.astype(F32)[:, None], (HEADS, LANES)))
    return out.at[8:8 + HEADS, :].set(jnp.broadcast_to(second.astype(F32)[:, None], (HEADS, LANES)))


def gated_deltanet_heads(h, small_t, conv_w, a_log, dt_bias, norm_g, fmasks, bmasks, bsz, seq):
    t = h.shape[0]
    lb = GROUP
    nb = seq // lb
    wide = QK_WIDTH
    first = COL_GDN * LANES // wide

    def col(j):
        return pl.BlockSpec((lb, wide), lambda b, s: (b * nb + s, first + j))

    fixed2 = lambda b, s: (0, 0)
    fixed3 = lambda b, s: (0, 0, 0)
    return pl.pallas_call(
        _gdn_heads_body,
        grid=(bsz, nb),
        in_specs=[col(0), col(1), col(2), col(3),
                  pl.BlockSpec((CONV_WIDTH, 3 * wide), fixed2),
                  pl.BlockSpec((4 * HEADS, lb), lambda b, s: (0, b * nb + s)),
                  pl.BlockSpec((16, LANES), fixed2),
                  pl.BlockSpec((1, HEAD_DIM), fixed2),
                  pl.BlockSpec((3, GROUP, GROUP), fixed3),
                  pl.BlockSpec((3, GROUP, GROUP), fixed3)],
        out_specs=pl.BlockSpec((lb, wide), lambda b, s: (b * nb + s, 0)),
        out_shape=jax.ShapeDtypeStruct((t, wide), BF16),
        scratch_shapes=[pltpu.VMEM((HEADS, HEAD_DIM, HEAD_DIM), F32),
                        pltpu.VMEM((lb + 8, wide), F32),
                        pltpu.VMEM((lb + 8, wide), F32),
                        pltpu.VMEM((lb + 8, wide), F32)],
        compiler_params=_cparams(("parallel", "arbitrary")),
        name="gated_deltanet",
    )(h, h, h, h, conv_w, small_t, _gate_params(a_log, dt_bias), norm_g.reshape(1, HEAD_DIM), fmasks, bmasks)


def _mlstm_heads_body(q_ref, k_ref, v_ref, o_ref, smt_ref, gp_ref, ng_ref, fm_ref, bm_ref,
                      y_ref, c_ref, n_ref, m_ref):
    lb = q_ref.shape[0]
    n_chunks = lb // CHUNK

    @pl.when(pl.program_id(1) == 0)
    def _():
        c_ref[...] = jnp.zeros_like(c_ref)
        n_ref[...] = jnp.zeros_like(n_ref)
        m_ref[...] = jnp.zeros_like(m_ref)

    hds = range(HEADS)
    lanes = [slice(hd * HEAD_DIM, (hd + 1) * HEAD_DIM) for hd in hds]
    q_all = q_ref[...]
    k_all = k_ref[...].astype(F32) * (HEAD_DIM ** -0.5)
    qg = [q_all[:, lanes[hd]] for hd in hds]
    kg = [k_all[:, lanes[hd]] for hd in hds]
    qk = [_dot_nt(qg[hd], kg[hd].astype(BF16)) for hd in hds]

    neg = fm_ref[0]
    up_b, last_b, eye_b = bm_ref[0], bm_ref[1], bm_ref[2]
    capped = _soft_cap(smt_ref[2 * HEADS:4 * HEADS, :] + gp_ref[...][0:8, 0:1])
    i8 = capped
    f8 = pltpu.roll(_log_sigmoid(capped), HEADS, axis=0)
    bc8 = _rows_times(f8, up_b, 2)
    bl8 = _rows_times(bc8, last_b, 3)
    lkw8 = bl8 - bc8 + i8
    chunk_id = lax.broadcasted_iota(jnp.int32, (8, lb), 1) // CHUNK
    m_st = m_ref[...][:, 0:1]
    m_prev_row = jnp.zeros((8, lb), F32)
    m_next_row = jnp.zeros((8, lb), F32)
    carry_decay = []
    for c in range(n_chunks):
        in_c = chunk_id == c
        b_last = bl8[:, c * CHUNK:c * CHUNK + 1]
        m_new = jnp.maximum(b_last + m_st, jnp.max(jnp.where(in_c, lkw8, -jnp.inf), axis=1, keepdims=True))
        carry_decay.append(jnp.exp(b_last + m_st - m_new))
        m_prev_row = jnp.where(in_c, m_st, m_prev_row)
        m_next_row = jnp.where(in_c, m_new, m_next_row)
        m_st = m_new
    m_ref[...] = jnp.broadcast_to(m_st, m_ref.shape)
    stack = jnp.concatenate([bc8, bc8 + m_prev_row, jnp.exp(lkw8 - m_next_row),
                             jnp.zeros((LANES - 24, lb), F32)], axis=0)
    cols = _rows_to_cols(stack, eye_b)

    v_all = v_ref[...]
    o_all = o_ref[...].astype(F32)
    ng_all = ng_ref[...]
    n_all = n_ref[...]
    col = lambda j, hd: cols[:, 8 * j + hd:8 * j + hd + 1]
    vg = [v_all[:, lanes[hd]] for hd in hds]
    log_d = [col(0, hd) - bc8[hd:hd + 1, :] + i8[hd:hd + 1, :] + neg for hd in hds]
    m_t = [jnp.maximum(col(1, hd), jnp.max(log_d[hd], axis=-1, keepdims=True)) for hd in hds]
    w_inter = [jnp.exp(col(1, hd) - m_t[hd]) for hd in hds]
    s = [qk[hd] * jnp.exp(log_d[hd] - m_t[hd]) for hd in hds]
    ones = jnp.ones((lb, HEAD_DIM), BF16)
    s_vx = [_dot(s[hd].astype(BF16), jnp.concatenate([vg[hd], ones], axis=1)) for hd in hds]
    s_v = [x[:, :HEAD_DIM] for x in s_vx]
    s_sum = [x[:, HEAD_DIM:HEAD_DIM + 1] for x in s_vx]
    floor = [jnp.exp(-m_t[hd]) for hd in hds]
    kw = [kg[hd] * col(2, hd) for hd in hds]
    kw_b = [x.astype(BF16) for x in kw]
    c_st = [c_ref[hd] for hd in hds]
    n_st = [n_all[hd:hd + 1, :] for hd in hds]
    for c in range(n_chunks):
        sl = slice(c * CHUNK, (c + 1) * CHUNK)
        q_c = [_dot(qg[hd][sl], c_st[hd].astype(BF16)) for hd in hds]
        q_n = [jnp.sum(qg[hd][sl].astype(F32) * n_st[hd], axis=-1, keepdims=True) for hd in hds]
        cd = [carry_decay[c][hd:hd + 1, :] for hd in hds]
        c_st = [cd[hd] * c_st[hd] + _dot_tn(kw_b[hd][sl], vg[hd][sl]) for hd in hds]
        n_st = [cd[hd] * n_st[hd] + jnp.sum(kw[hd][sl], axis=0, keepdims=True) for hd in hds]
        for hd in hds:
            num = w_inter[hd][sl] * q_c[hd] + s_v[hd][sl]
            den = w_inter[hd][sl] * q_n[hd] + s_sum[hd][sl]
            hid = num / jnp.maximum(jnp.abs(den), floor[hd][sl])
            y_ref[sl, lanes[hd]] = (_sigmoid(o_all[sl, lanes[hd]])
                                    * _rms_norm(hid, ng_all[:, lanes[hd]])).astype(y_ref.dtype)
    for hd in hds:
        c_ref[hd] = c_st[hd]
        n_ref[hd:hd + 1, :] = n_st[hd]


def mlstm_heads(h, small_t, gate_bias, norm_g, fmasks, bmasks, bsz, seq):
    t = h.shape[0]
    lb = GROUP
    nb = seq // lb
    wide = QK_WIDTH
    first = COL_MLSTM * LANES // wide

    def col(j):
        return pl.BlockSpec((lb, wide), lambda b, s: (b * nb + s, first + j))

    fixed2 = lambda b, s: (0, 0)
    fixed3 = lambda b, s: (0, 0, 0)
    gp = jnp.zeros((16, LANES), F32).at[0:2 * HEADS, :].set(
        jnp.broadcast_to(gate_bias.astype(F32)[:, None], (2 * HEADS, LANES)))
    return pl.pallas_call(
        _mlstm_heads_body,
        grid=(bsz, nb),
        in_specs=[col(0), col(1), col(2), col(3),
                  pl.BlockSpec((4 * HEADS, lb), lambda b, s: (0, b * nb + s)),
                  pl.BlockSpec((16, LANES), fixed2),
                  pl.BlockSpec((1, wide), fixed2),
                  pl.BlockSpec((3, GROUP, GROUP), fixed3),
                  pl.BlockSpec((3, GROUP, GROUP), fixed3)],
        out_specs=pl.BlockSpec((lb, wide), lambda b, s: (b * nb + s, 0)),
        out_shape=jax.ShapeDtypeStruct((t, wide), BF16),
        scratch_shapes=[pltpu.VMEM((HEADS, HEAD_DIM, HEAD_DIM), F32),
                        pltpu.VMEM((8, HEAD_DIM), F32),
                        pltpu.VMEM((8, LANES), F32)],
        compiler_params=_cparams(("parallel", "arbitrary")),
        name="mlstm",
    )(h, h, h, h, small_t, gp, norm_g.reshape(1, wide), fmasks, bmasks)


def _rope_table_body(pos_ref, freq_ref, sign_ref, cc_ref, ss_ref):
    ang = pos_ref[...] * freq_ref[...]
    sign = sign_ref[...]
    cc_ref[...] = jnp.cos(ang) * jnp.abs(sign)
    ss_ref[...] = jnp.sin(ang) * sign


def rope_tables(positions, tm=512):
    t = positions.size
    half = MLA_ROPE // 2
    inv_freq = 1.0 / (ROPE_THETA ** (jnp.arange(0, MLA_ROPE, 2, dtype=F32) / MLA_ROPE))
    zeros = jnp.zeros((LANES - MLA_ROPE,), F32)
    freq = jnp.concatenate([inv_freq, inv_freq, zeros]).reshape(1, LANES)
    sign = jnp.concatenate([-jnp.ones((half,), F32), jnp.ones((half,), F32), zeros]).reshape(1, LANES)
    return pl.pallas_call(
        _rope_table_body,
        grid=(t // tm,),
        in_specs=[pl.BlockSpec((tm, 1), lambda i: (i, 0)),
                  pl.BlockSpec((1, LANES), lambda i: (0, 0)),
                  pl.BlockSpec((1, LANES), lambda i: (0, 0))],
        out_specs=[pl.BlockSpec((tm, LANES), lambda i: (i, 0)),
                   pl.BlockSpec((tm, LANES), lambda i: (i, 0))],
        out_shape=[jax.ShapeDtypeStruct((t, LANES), F32), jax.ShapeDtypeStruct((t, LANES), F32)],
        compiler_params=_cparams(("parallel",)),
        name="rope_tables",
    )(positions.astype(F32).reshape(t, 1), freq, sign)


def _mla_pre_body(h_ref, cc_ref, ss_ref, qg_ref, kvg_ref, wqa_ref, wqb_ref, wkv_ref,
                  q_ref, kn_ref, kr_ref, v_ref):
    hblk = h_ref[...].astype(F32)
    cc = cc_ref[...]
    ss = ss_ref[...]
    cq = _rms_norm(hblk[:, :MLA_Q_LORA], qg_ref[...]).astype(BF16)
    ckv = _rms_norm(hblk[:, MLA_Q_LORA:MLA_Q_LORA + MLA_KV_LORA], kvg_ref[...]).astype(BF16)
    off = MLA_Q_LORA + MLA_KV_LORA
    kr_ref[...] = (hblk[:, off:off + LANES] * cc + hblk[:, off + LANES:off + 2 * LANES] * ss).astype(BF16)
    kv = _dot(ckv, wkv_ref[...])
    kn_ref[...] = kv[:, :QK_WIDTH].astype(BF16)
    v_ref[...] = kv[:, QK_WIDTH:].astype(BF16)
    qa = _dot(cq, wqa_ref[...])
    qb = _dot(cq, wqb_ref[...])
    scale = (HEAD_DIM + MLA_ROPE) ** -0.5 * LOG2_E
    for hh in range(HEADS):
        base = 2 * HEAD_DIM * hh
        q_ref[:, base:base + HEAD_DIM] = (qa[:, base:base + HEAD_DIM] * scale).astype(BF16)
        rope = qa[:, base + HEAD_DIM:base + 2 * HEAD_DIM] * cc + qb[:, hh * LANES:(hh + 1) * LANES] * ss
        q_ref[:, base + HEAD_DIM:base + 2 * HEAD_DIM] = (rope * scale).astype(BF16)


def mla_prepare(h, cc, ss, q_norm_g, kv_norm_g, wq_a, wq_b, wkv, tm=512):
    t = h.shape[0]
    row = lambda i: (i, 0)
    fixed = lambda i: (0, 0)
    return pl.pallas_call(
        _mla_pre_body,
        grid=(t // tm,),
        in_specs=[pl.BlockSpec((tm, 1024), row),
                  pl.BlockSpec((tm, LANES), row),
                  pl.BlockSpec((tm, LANES), row),
                  pl.BlockSpec((1, MLA_Q_LORA), fixed),
                  pl.BlockSpec((1, MLA_KV_LORA), fixed),
                  pl.BlockSpec(wq_a.shape, fixed),
                  pl.BlockSpec(wq_b.shape, fixed),
                  pl.BlockSpec(wkv.shape, fixed)],
        out_specs=[pl.BlockSpec((tm, 2 * QK_WIDTH), row),
                   pl.BlockSpec((tm, QK_WIDTH), row),
                   pl.BlockSpec((tm, LANES), row),
                   pl.BlockSpec((tm, QK_WIDTH), row)],
        out_shape=[jax.ShapeDtypeStruct((t, 2 * QK_WIDTH), BF16),
                   jax.ShapeDtypeStruct((t, QK_WIDTH), BF16),
                   jax.ShapeDtypeStruct((t, LANES), BF16),
                   jax.ShapeDtypeStruct((t, QK_WIDTH), BF16)],
        compiler_params=_cparams(("parallel",)),
        name="mla_prepare",
    )(h, cc, ss, q_norm_g.reshape(1, -1), kv_norm_g.reshape(1, -1), wq_a, wq_b, wkv)


def _attn_body(q_ref, kn_ref, kr_ref, v_ref, o_ref, *, tk, n_heads):
    qi = pl.program_id(2)
    tq = q_ref.shape[0]
    hds = range(n_heads)
    lanes = [slice(hd * HEAD_DIM, (hd + 1) * HEAD_DIM) for hd in hds]
    q = [q_ref[:, 2 * hd * HEAD_DIM:2 * (hd + 1) * HEAD_DIM] for hd in hds]

    def step(j, carry, masked):
        m, l, acc = carry
        start = pl.multiple_of(j * tk, tk)
        kr = kr_ref[pl.ds(start, tk), :]
        s = [_dot_nt(q[hd], jnp.concatenate([kn_ref[pl.ds(start, tk), lanes[hd]], kr], axis=1)) for hd in hds]
        if masked:
            r = lax.broadcasted_iota(jnp.int32, (tq, tk), 0)
            c = lax.broadcasted_iota(jnp.int32, (tq, tk), 1)
            s = [jnp.where(r >= c, x, -jnp.inf) for x in s]
        m_new = [jnp.maximum(m[hd], jnp.max(s[hd], axis=-1, keepdims=True)) for hd in hds]
        alpha = [jnp.exp2(m[hd] - m_new[hd]) for hd in hds]
        p = [jnp.exp2(s[hd] - m_new[hd]) for hd in hds]
        l_new = [alpha[hd] * l[hd] + jnp.sum(p[hd], axis=-1, keepdims=True) for hd in hds]
        p_b = [x.astype(BF16) for x in p]
        acc = [alpha[hd] * acc[hd] + _dot(p_b[hd], v_ref[pl.ds(start, tk), lanes[hd]]) for hd in hds]
        return tuple(m_new), tuple(l_new), tuple(acc)

    init = (tuple(jnp.full((tq, 1), -jnp.inf, F32) for _ in hds),
            tuple(jnp.zeros((tq, 1), F32) for _ in hds),
            tuple(jnp.zeros((tq, HEAD_DIM), F32) for _ in hds))
    carry = lax.fori_loop(0, qi, lambda j, cr: step(j, cr, False), init)
    _, l, acc = step(qi, carry, True)
    for hd in hds:
        o_ref[:, lanes[hd]] = (acc[hd] / l[hd]).astype(o_ref.dtype)


def latent_attention(q, kn, kr, v, bsz, seq, tq=512, n_heads=2):
    t = q.shape[0]
    nq = seq // tq
    wide = n_heads * HEAD_DIM
    return pl.pallas_call(
        functools.partial(_attn_body, tk=tq, n_heads=n_heads),
        grid=(bsz, HEADS // n_heads, nq),
        in_specs=[pl.BlockSpec((tq, 2 * wide), lambda b, hh, i: (b * nq + i, hh)),
                  pl.BlockSpec((seq, wide), lambda b, hh, i: (b, hh)),
                  pl.BlockSpec((seq, LANES), lambda b, hh, i: (b, 0)),
                  pl.BlockSpec((seq, wide), lambda b, hh, i: (b, hh))],
        out_specs=pl.BlockSpec((tq, wide), lambda b, hh, i: (b * nq + i, hh)),
        out_shape=jax.ShapeDtypeStruct((t, QK_WIDTH), BF16),
        compiler_params=_cparams(("parallel", "parallel", "arbitrary")),
        name="latent_attention",
    )(q, kn, kr, v)


def _merge_body(yg_ref, ym_ref, ya_ref, g0_ref, g1_ref, g2_ref, gb_ref, pg_ref, pm_ref, pa_ref,
                wo_ref, x_ref, ln_g_ref, ln_b_ref, of_ref, ob_ref):
    gb = gb_ref[...]

    def branch(y_ref, p_ref, g_ref, idx):
        gate = _sigmoid(g_ref[...].astype(F32) + gb[:, idx * D_MODEL:(idx + 1) * D_MODEL])
        return gate * _dot(y_ref[...], p_ref[...])

    merged = branch(yg_ref, pg_ref, g0_ref, 0) + branch(ym_ref, pm_ref, g1_ref, 1) \
        + branch(ya_ref, pa_ref, g2_ref, 2)
    mix = _dot(merged.astype(BF16), wo_ref[...])
    y = _layer_norm(DEEPNORM_ALPHA * x_ref[...] + mix, ln_g_ref[...], ln_b_ref[...])
    of_ref[...] = y
    ob_ref[...] = y.astype(BF16)


def merge_branches(y_gdn, y_mlstm, y_mla, h, gate_bias, p_gdn, p_mlstm, p_mla, w_out, x, ln_g, ln_b, tm=512):
    t, d = x.shape
    row = lambda i: (i, 0)
    fixed = lambda i: (0, 0)
    ybs = pl.BlockSpec((tm, QK_WIDTH), row)
    pbs = pl.BlockSpec((QK_WIDTH, d), fixed)
    return pl.pallas_call(
        _merge_body,
        grid=(t // tm,),
        in_specs=[ybs, ybs, ybs,
                  pl.BlockSpec((tm, d), lambda i: (i, COL_GATE)),
                  pl.BlockSpec((tm, d), lambda i: (i, COL_GATE + 1)),
                  pl.BlockSpec((tm, d), lambda i: (i, COL_GATE + 2)),
                  pl.BlockSpec((1, 3 * d), fixed),
                  pbs, pbs, pbs,
                  pl.BlockSpec((d, d), fixed),
                  pl.BlockSpec((tm, d), row),
                  pl.BlockSpec((1, d), fixed),
                  pl.BlockSpec((1, d), fixed)],
        out_specs=[pl.BlockSpec((tm, d), row), pl.BlockSpec((tm, d), row)],
        out_shape=[jax.ShapeDtypeStruct((t, d), F32), jax.ShapeDtypeStruct((t, d), BF16)],
        compiler_params=_cparams(("parallel",)),
        name="merge_branches",
    )(y_gdn, y_mlstm, y_mla, h, h, h, gate_bias.reshape(1, 3 * d), p_gdn, p_mlstm, p_mla, w_out, x,
      ln_g.reshape(1, d), ln_b.reshape(1, d))


def _top2_sum(a, b, c, d):
    hi1, lo1 = jnp.maximum(a, b), jnp.minimum(a, b)
    hi2, lo2 = jnp.maximum(c, d), jnp.minimum(c, d)
    return jnp.maximum(hi1, hi2) + jnp.maximum(jnp.minimum(hi1, hi2), jnp.maximum(lo1, lo2))


def _router_body(x_ref, rwt_ref, rb_ref, su_ref, e_ref, w_ref, rank_ref, cnt_ref, carry_ref):
    @pl.when(pl.program_id(0) == 0)
    def _():
        carry_ref[...] = jnp.zeros_like(carry_ref)

    logits = _dot_nt(rwt_ref[...], x_ref[...], lax.Precision.HIGHEST)
    scores = _sigmoid(logits)
    biased = scores + rb_ref[...][:, :1]
    tm = logits.shape[1]
    brow = [biased[e:e + 1, :] for e in range(N_EXPERTS)]
    srow = [scores[e:e + 1, :] for e in range(N_EXPERTS)]
    best = _top2_sum(*brow[0:EXPERTS_PER_GROUP])
    grp = jnp.zeros((1, tm), jnp.int32)
    for g in range(1, N_GROUPS):
        gs = _top2_sum(*brow[g * EXPERTS_PER_GROUP:(g + 1) * EXPERTS_PER_GROUP])
        upd = gs > best
        best = jnp.where(upd, gs, best)
        grp = jnp.where(upd, g, grp)

    def in_group(rows, j):
        out = rows[j]
        for g in range(1, N_GROUPS):
            out = jnp.where(grp == g, rows[g * EXPERTS_PER_GROUP + j], out)
        return out

    ib = [in_group(brow, j) for j in range(EXPERTS_PER_GROUP)]
    isc = [in_group(srow, j) for j in range(EXPERTS_PER_GROUP)]
    v1, i1, s1 = ib[0], jnp.zeros((1, tm), jnp.int32), isc[0]
    for j in range(1, EXPERTS_PER_GROUP):
        upd = ib[j] > v1
        v1 = jnp.where(upd, ib[j], v1)
        i1 = jnp.where(upd, j, i1)
        s1 = jnp.where(upd, isc[j], s1)
    v2 = jnp.full((1, tm), -jnp.inf, F32)
    i2 = jnp.zeros((1, tm), jnp.int32)
    s2 = jnp.zeros((1, tm), F32)
    for j in range(EXPERTS_PER_GROUP):
        upd = jnp.logical_and(i1 != j, ib[j] > v2)
        v2 = jnp.where(upd, ib[j], v2)
        i2 = jnp.where(upd, j, i2)
        s2 = jnp.where(upd, isc[j], s2)
    e1 = grp * EXPERTS_PER_GROUP + i1
    e2 = grp * EXPERTS_PER_GROUP + i2
    total = s1 + s2
    e_ref[0:1, :] = e1
    e_ref[1:2, :] = e2
    w_ref[0:1, :] = s1 / total
    w_ref[1:2, :] = s2 / total
    erow = lax.broadcasted_iota(jnp.int32, (N_EXPERTS, tm), 0)
    oh1 = jnp.where(erow == e1, 1.0, 0.0).astype(F32)
    oh2 = jnp.where(erow == e2, 1.0, 0.0).astype(F32)
    both = oh1 + oh2
    before = _dot(both.astype(BF16), su_ref[...]) + carry_ref[...][:, :1]
    rank_ref[0:1, :] = jnp.sum(oh1 * before, axis=0, keepdims=True).astype(jnp.int32)
    rank_ref[1:2, :] = jnp.sum(oh2 * before, axis=0, keepdims=True).astype(jnp.int32)
    carry = carry_ref[...] + jnp.sum(both, axis=1, keepdims=True)
    carry_ref[...] = carry
    cnt_ref[...] = carry.astype(jnp.int32)


def route_tokens(x, router_w, router_bias, tm=512):
    t, d = x.shape
    strict_upper = jnp.triu(jnp.ones((tm, tm), BF16), k=1)
    fixed = lambda i: (0, 0)
    tok = lambda i: (0, i)
    return pl.pallas_call(
        _router_body,
        grid=(t // tm,),
        in_specs=[pl.BlockSpec((tm, d), lambda i: (i, 0)),
                  pl.BlockSpec((N_EXPERTS, d), fixed),
                  pl.BlockSpec((N_EXPERTS, LANES), fixed),
                  pl.BlockSpec((tm, tm), fixed)],
        out_specs=[pl.BlockSpec((TOP_K, tm), tok), pl.BlockSpec((TOP_K, tm), tok),
                   pl.BlockSpec((TOP_K, tm), tok), pl.BlockSpec((N_EXPERTS, LANES), fixed)],
        out_shape=[jax.ShapeDtypeStruct((TOP_K, t), jnp.int32), jax.ShapeDtypeStruct((TOP_K, t), F32),
                   jax.ShapeDtypeStruct((TOP_K, t), jnp.int32),
                   jax.ShapeDtypeStruct((N_EXPERTS, LANES), jnp.int32)],
        scratch_shapes=[pltpu.VMEM((N_EXPERTS, LANES), F32)],
        compiler_params=_cparams(("arbitrary",)),
        name="route_tokens",
    )(x, router_w.T, jnp.broadcast_to(router_bias.reshape(N_EXPERTS, 1), (N_EXPERTS, LANES)), strict_upper)


def _row_copy(x_hbm, buf, sem, slot, row, tok):
    return pltpu.make_async_copy(x_hbm.at[pl.ds(tok, 1), :], buf.at[slot, pl.ds(row, 1), :], sem.at[slot])


def _expert_body(be_ref, rt_ref, x_hbm, w1_ref, w3_ref, w2_ref, y_ref, buf, sem):
    i = pl.program_id(0)
    n = pl.num_programs(0)

    def wait_slot(slot):
        def wait_one(r, carry):
            _row_copy(x_hbm, buf, sem, slot, r, 0).wait()
            return carry

        lax.fori_loop(0, EXPERT_BLOCK, wait_one, 0, unroll=8)

    @pl.when(i == 0)
    def _():
        def one(r, carry):
            _row_copy(x_hbm, buf, sem, 0, r, rt_ref[r]).start()
            return carry

        lax.fori_loop(0, EXPERT_BLOCK, one, 0, unroll=8)

    slot = i % 2
    wait_slot(slot)
    nxt = jnp.minimum(i + 1, n - 1) * EXPERT_BLOCK
    for r in range(EXPERT_BLOCK):
        _row_copy(x_hbm, buf, sem, 1 - slot, r, rt_ref[nxt + r]).start()
    x = buf[slot].astype(BF16)
    h1 = _dot(x, w1_ref[...])
    h3 = _dot(x, w3_ref[...])
    act = (h1 * _sigmoid(h1) * h3).astype(BF16)
    y_ref[...] = _dot(act, w2_ref[...])

    @pl.when(i == n - 1)
    def _():
        wait_slot(1 - slot)


def expert_ffn(block_expert, row_tok, x, w1, w3, w2):
    t, d = x.shape
    n_rows = row_tok.shape[0]
    n_blocks = n_rows // EXPERT_BLOCK
    grid_spec = pltpu.PrefetchScalarGridSpec(
        num_scalar_prefetch=2,
        grid=(n_blocks,),
        in_specs=[pl.BlockSpec(memory_space=pl.ANY),
                  pl.BlockSpec((None, d, D_EXPERT), lambda i, be, rt: (be[i], 0, 0)),
                  pl.BlockSpec((None, d, D_EXPERT), lambda i, be, rt: (be[i], 0, 0)),
                  pl.BlockSpec((None, D_EXPERT, d), lambda i, be, rt: (be[i], 0, 0))],
        out_specs=pl.BlockSpec((EXPERT_BLOCK, d), lambda i, be, rt: (i, 0)),
        scratch_shapes=[pltpu.VMEM((2, EXPERT_BLOCK, d), F32),
                        pltpu.SemaphoreType.DMA((2,))])
    return pl.pallas_call(
        _expert_body,
        grid_spec=grid_spec,
        out_shape=jax.ShapeDtypeStruct((n_rows, d), F32),
        compiler_params=_cparams(("arbitrary",)),
        name="expert_ffn",
    )(block_expert, row_tok, x, w1, w3, w2)


def _pair_copy(y_hbm, buf, sem, slot, k, row, src):
    return pltpu.make_async_copy(y_hbm.at[pl.ds(src, 1), :], buf.at[slot, k, pl.ds(row, 1), :], sem.at[slot])


def _combine_body(dest_ref, y_hbm, w_ref, x_ref, ln_g_ref, ln_b_ref, of_ref, ob_ref, buf, sem, *, n_tok):
    i = pl.program_id(0)
    n = pl.num_programs(0)
    tm = x_ref.shape[0]

    def issue(blk, slot):
        base = blk * tm

        def one(r, carry):
            for k in range(TOP_K):
                _pair_copy(y_hbm, buf, sem, slot, k, r, dest_ref[k * n_tok + base + r]).start()
            return carry

        lax.fori_loop(0, tm, one, 0, unroll=8)

    @pl.when(i == 0)
    def _():
        issue(0, 0)

    @pl.when(i + 1 < n)
    def _():
        issue(i + 1, (i + 1) % 2)

    slot = i % 2

    def wait_one(r, carry):
        for k in range(TOP_K):
            _pair_copy(y_hbm, buf, sem, slot, k, r, 0).wait()
        return carry

    lax.fori_loop(0, tm, wait_one, 0, unroll=8)
    w = w_ref[...]
    ffn = w[:, 0:1] * buf[slot, 0] + w[:, 1:2] * buf[slot, 1]
    y = _layer_norm(DEEPNORM_ALPHA * x_ref[...] + ffn, ln_g_ref[...], ln_b_ref[...])
    of_ref[...] = y
    ob_ref[...] = y.astype(BF16)


def combine_experts(dest, y_rows, w_pad, x, ln_g, ln_b, tm=256):
    t, d = x.shape
    row = lambda i, *_: (i, 0)
    fixed = lambda i, *_: (0, 0)
    grid_spec = pltpu.PrefetchScalarGridSpec(
        num_scalar_prefetch=1,
        grid=(t // tm,),
        in_specs=[pl.BlockSpec(memory_space=pl.ANY),
                  pl.BlockSpec((tm, LANES), row),
                  pl.BlockSpec((tm, d), row),
                  pl.BlockSpec((1, d), fixed),
                  pl.BlockSpec((1, d), fixed)],
        out_specs=[pl.BlockSpec((tm, d), row), pl.BlockSpec((tm, d), row)],
        scratch_shapes=[pltpu.VMEM((2, TOP_K, tm, d), F32),
                        pltpu.SemaphoreType.DMA((2,))])
    return pl.pallas_call(
        functools.partial(_combine_body, n_tok=t),
        grid_spec=grid_spec,
        out_shape=[jax.ShapeDtypeStruct((t, d), F32), jax.ShapeDtypeStruct((t, d), BF16)],
        compiler_params=_cparams(("arbitrary",)),
        name="combine_experts",
    )(dest, y_rows, w_pad, x, ln_g.reshape(1, d), ln_b.reshape(1, d))


def routed_experts(xf, router_w, router_bias, w1, w3, w2, ln_g, ln_b):
    t, d = xf.shape
    expert, weight, rank, counts = route_tokens(xf, router_w, router_bias)
    counts = counts[:, 0]
    padded = (counts + EXPERT_BLOCK - 1) // EXPERT_BLOCK * EXPERT_BLOCK
    padded_ends = jnp.cumsum(padded)
    padded_starts = padded_ends - padded
    start_of = jnp.zeros_like(expert)
    for e in range(N_EXPERTS):
        start_of = jnp.where(expert == e, padded_starts[e], start_of)
    dest = (start_of + rank).reshape(TOP_K * t)
    n_rows = TOP_K * t + N_EXPERTS * EXPERT_BLOCK
    tok = jnp.tile(jnp.arange(t, dtype=jnp.int32), TOP_K)
    row_tok = jnp.zeros((n_rows,), jnp.int32).at[dest].set(tok)
    block_start = jnp.arange(n_rows // EXPERT_BLOCK, dtype=jnp.int32) * EXPERT_BLOCK
    block_expert = jnp.minimum(jnp.searchsorted(padded_ends, block_start, side='right'),
                               N_EXPERTS - 1).astype(jnp.int32)
    y_rows = expert_ffn(block_expert, row_tok, xf, w1.astype(BF16), w3.astype(BF16), w2.astype(BF16))
    w_pad = jnp.pad(weight.T, ((0, 0), (0, LANES - TOP_K)))
    return combine_experts(dest, y_rows, w_pad, xf, ln_g, ln_b)


def _merge_route_body(yg_ref, ym_ref, ya_ref, g0_ref, g1_ref, g2_ref, gb_ref, pg_ref, pm_ref, pa_ref,
                      wo_ref, x_ref, ln_g_ref, ln_b_ref, rw_ref, rb_ref, su_ref, eye_ref,
                      xw_ref, grp_ref, rank_ref, cnt_ref, carry_ref, yprev_ref):
    step = pl.program_id(0)

    @pl.when(step == 0)
    def _():
        carry_ref[...] = jnp.zeros_like(carry_ref)
        yprev_ref[...] = jnp.zeros_like(yprev_ref)

    gb = gb_ref[...]

    def branch(y_ref, p_ref, g_ref, idx):
        gate = _sigmoid(g_ref[...].astype(F32) + gb[:, idx * D_MODEL:(idx + 1) * D_MODEL])
        return gate * _dot(y_ref[...], p_ref[...])

    y = yprev_ref[...]
    xw_ref[:, 0:D_MODEL] = y

    y_hi = y.astype(BF16)
    y_lo = (y - y_hi.astype(F32)).astype(BF16)
    rw_hi, rw_lo = rw_ref[0], rw_ref[1]
    logits_tok = _dot(y_hi, rw_hi) + (_dot(y_lo, rw_hi) + _dot(y_hi, rw_lo))
    merged = branch(yg_ref, pg_ref, g0_ref, 0)
    logits = logits_tok.T[0:N_EXPERTS, :]
    scores = _sigmoid(logits)
    biased = scores + rb_ref[...][:, :1]
    tm = logits.shape[1]
    brow = [biased[e:e + 1, :] for e in range(N_EXPERTS)]
    srow = [scores[e:e + 1, :] for e in range(N_EXPERTS)]
    best = _top2_sum(*brow[0:EXPERTS_PER_GROUP])
    grp = jnp.zeros((1, tm), jnp.int32)
    for g in range(1, N_GROUPS):
        gs = _top2_sum(*brow[g * EXPERTS_PER_GROUP:(g + 1) * EXPERTS_PER_GROUP])
        upd = gs > best
        best = jnp.where(upd, gs, best)
        grp = jnp.where(upd, g, grp)

    def in_group(rows, j):
        out = rows[j]
        for g in range(1, N_GROUPS):
            out = jnp.where(grp == g, rows[g * EXPERTS_PER_GROUP + j], out)
        return out

    ib = [in_group(brow, j) for j in range(EXPERTS_PER_GROUP)]
    isc = [in_group(srow, j) for j in range(EXPERTS_PER_GROUP)]
    v1, i1, s1 = ib[0], jnp.zeros((1, tm), jnp.int32), isc[0]
    for j in range(1, EXPERTS_PER_GROUP):
        upd = ib[j] > v1
        v1 = jnp.where(upd, ib[j], v1)
        i1 = jnp.where(upd, j, i1)
        s1 = jnp.where(upd, isc[j], s1)
    v2 = jnp.full((1, tm), -jnp.inf, F32)
    i2 = jnp.zeros((1, tm), jnp.int32)
    s2 = jnp.zeros((1, tm), F32)
    for j in range(EXPERTS_PER_GROUP):
        upd = jnp.logical_and(i1 != j, ib[j] > v2)
        v2 = jnp.where(upd, ib[j], v2)
        i2 = jnp.where(upd, j, i2)
        s2 = jnp.where(upd, isc[j], s2)
    total = s1 + s2
    merged = merged + branch(ym_ref, pm_ref, g1_ref, 1)
    wrow = [jnp.where(i1 == j, s1 / total, jnp.where(i2 == j, s2 / total, 0.0)) for j in range(EXPERTS_PER_GROUP)]
    row_id = lax.broadcasted_iota(jnp.int32, (8, tm), 0)
    w8 = jnp.zeros((8, tm), F32)
    for j in range(EXPERTS_PER_GROUP):
        w8 = jnp.where(row_id == j, wrow[j], w8)
    merged = merged + branch(ya_ref, pa_ref, g2_ref, 2)
    stack = jnp.concatenate([w8, jnp.zeros((LANES - 8, tm), F32)], axis=0)
    xw_ref[:, D_MODEL:D_MODEL + LANES] = _rows_to_cols(stack, eye_ref[...])
    mix = _dot(merged.astype(BF16), wo_ref[...])
    grp_ref[...] = grp
    grow = lax.broadcasted_iota(jnp.int32, (N_EXPERTS, tm), 0)
    onehot = jnp.where(grow == grp, 1.0, 0.0).astype(F32)
    before = _dot(onehot.astype(BF16), su_ref[...]) + carry_ref[...][:, :1]
    rank_ref[...] = jnp.sum(onehot * before, axis=0, keepdims=True).astype(jnp.int32)
    counted = jnp.where(step > 0, 1.0, 0.0)
    carry = carry_ref[...] + counted * jnp.sum(onehot, axis=1, keepdims=True)
    carry_ref[...] = carry
    cnt_ref[...] = carry.astype(jnp.int32)
    yprev_ref[...] = _layer_norm(DEEPNORM_ALPHA * x_ref[...] + mix, ln_g_ref[...], ln_b_ref[...])


def merge_and_route(y_gdn, y_mlstm, y_mla, h, gate_bias, p_gdn, p_mlstm, p_mla, w_out, x, ln_g, ln_b,
                    router_w, router_bias, tm=512):
    t, d = x.shape
    nt = t // tm
    cur = lambda i: jnp.minimum(i, nt - 1)
    prev = lambda i: jnp.maximum(i - 1, 0)
    row = lambda i: (cur(i), 0)
    fixed = lambda i: (0, 0)
    tok = lambda i: (0, prev(i))
    ybs = pl.BlockSpec((tm, QK_WIDTH), row)
    pbs = pl.BlockSpec((QK_WIDTH, d), fixed)
    strict_upper = jnp.triu(jnp.ones((tm, tm), BF16), k=1)
    eye_b = jnp.eye(tm, dtype=BF16)
    rw = jnp.pad(router_w.astype(F32), ((0, 0), (0, LANES - N_EXPERTS)))
    rw_hi = rw.astype(BF16)
    rw_split = jnp.stack([rw_hi, (rw - rw_hi.astype(F32)).astype(BF16)])
    return pl.pallas_call(
        _merge_route_body,
        grid=(nt + 1,),
        in_specs=[ybs, ybs, ybs,
                  pl.BlockSpec((tm, d), lambda i: (cur(i), COL_GATE)),
                  pl.BlockSpec((tm, d), lambda i: (cur(i), COL_GATE + 1)),
                  pl.BlockSpec((tm, d), lambda i: (cur(i), COL_GATE + 2)),
                  pl.BlockSpec((1, 3 * d), fixed),
                  pbs, pbs, pbs,
                  pl.BlockSpec((d, d), fixed),
                  pl.BlockSpec((tm, d), row),
                  pl.BlockSpec((1, d), fixed),
                  pl.BlockSpec((1, d), fixed),
                  pl.BlockSpec((2, d, LANES), lambda i: (0, 0, 0)),
                  pl.BlockSpec((N_EXPERTS, LANES), fixed),
                  pl.BlockSpec((tm, tm), fixed),
                  pl.BlockSpec((tm, tm), fixed)],
        out_specs=[pl.BlockSpec((tm, d + LANES), lambda i: (prev(i), 0)),
                   pl.BlockSpec((1, tm), tok), pl.BlockSpec((1, tm), tok),
                   pl.BlockSpec((N_EXPERTS, LANES), fixed)],
        out_shape=[jax.ShapeDtypeStruct((t, d + LANES), F32),
                   jax.ShapeDtypeStruct((1, t), jnp.int32), jax.ShapeDtypeStruct((1, t), jnp.int32),
                   jax.ShapeDtypeStruct((N_EXPERTS, LANES), jnp.int32)],
        scratch_shapes=[pltpu.VMEM((N_EXPERTS, LANES), F32), pltpu.VMEM((tm, d), F32)],
        compiler_params=_cparams(("arbitrary",)),
        name="merge_route",
    )(y_gdn, y_mlstm, y_mla, h, h, h, gate_bias.reshape(1, 3 * d), p_gdn, p_mlstm, p_mla, w_out, x,
      ln_g.reshape(1, d), ln_b.reshape(1, d), rw_split,
      jnp.broadcast_to(router_bias.reshape(N_EXPERTS, 1), (N_EXPERTS, LANES)), strict_upper, eye_b)


def _gather_copy(x_hbm, buf, sem, slot, row, tok):
    return pltpu.make_async_copy(x_hbm.at[pl.ds(tok, 1), :], buf.at[slot, pl.ds(row, 1), :], sem.at[slot])


def _scatter_copy(ybuf, y_hbm, sem, slot, row, dst):
    return pltpu.make_async_copy(ybuf.at[slot, pl.ds(row, 1), :], y_hbm.at[pl.ds(dst, 1), :], sem.at[slot])


def _group_ffn_body(bg_ref, src_ref, dst_ref, x_hbm, w1_ref, w3_ref, w2_ref, y_hbm,
                    buf, ybuf, gsem, ssem, *, spare_row):
    i = pl.program_id(0)
    n = pl.num_programs(0)

    def wait_rows(make):
        def one(r, carry):
            make(r).wait()
            return carry

        lax.fori_loop(0, MOE_BLOCK, one, 0, unroll=8)

    @pl.when(i == 0)
    def _():
        def one(r, carry):
            _gather_copy(x_hbm, buf, gsem, 0, r, src_ref[r]).start()
            return carry

        lax.fori_loop(0, MOE_BLOCK, one, 0, unroll=8)
        ybuf[1] = jnp.zeros((MOE_BLOCK, D_MODEL), F32)

    def step(slot):
        other = 1 - slot
        wait_rows(lambda r: _gather_copy(x_hbm, buf, gsem, slot, r, 0))
        nxt = jnp.minimum(i + 1, n - 1) * MOE_BLOCK
        prv = jnp.maximum(i - 1, 0) * MOE_BLOCK
        for r in range(MOE_BLOCK):
            _gather_copy(x_hbm, buf, gsem, other, r, src_ref[nxt + r]).start()
        for r in range(MOE_BLOCK):
            dst = jnp.where(i > 0, dst_ref[prv + r], spare_row + r)
            _scatter_copy(ybuf, y_hbm, ssem, other, r, dst).start()
        xw = buf[slot]
        xb = xw[:, :D_MODEL].astype(BF16)
        acts = []
        for e in range(EXPERTS_PER_GROUP):
            cols = slice(e * D_EXPERT, (e + 1) * D_EXPERT)
            h1 = _dot(xb, w1_ref[:, cols])
            h3 = _dot(xb, w3_ref[:, cols])
            acts.append((h1 * _sigmoid(h1) * h3 * xw[:, D_MODEL + e:D_MODEL + e + 1]).astype(BF16))
        ybuf[slot] = _dot(jnp.concatenate(acts, axis=1), w2_ref[...])
        wait_rows(lambda r: _scatter_copy(ybuf, y_hbm, ssem, other, r, 0))

        @pl.when(i == n - 1)
        def _():
            wait_rows(lambda r: _gather_copy(x_hbm, buf, gsem, other, r, 0))

            def one(r, carry):
                _scatter_copy(ybuf, y_hbm, ssem, slot, r, dst_ref[i * MOE_BLOCK + r]).start()
                return carry

            lax.fori_loop(0, MOE_BLOCK, one, 0, unroll=8)
            wait_rows(lambda r: _scatter_copy(ybuf, y_hbm, ssem, slot, r, 0))

    pl.when(i % 2 == 0)(lambda: step(0))
    pl.when(i % 2 == 1)(lambda: step(1))


def group_ffn(block_group, row_src, row_dst, xw, w1g, w3g, w2g, n_out_rows, spare_row):
    d = D_MODEL
    n_rows = row_src.shape[0]
    n_blocks = n_rows // MOE_BLOCK
    wide = EXPERTS_PER_GROUP * D_EXPERT
    grid_spec = pltpu.PrefetchScalarGridSpec(
        num_scalar_prefetch=3,
        grid=(n_blocks,),
        in_specs=[pl.BlockSpec(memory_space=pl.ANY),
                  pl.BlockSpec((None, d, wide), lambda i, bg, rs, rd: (bg[i], 0, 0), pipeline_mode=pl.Buffered(1)),
                  pl.BlockSpec((None, d, wide), lambda i, bg, rs, rd: (bg[i], 0, 0), pipeline_mode=pl.Buffered(1)),
                  pl.BlockSpec((None, wide, d), lambda i, bg, rs, rd: (bg[i], 0, 0), pipeline_mode=pl.Buffered(1))],
        out_specs=pl.BlockSpec(memory_space=pl.ANY),
        scratch_shapes=[pltpu.VMEM((2, MOE_BLOCK, d + LANES), F32),
                        pltpu.VMEM((2, MOE_BLOCK, d), F32),
                        pltpu.SemaphoreType.DMA((2,)),
                        pltpu.SemaphoreType.DMA((2,))])
    return pl.pallas_call(
        functools.partial(_group_ffn_body, spare_row=spare_row),
        grid_spec=grid_spec,
        out_shape=jax.ShapeDtypeStruct((n_out_rows, d), F32),
        compiler_params=pltpu.CompilerParams(dimension_semantics=("arbitrary",),
                                             vmem_limit_bytes=MOE_VMEM_LIMIT),
        name="group_ffn",
    )(block_group, row_src, row_dst, xw, w1g, w3g, w2g)


def _residual_norm_body(x_ref, y_ref, ln_g_ref, ln_b_ref, of_ref, ob_ref):
    y = _layer_norm(DEEPNORM_ALPHA * x_ref[...] + y_ref[...], ln_g_ref[...], ln_b_ref[...])
    of_ref[...] = y
    ob_ref[...] = y.astype(BF16)


def residual_norm(xw, y_tok, ln_g, ln_b, tm=512):
    t = xw.shape[0]
    d = D_MODEL
    row = lambda i: (i, 0)
    fixed = lambda i: (0, 0)
    return pl.pallas_call(
        _residual_norm_body,
        grid=(t // tm,),
        in_specs=[pl.BlockSpec((tm, d), row), pl.BlockSpec((tm, d), row),
                  pl.BlockSpec((1, d), fixed), pl.BlockSpec((1, d), fixed)],
        out_specs=[pl.BlockSpec((tm, d), row), pl.BlockSpec((tm, d), row)],
        out_shape=[jax.ShapeDtypeStruct((t, d), F32), jax.ShapeDtypeStruct((t, d), BF16)],
        compiler_params=_cparams(("parallel",)),
        name="residual_norm",
    )(xw, y_tok, ln_g.reshape(1, d), ln_b.reshape(1, d))


def grouped_experts(xw, grp, rank, counts, w1, w3, w2, ln_g, ln_b):
    t = xw.shape[0]
    grp = grp.reshape(t)
    counts = counts[:N_GROUPS, 0]
    padded = (counts + MOE_BLOCK - 1) // MOE_BLOCK * MOE_BLOCK
    padded_ends = jnp.cumsum(padded)
    padded_starts = padded_ends - padded
    start_of = jnp.zeros_like(grp)
    for g in range(N_GROUPS):
        start_of = jnp.where(grp == g, padded_starts[g], start_of)
    dest = start_of + rank.reshape(t)
    n_rows = t + N_GROUPS * MOE_BLOCK
    slot_tok = jnp.full((n_rows,), -1, jnp.int32).at[dest].set(jnp.arange(t, dtype=jnp.int32))
    row_ids = jnp.arange(n_rows, dtype=jnp.int32)
    block_start = jnp.arange(n_rows // MOE_BLOCK, dtype=jnp.int32) * MOE_BLOCK
    block_group = jnp.minimum(jnp.sum(block_start[:, None] >= padded_ends[None, :], axis=1),
                              N_GROUPS - 1).astype(jnp.int32)
    pads = padded - counts
    pad_base = jnp.cumsum(pads) - pads - (padded_starts + counts)
    row_group = jnp.repeat(block_group, MOE_BLOCK)
    pad_off = jnp.zeros_like(row_ids)
    for g in range(N_GROUPS):
        pad_off = jnp.where(row_group == g, pad_base[g], pad_off)
    row_src = jnp.maximum(slot_tok, 0)
    row_dst = jnp.where(slot_tok < 0, t + row_ids + pad_off, slot_tok)
    d = D_MODEL
    wide = EXPERTS_PER_GROUP * D_EXPERT
    w1g = w1.reshape(N_GROUPS, EXPERTS_PER_GROUP, d, D_EXPERT).transpose(0, 2, 1, 3).reshape(N_GROUPS, d, wide)
    w3g = w3.reshape(N_GROUPS, EXPERTS_PER_GROUP, d, D_EXPERT).transpose(0, 2, 1, 3).reshape(N_GROUPS, d, wide)
    w2g = w2.reshape(N_GROUPS, wide, d)
    spare_row = n_rows
    y_tok = group_ffn(block_group, row_src, row_dst, xw, w1g.astype(BF16), w3g.astype(BF16), w2g.astype(BF16),
                      spare_row + MOE_BLOCK, spare_row)
    return residual_norm(xw, y_tok, ln_g, ln_b)


def _split_w_in(w_in):
    sizes = (QK_WIDTH, QK_WIDTH, QK_WIDTH, QK_WIDTH, HEADS, HEADS,
             QK_WIDTH, QK_WIDTH, QK_WIDTH, QK_WIDTH, HEADS, HEADS,
             MLA_Q_LORA, MLA_KV_LORA, MLA_ROPE, 3 * D_MODEL)
    parts, acc = [], 0
    for size in sizes:
        parts.append(w_in[:, acc:acc + size])
        acc += size
    return parts


def _arrange_w_in(w_in):
    (g_q, g_k, g_v, g_z, g_a, g_b, m_q, m_k, m_v, m_o, m_i, m_f, c_q, c_kv, k_rope, gates) = _split_w_in(w_in)
    d = w_in.shape[0]
    half = MLA_ROPE // 2
    pad64 = jnp.zeros((d, LANES - MLA_ROPE), w_in.dtype)
    rope_sw = jnp.concatenate([k_rope[:, half:], k_rope[:, :half]], axis=1)
    main = jnp.concatenate([c_q, c_kv, k_rope, pad64, rope_sw, pad64, jnp.zeros((d, LANES), w_in.dtype),
                            g_q, g_k, g_v, g_z, m_q, m_k, m_v, m_o, gates], axis=1)
    small = jnp.concatenate([g_a, g_b, m_i, m_f, jnp.zeros((d, LANES - 4 * HEADS), w_in.dtype)], axis=1)
    return main.astype(BF16), small.astype(BF16)


def _arrange_mla(w_uq, w_ukv):
    half = MLA_ROPE // 2
    wq = w_uq.reshape(MLA_Q_LORA, HEADS, HEAD_DIM + MLA_ROPE)
    nope, rope = wq[:, :, :HEAD_DIM], wq[:, :, HEAD_DIM:]
    pad = jnp.zeros((MLA_Q_LORA, HEADS, LANES - MLA_ROPE), w_uq.dtype)
    wq_a = jnp.concatenate([nope, rope, pad], axis=2).reshape(MLA_Q_LORA, HEADS * 2 * HEAD_DIM)
    rope_sw = jnp.concatenate([rope[:, :, half:], rope[:, :, :half]], axis=2)
    wq_b = jnp.concatenate([rope_sw, pad], axis=2).reshape(MLA_Q_LORA, HEADS * LANES)
    wkv = w_ukv.reshape(MLA_KV_LORA, HEADS, 2 * HEAD_DIM)
    wkv = jnp.concatenate([wkv[:, :, :HEAD_DIM].reshape(MLA_KV_LORA, QK_WIDTH),
                           wkv[:, :, HEAD_DIM:].reshape(MLA_KV_LORA, QK_WIDTH)], axis=1)
    return wq_a.astype(BF16), wq_b.astype(BF16), wkv.astype(BF16)


def kernel(x, positions, ln_in_g, ln_in_b, w_in, gdn_conv, gdn_a_log, gdn_dt_bias, gdn_norm, mlstm_gate_bias, mlstm_norm, mla_q_norm, mla_kv_norm, mla_w_uq, mla_w_ukv, w_br_gdn, w_br_mlstm, w_br_mla, gate_bias, w_out, ln1_g, ln1_b, router_w, router_bias, moe_w1, moe_w3, moe_w2, ln2_g, ln2_b):
    bsz, seq, d = x.shape
    t = bsz * seq
    xf, xb = layer_norm_entry(x.reshape(t, d), ln_in_g, ln_in_b)
    cc, ss = rope_tables(positions)
    fmasks, bmasks = _group_constants()
    for l in range(DEPTH):
        w_main, w_small = _arrange_w_in(w_in[l])
        h, small = in_projection(xb, w_main, w_small)
        small_t = small[:, :4 * HEADS].T
        y_gdn = gated_deltanet_heads(h, small_t, gdn_conv[l], gdn_a_log[l], gdn_dt_bias[l], gdn_norm[l],
                                     fmasks, bmasks, bsz, seq)
        y_mlstm = mlstm_heads(h, small_t, mlstm_gate_bias[l], mlstm_norm[l], fmasks, bmasks, bsz, seq)
        wq_a, wq_b, wkv = _arrange_mla(mla_w_uq[l], mla_w_ukv[l])
        q, kn, kr, v = mla_prepare(h, cc, ss, mla_q_norm[l], mla_kv_norm[l], wq_a, wq_b, wkv)
        y_mla = latent_attention(q, kn, kr, v, bsz, seq)
        xw, grp, rank, counts = merge_and_route(
            y_gdn, y_mlstm, y_mla, h, gate_bias[l], w_br_gdn[l].astype(BF16), w_br_mlstm[l].astype(BF16),
            w_br_mla[l].astype(BF16), w_out[l].astype(BF16), xf, ln1_g[l], ln1_b[l], router_w, router_bias)
        xf, xb = grouped_experts(xw, grp, rank, counts, moe_w1[l], moe_w3[l], moe_w2[l], ln2_g[l], ln2_b[l])
    return xf.reshape(bsz, seq, d)
```

```python
import functools

import jax
import jax.numpy as jnp
from jax import lax
from jax.experimental import pallas as pl
from jax.experimental.pallas import tpu as pltpu

F32 = jnp.float32
BF16 = jnp.bfloat16

D_MODEL = 1024
DEPTH = 2
HEADS = 4
HEAD_DIM = 128
CHUNK = 64
CONV_WIDTH = 4
GATE_CAP = 15.0
MLA_ROPE = 64
MLA_Q_LORA = 384
MLA_KV_LORA = 256
ROPE_THETA = 10000.0
N_EXPERTS = 16
N_GROUPS = 4
EXPERTS_PER_GROUP = 4
D_EXPERT = 512
LN_EPS = 1e-5
RMS_EPS = 1e-6
DEEPNORM_ALPHA = (2 * DEPTH) ** 0.25
LOG2_E = 1.4426950408889634

LANES = 128
QK_WIDTH = HEADS * HEAD_DIM
COL_GDN = 8
COL_MLSTM = 24
COL_GATE = 5
GROUP = 256
MOE_BLOCK = 512
VMEM_LIMIT = 48 * 1024 * 1024
MOE_VMEM_LIMIT = 56 * 1024 * 1024


def _cparams(sem):
    return pltpu.CompilerParams(dimension_semantics=sem, vmem_limit_bytes=VMEM_LIMIT)


def _sigmoid(x):
    return 1.0 / (1.0 + jnp.exp(-x))


def _layer_norm(x, g, b):
    mu = jnp.mean(x, axis=-1, keepdims=True)
    xc = x - mu
    var = jnp.mean(xc * xc, axis=-1, keepdims=True)
    return xc * lax.rsqrt(var + LN_EPS) * g + b


def _rms_norm(x, g):
    return x * lax.rsqrt(jnp.mean(x * x, axis=-1, keepdims=True) + RMS_EPS) * g


def _dot(a, b):
    return jnp.dot(a, b, preferred_element_type=F32)


def _dot_nt(a, b):
    return lax.dot_general(a, b, (((1,), (1,)), ((), ())), preferred_element_type=F32)


def _dot_tn(a, b):
    return lax.dot_general(a, b, (((0,), (0,)), ((), ())), preferred_element_type=F32)


def _ln_body(x_ref, g_ref, b_ref, of_ref, ob_ref):
    y = _layer_norm(x_ref[...], g_ref[...], b_ref[...])
    of_ref[...] = y
    ob_ref[...] = y.astype(BF16)


def layer_norm_entry(x, g, b, tm=512):
    t, d = x.shape
    return pl.pallas_call(
        _ln_body,
        grid=(t // tm,),
        in_specs=[pl.BlockSpec((tm, d), lambda i: (i, 0)),
                  pl.BlockSpec((1, d), lambda i: (0, 0)),
                  pl.BlockSpec((1, d), lambda i: (0, 0))],
        out_specs=[pl.BlockSpec((tm, d), lambda i: (i, 0)),
                   pl.BlockSpec((tm, d), lambda i: (i, 0))],
        out_shape=[jax.ShapeDtypeStruct((t, d), F32), jax.ShapeDtypeStruct((t, d), BF16)],
        compiler_params=_cparams(("parallel",)),
        name="ln_entry",
    )(x, g.reshape(1, d), b.reshape(1, d))


def _inproj_body(x_ref, w_ref, ws_ref, h_ref, hs_ref):
    x = x_ref[...]
    h_ref[...] = _dot(x, w_ref[...]).astype(BF16)

    @pl.when(pl.program_id(1) == 0)
    def _():
        hs_ref[...] = _dot(x, ws_ref[...])


def in_projection(xb, w_main, w_small, tm=2048, tn=512):
    t, d = xb.shape
    n = w_main.shape[1]
    return pl.pallas_call(
        _inproj_body,
        grid=(t // tm, n // tn),
        in_specs=[pl.BlockSpec((tm, d), lambda i, j: (i, 0)),
                  pl.BlockSpec((d, tn), lambda i, j: (0, j)),
                  pl.BlockSpec((d, LANES), lambda i, j: (0, 0))],
        out_specs=[pl.BlockSpec((tm, tn), lambda i, j: (i, j)),
                   pl.BlockSpec((tm, LANES), lambda i, j: (i, 0))],
        out_shape=[jax.ShapeDtypeStruct((t, n), BF16), jax.ShapeDtypeStruct((t, LANES), F32)],
        compiler_params=_cparams(("parallel", "arbitrary")),
        name="in_proj",
    )(xb, w_main, w_small)


def _softplus(x):
    return jnp.maximum(x, 0.0) + jnp.log1p(jnp.exp(-jnp.abs(x)))


def _soft_cap(x):
    return GATE_CAP * jnp.tanh(x / GATE_CAP)


def _log_sigmoid(x):
    return jnp.minimum(x, 0.0) - jnp.log1p(jnp.exp(-jnp.abs(x)))


def _group_constants():
    r = jnp.arange(GROUP, dtype=jnp.int32)[:, None]
    c = jnp.arange(GROUP, dtype=jnp.int32)[None, :]
    same = (r // CHUNK) == (c // CHUNK)
    neg = jnp.where(same & (r >= c), 0.0, -jnp.inf).astype(F32)
    strict = (same & (r > c)).astype(F32)
    eye = (r == c).astype(F32)
    upper = (same & (r <= c)).astype(BF16)
    last = (r == (c // CHUNK) * CHUNK + CHUNK - 1).astype(BF16)
    return jnp.stack([neg, strict, eye]), jnp.stack([upper, last, eye.astype(BF16)])


def _split3(x):
    hi = x.astype(BF16)
    r1 = x - hi.astype(F32)
    mid = r1.astype(BF16)
    return hi, mid, (r1 - mid.astype(F32)).astype(BF16)


def _rows_times(rows8, mat_b, terms):
    rows = jnp.concatenate([rows8, jnp.zeros_like(rows8)], axis=0)
    parts = _split3(rows)[:terms]
    out = _dot(parts[0], mat_b)
    for p in parts[1:]:
        out = out + _dot(p, mat_b)
    return out[0:8]


def _rows_to_cols(stack, eye_b):
    parts = _split3(stack)
    out = _dot_nt(eye_b, parts[0])
    for p in parts[1:]:
        out = out + _dot_nt(eye_b, p)
    return out


def _conv_silu(x_ref, w, e_ref):
    lb = x_ref.shape[0]
    e_ref[8:, :] = x_ref[...].astype(F32)
    y = w[0:1, :] * e_ref[pl.ds(8 - CONV_WIDTH + 1, lb), :]
    for j in range(1, CONV_WIDTH):
        y = y + w[j:j + 1, :] * e_ref[pl.ds(8 - CONV_WIDTH + 1 + j, lb), :]
    e_ref[0:8, :] = e_ref[lb:lb + 8, :]
    return y * _sigmoid(y)


def _recurrent_body(gq_ref, gk_ref, gv_ref, gz_ref, cw_ref, mq_ref, mk_ref, mv_ref, mo_ref, smt_ref,
                    gpg_ref, gpm_ref, ngg_ref, ngm_ref, fm_ref, bm_ref, yg_ref, ym_ref,
                    state_ref, eq_ref, ek_ref, ev_ref, c_ref, n_ref, m_ref):
    lb = gq_ref.shape[0]
    n_chunks = lb // CHUNK
    hds = range(HEADS)
    lanes = [slice(hd * HEAD_DIM, (hd + 1) * HEAD_DIM) for hd in hds]

    @pl.when(pl.program_id(1) == 0)
    def _():
        state_ref[...] = jnp.zeros_like(state_ref)
        for e_ref in (eq_ref, ek_ref, ev_ref):
            e_ref[0:8, :] = jnp.zeros((8, QK_WIDTH), F32)
        c_ref[...] = jnp.zeros_like(c_ref)
        n_ref[...] = jnp.zeros_like(n_ref)
        m_ref[...] = jnp.zeros_like(m_ref)

    neg, strict01, eye = fm_ref[0], fm_ref[1], fm_ref[2]
    up_b, last_b, eye_b = bm_ref[0], bm_ref[1], bm_ref[2]
    cw = cw_ref[...]

    mq_all = mq_ref[...]
    mk_all = mk_ref[...].astype(F32) * (HEAD_DIM ** -0.5)
    mq = [mq_all[:, lanes[hd]] for hd in hds]
    mk = [mk_all[:, lanes[hd]] for hd in hds]
    m_qk = [_dot_nt(mq[hd], mk[hd].astype(BF16)) for hd in hds]
    q_all = _conv_silu(gq_ref, cw[:, 0:QK_WIDTH], eq_ref)
    capped = _soft_cap(smt_ref[2 * HEADS:4 * HEADS, :] + gpm_ref[...][0:8, 0:1])
    i8 = capped
    f8 = pltpu.roll(_log_sigmoid(capped), HEADS, axis=0)
    bc8 = _rows_times(f8, up_b, 2)
    k_all = _conv_silu(gk_ref, cw[:, QK_WIDTH:2 * QK_WIDTH], ek_ref)
    bl8 = _rows_times(bc8, last_b, 3)
    lkw8 = bl8 - bc8 + i8
    v_all = _conv_silu(gv_ref, cw[:, 2 * QK_WIDTH:3 * QK_WIDTH], ev_ref)
    chunk_id = lax.broadcasted_iota(jnp.int32, (8, lb), 1) // CHUNK
    m_st = m_ref[...][:, 0:1]
    m_prev_row = jnp.zeros((8, lb), F32)
    m_next_row = jnp.zeros((8, lb), F32)
    carry_decay = []
    for c in range(n_chunks):
        in_c = chunk_id == c
        b_last = bl8[:, c * CHUNK:c * CHUNK + 1]
        m_new = jnp.maximum(b_last + m_st, jnp.max(jnp.where(in_c, lkw8, -jnp.inf), axis=1, keepdims=True))
        carry_decay.append(jnp.exp(b_last + m_st - m_new))
        m_prev_row = jnp.where(in_c, m_st, m_prev_row)
        m_next_row = jnp.where(in_c, m_new, m_next_row)
        m_st = m_new
    m_ref[...] = jnp.broadcast_to(m_st, m_ref.shape)
    gp = gpg_ref[...]
    g8 = -jnp.exp(gp[0:8, 0:1]) * _softplus(smt_ref[0:8, :] + gp[8:16, 0:1])
    beta8 = _sigmoid(smt_ref[HEADS:HEADS + 8, :])
    gc8 = _rows_times(g8, up_b, 2)
    m_stack = jnp.concatenate([bc8, bc8 + m_prev_row, jnp.exp(lkw8 - m_next_row),
                               jnp.zeros((LANES - 24, lb), F32)], axis=0)
    m_cols = _rows_to_cols(m_stack, eye_b)
    gl8 = _rows_times(gc8, last_b, 3)
    egc8 = jnp.exp(gc8)
    tail8 = jnp.exp(gl8 - gc8)
    elast8 = jnp.exp(gl8)
    g_stack = jnp.concatenate([gc8, beta8, egc8, tail8, beta8 * egc8,
                               jnp.zeros((LANES - 40, lb), F32)], axis=0)
    g_cols = _rows_to_cols(g_stack, eye_b)
    mcol = lambda j, hd: m_cols[:, 8 * j + hd:8 * j + hd + 1]
    gcol = lambda j, hd: g_cols[:, 8 * j + hd:8 * j + hd + 1]

    qg = [q_all[:, lanes[hd]] for hd in hds]
    kg = [k_all[:, lanes[hd]] for hd in hds]
    qg = [x * lax.rsqrt(jnp.sum(x * x, axis=-1, keepdims=True) + RMS_EPS) * (HEAD_DIM ** -0.5) for x in qg]
    kg = [x * lax.rsqrt(jnp.sum(x * x, axis=-1, keepdims=True) + RMS_EPS) for x in kg]
    kg_b = [x.astype(BF16) for x in kg]
    mv_all = mv_ref[...]
    mv = [mv_all[:, lanes[hd]] for hd in hds]
    log_d = [mcol(0, hd) - bc8[hd:hd + 1, :] + i8[hd:hd + 1, :] + neg for hd in hds]
    m_t = [jnp.maximum(mcol(1, hd), jnp.max(log_d[hd], axis=-1, keepdims=True)) for hd in hds]
    decay = [jnp.exp(gcol(0, hd) - gc8[hd:hd + 1, :] + neg) for hd in hds]
    a_mat = [_dot_nt((kg[hd] * gcol(1, hd)).astype(BF16), kg_b[hd]) * decay[hd] * strict01 for hd in hds]
    w_inter = [jnp.exp(mcol(1, hd) - m_t[hd]) for hd in hds]
    s = [m_qk[hd] * jnp.exp(log_d[hd] - m_t[hd]) for hd in hds]
    pw = [-a for a in a_mat]
    t_mat = [eye + p for p in pw]
    ones = jnp.ones((lb, HEAD_DIM), BF16)
    s_vx = None
    floor = kw = kw_b = None
    for it in range(5):
        pw_b = [p.astype(BF16) for p in pw]
        pw = [_dot(p, p) for p in pw_b]
        t_mat = [t_mat[hd] + _dot(t_mat[hd].astype(BF16), pw[hd].astype(BF16)) for hd in hds]
        if it == 0:
            s_vx = [_dot(s[hd].astype(BF16), jnp.concatenate([mv[hd], ones], axis=1)) for hd in hds]
        if it == 1:
            floor = [jnp.exp(-m_t[hd]) for hd in hds]
            kw = [mk[hd] * mcol(2, hd) for hd in hds]
            kw_b = [x.astype(BF16) for x in kw]
    s_v = [x[:, :HEAD_DIM] for x in s_vx]
    s_sum = [x[:, HEAD_DIM:HEAD_DIM + 1] for x in s_vx]
    rhs = [jnp.concatenate([(v_all[:, lanes[hd]] * gcol(1, hd)).astype(BF16),
                            (kg[hd] * gcol(4, hd)).astype(BF16)], axis=1) for hd in hds]
    uw = [_dot(t_mat[hd].astype(BF16), rhs[hd]).astype(BF16) for hd in hds]
    qk = [(_dot_nt(qg[hd].astype(BF16), kg_b[hd]) * decay[hd]).astype(BF16) for hd in hds]
    qk_uw = [_dot(qk[hd], uw[hd]) for hd in hds]
    q_eff = [qg[hd] * gcol(2, hd) - qk_uw[hd][:, HEAD_DIM:] for hd in hds]
    k_tail = [(kg[hd] * gcol(3, hd)).astype(BF16) for hd in hds]
    z_all = gz_ref[...].astype(F32)
    o_all = mo_ref[...].astype(F32)
    ngg = ngg_ref[...]
    ngm = ngm_ref[...]
    state = [state_ref[hd] for hd in hds]
    c_st = [c_ref[hd] for hd in hds]
    n_all = n_ref[...]
    n_st = [n_all[hd:hd + 1, :] for hd in hds]
    for c in range(n_chunks):
        sl = slice(c * CHUNK, (c + 1) * CHUNK)
        kt_uw = [_dot_tn(k_tail[hd][sl], uw[hd][sl]) for hd in hds]
        lhs = [jnp.concatenate([q_eff[hd][sl], kt_uw[hd][:, HEAD_DIM:]], axis=0).astype(BF16) for hd in hds]
        res = [_dot(lhs[hd], state[hd].astype(BF16)) for hd in hds]
        q_c = [_dot(mq[hd][sl], c_st[hd].astype(BF16)) for hd in hds]
        q_n = [jnp.sum(mq[hd][sl].astype(F32) * n_st[hd], axis=-1, keepdims=True) for hd in hds]
        state = [state[hd] * elast8[hd:hd + 1, c * CHUNK:c * CHUNK + 1] - res[hd][CHUNK:]
                 + kt_uw[hd][:, :HEAD_DIM] for hd in hds]
        cd = [carry_decay[c][hd:hd + 1, :] for hd in hds]
        c_st = [cd[hd] * c_st[hd] + _dot_tn(kw_b[hd][sl], mv[hd][sl]) for hd in hds]
        n_st = [cd[hd] * n_st[hd] + jnp.sum(kw[hd][sl], axis=0, keepdims=True) for hd in hds]
        for hd in hds:
            out = res[hd][:CHUNK] + qk_uw[hd][sl, :HEAD_DIM]
            zc = z_all[sl, lanes[hd]]
            yg_ref[sl, lanes[hd]] = (_rms_norm(out, ngg) * (zc * _sigmoid(zc))).astype(yg_ref.dtype)
        for hd in hds:
            num = w_inter[hd][sl] * q_c[hd] + s_v[hd][sl]
            den = w_inter[hd][sl] * q_n[hd] + s_sum[hd][sl]
            hid = num / jnp.maximum(jnp.abs(den), floor[hd][sl])
            ym_ref[sl, lanes[hd]] = (_sigmoid(o_all[sl, lanes[hd]])
                                     * _rms_norm(hid, ngm[:, lanes[hd]])).astype(ym_ref.dtype)
    for hd in hds:
        state_ref[hd] = state[hd]
        c_ref[hd] = c_st[hd]
        n_ref[hd:hd + 1, :] = n_st[hd]


def _gate_params(first, second):
    out = jnp.zeros((16, LANES), F32)
    out = out.at[0:first.shape[0], :].set(jnp.broadcast_to(first.astype(F32)[:, None], (first.shape[0], LANES)))
    if second is not None:
        out = out.at[8:8 + second.shape[0], :].set(
            jnp.broadcast_to(second.astype(F32)[:, None], (second.shape[0], LANES)))
    return out


def recurrent_mixers(h, small_t, conv_w, a_log, dt_bias, gdn_norm, gate_bias, mlstm_norm, fmasks, bmasks,
                     bsz, seq):
    t = h.shape[0]
    lb = GROUP
    nb = seq // lb
    wide = QK_WIDTH
    gdn0 = COL_GDN * LANES // wide
    mls0 = COL_MLSTM * LANES // wide

    def col(j):
        return pl.BlockSpec((lb, wide), lambda b, s: (b * nb + s, j))

    fixed2 = lambda b, s: (0, 0)
    fixed3 = lambda b, s: (0, 0, 0)
    out_spec = pl.BlockSpec((lb, wide), lambda b, s: (b * nb + s, 0))
    return pl.pallas_call(
        _recurrent_body,
        grid=(bsz, nb),
        in_specs=[col(gdn0), col(gdn0 + 1), col(gdn0 + 2), col(gdn0 + 3),
                  pl.BlockSpec((CONV_WIDTH, 3 * wide), fixed2),
                  col(mls0), col(mls0 + 1), col(mls0 + 2), col(mls0 + 3),
                  pl.BlockSpec((4 * HEADS, lb), lambda b, s: (0, b * nb + s)),
                  pl.BlockSpec((16, LANES), fixed2),
                  pl.BlockSpec((16, LANES), fixed2),
                  pl.BlockSpec((1, HEAD_DIM), fixed2),
                  pl.BlockSpec((1, wide), fixed2),
                  pl.BlockSpec((3, GROUP, GROUP), fixed3),
                  pl.BlockSpec((3, GROUP, GROUP), fixed3)],
        out_specs=[out_spec, out_spec],
        out_shape=[jax.ShapeDtypeStruct((t, wide), BF16), jax.ShapeDtypeStruct((t, wide), BF16)],
        scratch_shapes=[pltpu.VMEM((HEADS, HEAD_DIM, HEAD_DIM), F32),
                        pltpu.VMEM((lb + 8, wide), F32),
                        pltpu.VMEM((lb + 8, wide), F32),
                        pltpu.VMEM((lb + 8, wide), F32),
                        pltpu.VMEM((HEADS, HEAD_DIM, HEAD_DIM), F32),
                        pltpu.VMEM((8, HEAD_DIM), F32),
                        pltpu.VMEM((8, LANES), F32)],
        compiler_params=_cparams(("parallel", "arbitrary")),
        name="recurrent_mixers",
    )(h, h, h, h, conv_w, h, h, h, h, small_t, _gate_params(a_log, dt_bias), _gate_params(gate_bias, None),
      gdn_norm.reshape(1, HEAD_DIM), mlstm_norm.reshape(1, wide), fmasks, bmasks)


def _rope_table_body(pos_ref, freq_ref, sign_ref, cc_ref, ss_ref):
    ang = pos_ref[...] * freq_ref[...]
    sign = sign_ref[...]
    cc_ref[...] = jnp.cos(ang) * jnp.abs(sign)
    ss_ref[...] = jnp.sin(ang) * sign


def rope_tables(positions, tm=512):
    t = positions.size
    half = MLA_ROPE // 2
    inv_freq = 1.0 / (ROPE_THETA ** (jnp.arange(0, MLA_ROPE, 2, dtype=F32) / MLA_ROPE))
    zeros = jnp.zeros((LANES - MLA_ROPE,), F32)
    freq = jnp.concatenate([inv_freq, inv_freq, zeros]).reshape(1, LANES)
    sign = jnp.concatenate([-jnp.ones((half,), F32), jnp.ones((half,), F32), zeros]).reshape(1, LANES)
    return pl.pallas_call(
        _rope_table_body,
        grid=(t // tm,),
        in_specs=[pl.BlockSpec((tm, 1), lambda i: (i, 0)),
                  pl.BlockSpec((1, LANES), lambda i: (0, 0)),
                  pl.BlockSpec((1, LANES), lambda i: (0, 0))],
        out_specs=[pl.BlockSpec((tm, LANES), lambda i: (i, 0)),
                   pl.BlockSpec((tm, LANES), lambda i: (i, 0))],
        out_shape=[jax.ShapeDtypeStruct((t, LANES), F32), jax.ShapeDtypeStruct((t, LANES), F32)],
        compiler_params=_cparams(("parallel",)),
        name="rope_tables",
    )(positions.astype(F32).reshape(t, 1), freq, sign)


def _mla_pre_body(h_ref, cc_ref, ss_ref, qg_ref, kvg_ref, wqa_ref, wqb_ref, wkv_ref,
                  q_ref, kn_ref, kr_ref, v_ref):
    hblk = h_ref[...].astype(F32)
    cc = cc_ref[...]
    ss = ss_ref[...]
    cq = _rms_norm(hblk[:, :MLA_Q_LORA], qg_ref[...]).astype(BF16)
    ckv = _rms_norm(hblk[:, MLA_Q_LORA:MLA_Q_LORA + MLA_KV_LORA], kvg_ref[...]).astype(BF16)
    off = MLA_Q_LORA + MLA_KV_LORA
    kr_ref[...] = (hblk[:, off:off + LANES] * cc + hblk[:, off + LANES:off + 2 * LANES] * ss).astype(BF16)
    kv = _dot(ckv, wkv_ref[...])
    kn_ref[...] = kv[:, :QK_WIDTH].astype(BF16)
    v_ref[...] = kv[:, QK_WIDTH:].astype(BF16)
    qa = _dot(cq, wqa_ref[...])
    qb = _dot(cq, wqb_ref[...])
    scale = (HEAD_DIM + MLA_ROPE) ** -0.5 * LOG2_E
    for hh in range(HEADS):
        base = 2 * HEAD_DIM * hh
        q_ref[:, base:base + HEAD_DIM] = (qa[:, base:base + HEAD_DIM] * scale).astype(BF16)
        rope = qa[:, base + HEAD_DIM:base + 2 * HEAD_DIM] * cc + qb[:, hh * LANES:(hh + 1) * LANES] * ss
        q_ref[:, base + HEAD_DIM:base + 2 * HEAD_DIM] = (rope * scale).astype(BF16)


def mla_prepare(h, cc, ss, q_norm_g, kv_norm_g, wq_a, wq_b, wkv, tm=512):
    t = h.shape[0]
    row = lambda i: (i, 0)
    fixed = lambda i: (0, 0)
    return pl.pallas_call(
        _mla_pre_body,
        grid=(t // tm,),
        in_specs=[pl.BlockSpec((tm, 1024), row),
                  pl.BlockSpec((tm, LANES), row),
                  pl.BlockSpec((tm, LANES), row),
                  pl.BlockSpec((1, MLA_Q_LORA), fixed),
                  pl.BlockSpec((1, MLA_KV_LORA), fixed),
                  pl.BlockSpec(wq_a.shape, fixed),
                  pl.BlockSpec(wq_b.shape, fixed),
                  pl.BlockSpec(wkv.shape, fixed)],
        out_specs=[pl.BlockSpec((tm, 2 * QK_WIDTH), row),
                   pl.BlockSpec((tm, QK_WIDTH), row),
                   pl.BlockSpec((tm, LANES), row),
                   pl.BlockSpec((tm, QK_WIDTH), row)],
        out_shape=[jax.ShapeDtypeStruct((t, 2 * QK_WIDTH), BF16),
                   jax.ShapeDtypeStruct((t, QK_WIDTH), BF16),
                   jax.ShapeDtypeStruct((t, LANES), BF16),
                   jax.ShapeDtypeStruct((t, QK_WIDTH), BF16)],
        compiler_params=_cparams(("parallel",)),
        name="mla_prepare",
    )(h, cc, ss, q_norm_g.reshape(1, -1), kv_norm_g.reshape(1, -1), wq_a, wq_b, wkv)


def _attn_body(q_ref, kn_ref, kr_ref, v_ref, o_ref, *, tk, n_heads):
    qi = pl.program_id(2)
    tq = q_ref.shape[0]
    hds = range(n_heads)
    lanes = [slice(hd * HEAD_DIM, (hd + 1) * HEAD_DIM) for hd in hds]
    q = [q_ref[:, 2 * hd * HEAD_DIM:2 * (hd + 1) * HEAD_DIM] for hd in hds]

    def step(j, carry, masked):
        m, l, acc = carry
        start = pl.multiple_of(j * tk, tk)
        kr = kr_ref[pl.ds(start, tk), :]
        s = [_dot_nt(q[hd], jnp.concatenate([kn_ref[pl.ds(start, tk), lanes[hd]], kr], axis=1)) for hd in hds]
        if masked:
            r = lax.broadcasted_iota(jnp.int32, (tq, tk), 0)
            c = lax.broadcasted_iota(jnp.int32, (tq, tk), 1)
            s = [jnp.where(r >= c, x, -jnp.inf) for x in s]
        m_new = [jnp.maximum(m[hd], jnp.max(s[hd], axis=-1, keepdims=True)) for hd in hds]
        alpha = [jnp.exp2(m[hd] - m_new[hd]) for hd in hds]
        p = [jnp.exp2(s[hd] - m_new[hd]) for hd in hds]
        l_new = [alpha[hd] * l[hd] + jnp.sum(p[hd], axis=-1, keepdims=True) for hd in hds]
        p_b = [x.astype(BF16) for x in p]
        acc = [alpha[hd] * acc[hd] + _dot(p_b[hd], v_ref[pl.ds(start, tk), lanes[hd]]) for hd in hds]
        return tuple(m_new), tuple(l_new), tuple(acc)

    init = (tuple(jnp.full((tq, 1), -jnp.inf, F32) for _ in hds),
            tuple(jnp.zeros((tq, 1), F32) for _ in hds),
            tuple(jnp.zeros((tq, HEAD_DIM), F32) for _ in hds))
    carry = lax.fori_loop(0, qi, lambda j, cr: step(j, cr, False), init)
    _, l, acc = step(qi, carry, True)
    for hd in hds:
        o_ref[:, lanes[hd]] = (acc[hd] / l[hd]).astype(o_ref.dtype)


def latent_attention(q, kn, kr, v, bsz, seq, tq=512, n_heads=2):
    t = q.shape[0]
    nq = seq // tq
    wide = n_heads * HEAD_DIM
    return pl.pallas_call(
        functools.partial(_attn_body, tk=tq, n_heads=n_heads),
        grid=(bsz, HEADS // n_heads, nq),
        in_specs=[pl.BlockSpec((tq, 2 * wide), lambda b, hh, i: (b * nq + i, hh)),
                  pl.BlockSpec((seq, wide), lambda b, hh, i: (b, hh)),
                  pl.BlockSpec((seq, LANES), lambda b, hh, i: (b, 0)),
                  pl.BlockSpec((seq, wide), lambda b, hh, i: (b, hh))],
        out_specs=pl.BlockSpec((tq, wide), lambda b, hh, i: (b * nq + i, hh)),
        out_shape=jax.ShapeDtypeStruct((t, QK_WIDTH), BF16),
        compiler_params=_cparams(("parallel", "parallel", "arbitrary")),
        name="latent_attention",
    )(q, kn, kr, v)


def _top2_sum(a, b, c, d):
    hi1, lo1 = jnp.maximum(a, b), jnp.minimum(a, b)
    hi2, lo2 = jnp.maximum(c, d), jnp.minimum(c, d)
    return jnp.maximum(hi1, hi2) + jnp.maximum(jnp.minimum(hi1, hi2), jnp.maximum(lo1, lo2))


def _merge_route_body(yg_ref, ym_ref, ya_ref, g0_ref, g1_ref, g2_ref, gb_ref, pg_ref, pm_ref, pa_ref,
                      wo_ref, x_ref, ln_g_ref, ln_b_ref, rw_ref, rb_ref, su_ref, eye_ref,
                      xw_ref, grp_ref, rank_ref, cnt_ref, carry_ref, yprev_ref):
    step = pl.program_id(0)

    @pl.when(step == 0)
    def _():
        carry_ref[...] = jnp.zeros_like(carry_ref)
        yprev_ref[...] = jnp.zeros_like(yprev_ref)

    gb = gb_ref[...]

    def branch(y_ref, p_ref, g_ref, idx):
        gate = _sigmoid(g_ref[...].astype(F32) + gb[:, idx * D_MODEL:(idx + 1) * D_MODEL])
        return gate * _dot(y_ref[...], p_ref[...])

    y = yprev_ref[...]
    xw_ref[:, 0:D_MODEL] = y
    y_hi = y.astype(BF16)
    y_lo = (y - y_hi.astype(F32)).astype(BF16)
    rw_hi, rw_lo = rw_ref[0], rw_ref[1]
    logits_tok = _dot(y_hi, rw_hi) + (_dot(y_lo, rw_hi) + _dot(y_hi, rw_lo))
    merged = branch(yg_ref, pg_ref, g0_ref, 0)
    logits = logits_tok.T[0:N_EXPERTS, :]
    scores = _sigmoid(logits)
    biased = scores + rb_ref[...][:, :1]
    tm = logits.shape[1]
    brow = [biased[e:e + 1, :] for e in range(N_EXPERTS)]
    srow = [scores[e:e + 1, :] for e in range(N_EXPERTS)]
    best = _top2_sum(*brow[0:EXPERTS_PER_GROUP])
    grp = jnp.zeros((1, tm), jnp.int32)
    for g in range(1, N_GROUPS):
        gs = _top2_sum(*brow[g * EXPERTS_PER_GROUP:(g + 1) * EXPERTS_PER_GROUP])
        upd = gs > best
        best = jnp.where(upd, gs, best)
        grp = jnp.where(upd, g, grp)

    def in_group(rows, j):
        out = rows[j]
        for g in range(1, N_GROUPS):
            out = jnp.where(grp == g, rows[g * EXPERTS_PER_GROUP + j], out)
        return out

    ib = [in_group(brow, j) for j in range(EXPERTS_PER_GROUP)]
    isc = [in_group(srow, j) for j in range(EXPERTS_PER_GROUP)]
    v1, i1, s1 = ib[0], jnp.zeros((1, tm), jnp.int32), isc[0]
    for j in range(1, EXPERTS_PER_GROUP):
        upd = ib[j] > v1
        v1 = jnp.where(upd, ib[j], v1)
        i1 = jnp.where(upd, j, i1)
        s1 = jnp.where(upd, isc[j], s1)
    v2 = jnp.full((1, tm), -jnp.inf, F32)
    i2 = jnp.zeros((1, tm), jnp.int32)
    s2 = jnp.zeros((1, tm), F32)
    for j in range(EXPERTS_PER_GROUP):
        upd = jnp.logical_and(i1 != j, ib[j] > v2)
        v2 = jnp.where(upd, ib[j], v2)
        i2 = jnp.where(upd, j, i2)
        s2 = jnp.where(upd, isc[j], s2)
    total = s1 + s2
    merged = merged + branch(ym_ref, pm_ref, g1_ref, 1)
    wrow = [jnp.where(i1 == j, s1 / total, jnp.where(i2 == j, s2 / total, 0.0)) for j in range(EXPERTS_PER_GROUP)]
    row_id = lax.broadcasted_iota(jnp.int32, (8, tm), 0)
    w8 = jnp.zeros((8, tm), F32)
    for j in range(EXPERTS_PER_GROUP):
        w8 = jnp.where(row_id == j, wrow[j], w8)
    merged = merged + branch(ya_ref, pa_ref, g2_ref, 2)
    stack = jnp.concatenate([w8, jnp.zeros((LANES - 8, tm), F32)], axis=0)
    xw_ref[:, D_MODEL:D_MODEL + LANES] = _rows_to_cols(stack, eye_ref[...])
    mix = _dot(merged.astype(BF16), wo_ref[...])
    grp_ref[...] = grp
    grow = lax.broadcasted_iota(jnp.int32, (N_EXPERTS, tm), 0)
    onehot = jnp.where(grow == grp, 1.0, 0.0).astype(F32)
    before = _dot(onehot.astype(BF16), su_ref[...]) + carry_ref[...][:, :1]
    rank_ref[...] = jnp.sum(onehot * before, axis=0, keepdims=True).astype(jnp.int32)
    counted = jnp.where(step > 0, 1.0, 0.0)
    carry = carry_ref[...] + counted * jnp.sum(onehot, axis=1, keepdims=True)
    carry_ref[...] = carry
    cnt_ref[...] = carry.astype(jnp.int32)
    yprev_ref[...] = _layer_norm(DEEPNORM_ALPHA * x_ref[...] + mix, ln_g_ref[...], ln_b_ref[...])


def merge_and_route(y_gdn, y_mlstm, y_mla, h, gate_bias, p_gdn, p_mlstm, p_mla, w_out, x, ln_g, ln_b,
                    router_w, router_bias, tm=512):
    t, d = x.shape
    nt = t // tm
    cur = lambda i: jnp.minimum(i, nt - 1)
    prev = lambda i: jnp.maximum(i - 1, 0)
    row = lambda i: (cur(i), 0)
    fixed = lambda i: (0, 0)
    tok = lambda i: (0, prev(i))
    ybs = pl.BlockSpec((tm, QK_WIDTH), row)
    pbs = pl.BlockSpec((QK_WIDTH, d), fixed)
    strict_upper = jnp.triu(jnp.ones((tm, tm), BF16), k=1)
    eye_b = jnp.eye(tm, dtype=BF16)
    rw = jnp.pad(router_w.astype(F32), ((0, 0), (0, LANES - N_EXPERTS)))
    rw_hi = rw.astype(BF16)
    rw_split = jnp.stack([rw_hi, (rw - rw_hi.astype(F32)).astype(BF16)])
    return pl.pallas_call(
        _merge_route_body,
        grid=(nt + 1,),
        in_specs=[ybs, ybs, ybs,
                  pl.BlockSpec((tm, d), lambda i: (cur(i), COL_GATE)),
                  pl.BlockSpec((tm, d), lambda i: (cur(i), COL_GATE + 1)),
                  pl.BlockSpec((tm, d), lambda i: (cur(i), COL_GATE + 2)),
                  pl.BlockSpec((1, 3 * d), fixed),
                  pbs, pbs, pbs,
                  pl.BlockSpec((d, d), fixed),
                  pl.BlockSpec((tm, d), row),
                  pl.BlockSpec((1, d), fixed),
                  pl.BlockSpec((1, d), fixed),
                  pl.BlockSpec((2, d, LANES), lambda i: (0, 0, 0)),
                  pl.BlockSpec((N_EXPERTS, LANES), fixed),
                  pl.BlockSpec((tm, tm), fixed),
                  pl.BlockSpec((tm, tm), fixed)],
        out_specs=[pl.BlockSpec((tm, d + LANES), lambda i: (prev(i), 0)),
                   pl.BlockSpec((1, tm), tok), pl.BlockSpec((1, tm), tok),
                   pl.BlockSpec((N_EXPERTS, LANES), fixed)],
        out_shape=[jax.ShapeDtypeStruct((t, d + LANES), F32),
                   jax.ShapeDtypeStruct((1, t), jnp.int32), jax.ShapeDtypeStruct((1, t), jnp.int32),
                   jax.ShapeDtypeStruct((N_EXPERTS, LANES), jnp.int32)],
        scratch_shapes=[pltpu.VMEM((N_EXPERTS, LANES), F32), pltpu.VMEM((tm, d), F32)],
        compiler_params=_cparams(("arbitrary",)),
        name="merge_route",
    )(y_gdn, y_mlstm, y_mla, h, h, h, gate_bias.reshape(1, 3 * d), p_gdn, p_mlstm, p_mla, w_out, x,
      ln_g.reshape(1, d), ln_b.reshape(1, d), rw_split,
      jnp.broadcast_to(router_bias.reshape(N_EXPERTS, 1), (N_EXPERTS, LANES)), strict_upper, eye_b)


def _gather_copy(x_hbm, buf, sem, slot, row, tok):
    return pltpu.make_async_copy(x_hbm.at[pl.ds(tok, 1), :], buf.at[slot, pl.ds(row, 1), :], sem.at[slot])


def _scatter_copy(ybuf, y_hbm, sem, slot, row, dst):
    return pltpu.make_async_copy(ybuf.at[slot, pl.ds(row, 1), :], y_hbm.at[pl.ds(dst, 1), :], sem.at[slot])


def _group_ffn_body(bg_ref, src_ref, dst_ref, x_hbm, w1_ref, w3_ref, w2_ref, y_hbm,
                    buf, ybuf, gsem, ssem, *, spare_row):
    i = pl.program_id(0)
    n = pl.num_programs(0)

    def wait_rows(make):
        def one(r, carry):
            make(r).wait()
            return carry

        lax.fori_loop(0, MOE_BLOCK, one, 0, unroll=8)

    @pl.when(i == 0)
    def _():
        def one(r, carry):
            _gather_copy(x_hbm, buf, gsem, 0, r, src_ref[r]).start()
            return carry

        lax.fori_loop(0, MOE_BLOCK, one, 0, unroll=8)
        ybuf[1] = jnp.zeros((MOE_BLOCK, D_MODEL), F32)

    def step(slot):
        other = 1 - slot
        wait_rows(lambda r: _gather_copy(x_hbm, buf, gsem, slot, r, 0))
        nxt = jnp.minimum(i + 1, n - 1) * MOE_BLOCK
        prv = jnp.maximum(i - 1, 0) * MOE_BLOCK
        for r in range(MOE_BLOCK):
            _gather_copy(x_hbm, buf, gsem, other, r, src_ref[nxt + r]).start()
        for r in range(MOE_BLOCK):
            dst = jnp.where(i > 0, dst_ref[prv + r], spare_row + r)
            _scatter_copy(ybuf, y_hbm, ssem, other, r, dst).start()
        xw = buf[slot]
        xb = xw[:, :D_MODEL].astype(BF16)
        acts = []
        for e in range(EXPERTS_PER_GROUP):
            cols = slice(e * D_EXPERT, (e + 1) * D_EXPERT)
            h1 = _dot(xb, w1_ref[:, cols])
            h3 = _dot(xb, w3_ref[:, cols])
            acts.append((h1 * _sigmoid(h1) * h3 * xw[:, D_MODEL + e:D_MODEL + e + 1]).astype(BF16))
        ybuf[slot] = _dot(jnp.concatenate(acts, axis=1), w2_ref[...])
        wait_rows(lambda r: _scatter_copy(ybuf, y_hbm, ssem, other, r, 0))

        @pl.when(i == n - 1)
        def _():
            wait_rows(lambda r: _gather_copy(x_hbm, buf, gsem, other, r, 0))

            def one(r, carry):
                _scatter_copy(ybuf, y_hbm, ssem, slot, r, dst_ref[i * MOE_BLOCK + r]).start()
                return carry

            lax.fori_loop(0, MOE_BLOCK, one, 0, unroll=8)
            wait_rows(lambda r: _scatter_copy(ybuf, y_hbm, ssem, slot, r, 0))

    pl.when(i % 2 == 0)(lambda: step(0))
    pl.when(i % 2 == 1)(lambda: step(1))


def group_ffn(block_group, row_src, row_dst, xw, w1g, w3g, w2g, n_out_rows, spare_row):
    d = D_MODEL
    n_rows = row_src.shape[0]
    n_blocks = n_rows // MOE_BLOCK
    wide = EXPERTS_PER_GROUP * D_EXPERT
    once = pl.Buffered(1)
    grid_spec = pltpu.PrefetchScalarGridSpec(
        num_scalar_prefetch=3,
        grid=(n_blocks,),
        in_specs=[pl.BlockSpec(memory_space=pl.ANY),
                  pl.BlockSpec((None, d, wide), lambda i, bg, rs, rd: (bg[i], 0, 0), pipeline_mode=once),
                  pl.BlockSpec((None, d, wide), lambda i, bg, rs, rd: (bg[i], 0, 0), pipeline_mode=once),
                  pl.BlockSpec((None, wide, d), lambda i, bg, rs, rd: (bg[i], 0, 0), pipeline_mode=once)],
        out_specs=pl.BlockSpec(memory_space=pl.ANY),
        scratch_shapes=[pltpu.VMEM((2, MOE_BLOCK, d + LANES), F32),
                        pltpu.VMEM((2, MOE_BLOCK, d), F32),
                        pltpu.SemaphoreType.DMA((2,)),
                        pltpu.SemaphoreType.DMA((2,))])
    return pl.pallas_call(
        functools.partial(_group_ffn_body, spare_row=spare_row),
        grid_spec=grid_spec,
        out_shape=jax.ShapeDtypeStruct((n_out_rows, d), F32),
        compiler_params=pltpu.CompilerParams(dimension_semantics=("arbitrary",),
                                             vmem_limit_bytes=MOE_VMEM_LIMIT),
        name="group_ffn",
    )(block_group, row_src, row_dst, xw, w1g, w3g, w2g)


def _residual_norm_body(x_ref, y_ref, ln_g_ref, ln_b_ref, of_ref, ob_ref):
    y = _layer_norm(DEEPNORM_ALPHA * x_ref[...] + y_ref[...], ln_g_ref[...], ln_b_ref[...])
    of_ref[...] = y
    ob_ref[...] = y.astype(BF16)


def residual_norm(xw, y_tok, ln_g, ln_b, tm=512):
    t = xw.shape[0]
    d = D_MODEL
    row = lambda i: (i, 0)
    fixed = lambda i: (0, 0)
    return pl.pallas_call(
        _residual_norm_body,
        grid=(t // tm,),
        in_specs=[pl.BlockSpec((tm, d), row), pl.BlockSpec((tm, d), row),
                  pl.BlockSpec((1, d), fixed), pl.BlockSpec((1, d), fixed)],
        out_specs=[pl.BlockSpec((tm, d), row), pl.BlockSpec((tm, d), row)],
        out_shape=[jax.ShapeDtypeStruct((t, d), F32), jax.ShapeDtypeStruct((t, d), BF16)],
        compiler_params=_cparams(("parallel",)),
        name="residual_norm",
    )(xw, y_tok, ln_g.reshape(1, d), ln_b.reshape(1, d))


def grouped_experts(xw, grp, rank, counts, w1, w3, w2, ln_g, ln_b):
    t = xw.shape[0]
    grp = grp.reshape(t)
    counts = counts[:N_GROUPS, 0]
    padded = (counts + MOE_BLOCK - 1) // MOE_BLOCK * MOE_BLOCK
    padded_ends = jnp.cumsum(padded)
    padded_starts = padded_ends - padded
    start_of = jnp.zeros_like(grp)
    for g in range(N_GROUPS):
        start_of = jnp.where(grp == g, padded_starts[g], start_of)
    dest = start_of + rank.reshape(t)
    n_rows = t + N_GROUPS * MOE_BLOCK
    slot_tok = jnp.full((n_rows,), -1, jnp.int32).at[dest].set(jnp.arange(t, dtype=jnp.int32))
    row_ids = jnp.arange(n_rows, dtype=jnp.int32)
    block_start = jnp.arange(n_rows // MOE_BLOCK, dtype=jnp.int32) * MOE_BLOCK
    block_group = jnp.minimum(jnp.sum(block_start[:, None] >= padded_ends[None, :], axis=1),
                              N_GROUPS - 1).astype(jnp.int32)
    pads = padded - counts
    pad_base = jnp.cumsum(pads) - pads - (padded_starts + counts)
    row_group = jnp.repeat(block_group, MOE_BLOCK)
    pad_off = jnp.zeros_like(row_ids)
    for g in range(N_GROUPS):
        pad_off = jnp.where(row_group == g, pad_base[g], pad_off)
    row_src = jnp.maximum(slot_tok, 0)
    row_dst = jnp.where(slot_tok < 0, t + row_ids + pad_off, slot_tok)
    d = D_MODEL
    wide = EXPERTS_PER_GROUP * D_EXPERT
    w1g = w1.reshape(N_GROUPS, EXPERTS_PER_GROUP, d, D_EXPERT).transpose(0, 2, 1, 3).reshape(N_GROUPS, d, wide)
    w3g = w3.reshape(N_GROUPS, EXPERTS_PER_GROUP, d, D_EXPERT).transpose(0, 2, 1, 3).reshape(N_GROUPS, d, wide)
    w2g = w2.reshape(N_GROUPS, wide, d)
    spare_row = n_rows
    y_tok = group_ffn(block_group, row_src, row_dst, xw, w1g.astype(BF16), w3g.astype(BF16), w2g.astype(BF16),
                      spare_row + MOE_BLOCK, spare_row)
    return residual_norm(xw, y_tok, ln_g, ln_b)


def _split_w_in(w_in):
    sizes = (QK_WIDTH, QK_WIDTH, QK_WIDTH, QK_WIDTH, HEADS, HEADS,
             QK_WIDTH, QK_WIDTH, QK_WIDTH, QK_WIDTH, HEADS, HEADS,
             MLA_Q_LORA, MLA_KV_LORA, MLA_ROPE, 3 * D_MODEL)
    parts, acc = [], 0
    for size in sizes:
        parts.append(w_in[:, acc:acc + size])
        acc += size
    return parts


def _arrange_w_in(w_in):
    (g_q, g_k, g_v, g_z, g_a, g_b, m_q, m_k, m_v, m_o, m_i, m_f, c_q, c_kv, k_rope, gates) = _split_w_in(w_in)
    d = w_in.shape[0]
    half = MLA_ROPE // 2
    pad64 = jnp.zeros((d, LANES - MLA_ROPE), w_in.dtype)
    rope_sw = jnp.concatenate([k_rope[:, half:], k_rope[:, :half]], axis=1)
    main = jnp.concatenate([c_q, c_kv, k_rope, pad64, rope_sw, pad64, jnp.zeros((d, LANES), w_in.dtype),
                            g_q, g_k, g_v, g_z, m_q, m_k, m_v, m_o, gates], axis=1)
    small = jnp.concatenate([g_a, g_b, m_i, m_f, jnp.zeros((d, LANES - 4 * HEADS), w_in.dtype)], axis=1)
    return main.astype(BF16), small.astype(BF16)


def _arrange_mla(w_uq, w_ukv):
    half = MLA_ROPE // 2
    wq = w_uq.reshape(MLA_Q_LORA, HEADS, HEAD_DIM + MLA_ROPE)
    nope, rope = wq[:, :, :HEAD_DIM], wq[:, :, HEAD_DIM:]
    pad = jnp.zeros((MLA_Q_LORA, HEADS, LANES - MLA_ROPE), w_uq.dtype)
    wq_a = jnp.concatenate([nope, rope, pad], axis=2).reshape(MLA_Q_LORA, HEADS * 2 * HEAD_DIM)
    rope_sw = jnp.concatenate([rope[:, :, half:], rope[:, :, :half]], axis=2)
    wq_b = jnp.concatenate([rope_sw, pad], axis=2).reshape(MLA_Q_LORA, HEADS * LANES)
    wkv = w_ukv.reshape(MLA_KV_LORA, HEADS, 2 * HEAD_DIM)
    wkv = jnp.concatenate([wkv[:, :, :HEAD_DIM].reshape(MLA_KV_LORA, QK_WIDTH),
                           wkv[:, :, HEAD_DIM:].reshape(MLA_KV_LORA, QK_WIDTH)], axis=1)
    return wq_a.astype(BF16), wq_b.astype(BF16), wkv.astype(BF16)


def kernel(x, positions, ln_in_g, ln_in_b, w_in, gdn_conv, gdn_a_log, gdn_dt_bias, gdn_norm, mlstm_gate_bias, mlstm_norm, mla_q_norm, mla_kv_norm, mla_w_uq, mla_w_ukv, w_br_gdn, w_br_mlstm, w_br_mla, gate_bias, w_out, ln1_g, ln1_b, router_w, router_bias, moe_w1, moe_w3, moe_w2, ln2_g, ln2_b):
    bsz, seq, d = x.shape
    t = bsz * seq
    xf, xb = layer_norm_entry(x.reshape(t, d), ln_in_g, ln_in_b)
    cc, ss = rope_tables(positions)
    fmasks, bmasks = _group_constants()
    for l in range(DEPTH):
        w_main, w_small = _arrange_w_in(w_in[l])
        h, small = in_projection(xb, w_main, w_small)
        small_t = small[:, :4 * HEADS].T
        y_gdn, y_mlstm = recurrent_mixers(h, small_t, gdn_conv[l], gdn_a_log[l], gdn_dt_bias[l], gdn_norm[l],
                                          mlstm_gate_bias[l], mlstm_norm[l], fmasks, bmasks, bsz, seq)
        wq_a, wq_b, wkv = _arrange_mla(mla_w_uq[l], mla_w_ukv[l])
        q, kn, kr, v = mla_prepare(h, cc, ss, mla_q_norm[l], mla_kv_norm[l], wq_a, wq_b, wkv)
        y_mla = latent_attention(q, kn, kr, v, bsz, seq)
        xw, grp, rank, counts = merge_and_route(
            y_gdn, y_mlstm, y_mla, h, gate_bias[l], w_br_gdn[l].astype(BF16), w_br_mlstm[l].astype(BF16),
            w_br_mla[l].astype(BF16), w_out[l].astype(BF16), xf, ln1_g[l], ln1_b[l], router_w, router_bias)
        xf, xb = grouped_experts(xw, grp, rank, counts, moe_w1[l], moe_w3[l], moe_w2[l], ln2_g[l], ln2_b[l])
    return xf.reshape(bsz, seq, d)
```

```python
import functools

import jax
import jax.numpy as jnp
from jax import lax
from jax.experimental import pallas as pl
from jax.experimental.pallas import tpu as pltpu

F32 = jnp.float32
BF16 = jnp.bfloat16

D_MODEL = 1024
DEPTH = 2
HEADS = 4
HEAD_DIM = 128
CHUNK = 64
CONV_WIDTH = 4
GATE_CAP = 15.0
MLA_ROPE = 64
MLA_Q_LORA = 384
MLA_KV_LORA = 256
ROPE_THETA = 10000.0
N_EXPERTS = 16
N_GROUPS = 4
EXPERTS_PER_GROUP = 4
D_EXPERT = 512
LN_EPS = 1e-5
RMS_EPS = 1e-6
DEEPNORM_ALPHA = (2 * DEPTH) ** 0.25
LOG2_E = 1.4426950408889634

LANES = 128
QK_WIDTH = HEADS * HEAD_DIM
COL_GDN = 8
COL_MLSTM = 24
COL_GATE = 5
GROUP = 256
MOE_BLOCK = 512
VMEM_LIMIT = 48 * 1024 * 1024
MOE_VMEM_LIMIT = 56 * 1024 * 1024


def _cparams(sem):
    return pltpu.CompilerParams(dimension_semantics=sem, vmem_limit_bytes=VMEM_LIMIT)


def _sigmoid(x):
    return 1.0 / (1.0 + jnp.exp(-x))


def _layer_norm(x, g, b):
    mu = jnp.mean(x, axis=-1, keepdims=True)
    xc = x - mu
    var = jnp.mean(xc * xc, axis=-1, keepdims=True)
    return xc * lax.rsqrt(var + LN_EPS) * g + b


def _rms_norm(x, g):
    return x * lax.rsqrt(jnp.mean(x * x, axis=-1, keepdims=True) + RMS_EPS) * g


def _dot(a, b):
    return jnp.dot(a, b, preferred_element_type=F32)


def _dot_nt(a, b):
    return lax.dot_general(a, b, (((1,), (1,)), ((), ())), preferred_element_type=F32)


def _dot_tn(a, b):
    return lax.dot_general(a, b, (((0,), (0,)), ((), ())), preferred_element_type=F32)


def _ln_body(x_ref, g_ref, b_ref, of_ref, ob_ref):
    y = _layer_norm(x_ref[...], g_ref[...], b_ref[...])
    of_ref[...] = y
    ob_ref[...] = y.astype(BF16)


def layer_norm_entry(x, g, b, tm=512):
    t, d = x.shape
    return pl.pallas_call(
        _ln_body,
        grid=(t // tm,),
        in_specs=[pl.BlockSpec((tm, d), lambda i: (i, 0)),
                  pl.BlockSpec((1, d), lambda i: (0, 0)),
                  pl.BlockSpec((1, d), lambda i: (0, 0))],
        out_specs=[pl.BlockSpec((tm, d), lambda i: (i, 0)),
                   pl.BlockSpec((tm, d), lambda i: (i, 0))],
        out_shape=[jax.ShapeDtypeStruct((t, d), F32), jax.ShapeDtypeStruct((t, d), BF16)],
        compiler_params=_cparams(("parallel",)),
        name="ln_entry",
    )(x, g.reshape(1, d), b.reshape(1, d))


def _inproj_body(x_ref, w_ref, ws_ref, h_ref, hs_ref):
    x = x_ref[...]
    h_ref[...] = _dot(x, w_ref[...]).astype(BF16)

    @pl.when(pl.program_id(1) == 0)
    def _():
        hs_ref[...] = _dot(x, ws_ref[...])


def in_projection(xb, w_main, w_small, tm=4096, tn=512):
    t, d = xb.shape
    n = w_main.shape[1]
    tm = min(tm, t)
    return pl.pallas_call(
        _inproj_body,
        grid=(t // tm, n // tn),
        in_specs=[pl.BlockSpec((tm, d), lambda i, j: (i, 0)),
                  pl.BlockSpec((d, tn), lambda i, j: (0, j)),
                  pl.BlockSpec((d, LANES), lambda i, j: (0, 0))],
        out_specs=[pl.BlockSpec((tm, tn), lambda i, j: (i, j)),
                   pl.BlockSpec((tm, LANES), lambda i, j: (i, 0))],
        out_shape=[jax.ShapeDtypeStruct((t, n), BF16), jax.ShapeDtypeStruct((t, LANES), F32)],
        compiler_params=_cparams(("parallel", "arbitrary")),
        name="in_proj",
    )(xb, w_main, w_small)


def _softplus(x):
    return jnp.maximum(x, 0.0) + jnp.log1p(jnp.exp(-jnp.abs(x)))


def _soft_cap(x):
    return GATE_CAP * jnp.tanh(x / GATE_CAP)


def _log_sigmoid(x):
    return jnp.minimum(x, 0.0) - jnp.log1p(jnp.exp(-jnp.abs(x)))


def _group_constants():
    r = jnp.arange(GROUP, dtype=jnp.int32)[:, None]
    c = jnp.arange(GROUP, dtype=jnp.int32)[None, :]
    same = (r // CHUNK) == (c // CHUNK)
    neg = jnp.where(same & (r >= c), 0.0, -jnp.inf).astype(F32)
    strict = (same & (r > c)).astype(F32)
    eye = (r == c).astype(F32)
    upper = (same & (r <= c)).astype(BF16)
    last = (r == (c // CHUNK) * CHUNK + CHUNK - 1).astype(BF16)
    return jnp.stack([neg, strict, eye]), jnp.stack([upper, last, eye.astype(BF16)])


def _split3(x):
    hi = x.astype(BF16)
    r1 = x - hi.astype(F32)
    mid = r1.astype(BF16)
    return hi, mid, (r1 - mid.astype(F32)).astype(BF16)


def _rows_times(rows8, mat_b, terms):
    rows = jnp.concatenate([rows8, jnp.zeros_like(rows8)], axis=0)
    parts = _split3(rows)[:terms]
    out = _dot(parts[0], mat_b)
    for p in parts[1:]:
        out = out + _dot(p, mat_b)
    return out[0:8]


def _rows_to_cols(stack, eye_b):
    parts = _split3(stack)
    out = _dot_nt(eye_b, parts[0])
    for p in parts[1:]:
        out = out + _dot_nt(eye_b, p)
    return out


def _conv_silu(x_ref, w, e_ref):
    lb = x_ref.shape[0]
    e_ref[8:, :] = x_ref[...].astype(F32)
    y = w[0:1, :] * e_ref[pl.ds(8 - CONV_WIDTH + 1, lb), :]
    for j in range(1, CONV_WIDTH):
        y = y + w[j:j + 1, :] * e_ref[pl.ds(8 - CONV_WIDTH + 1 + j, lb), :]
    e_ref[0:8, :] = e_ref[lb:lb + 8, :]
    return y * _sigmoid(y)


def _recurrent_body(gq_ref, gk_ref, gv_ref, gz_ref, cw_ref, mq_ref, mk_ref, mv_ref, mo_ref, smt_ref,
                    gpg_ref, gpm_ref, ngg_ref, ngm_ref, fm_ref, bm_ref, yg_ref, ym_ref,
                    state_ref, eq_ref, ek_ref, ev_ref, c_ref, n_ref, m_ref):
    lb = gq_ref.shape[0]
    n_chunks = lb // CHUNK
    hds = range(HEADS)
    lanes = [slice(hd * HEAD_DIM, (hd + 1) * HEAD_DIM) for hd in hds]

    @pl.when(pl.program_id(1) == 0)
    def _():
        state_ref[...] = jnp.zeros_like(state_ref)
        for e_ref in (eq_ref, ek_ref, ev_ref):
            e_ref[0:8, :] = jnp.zeros((8, QK_WIDTH), F32)
        c_ref[...] = jnp.zeros_like(c_ref)
        n_ref[...] = jnp.zeros_like(n_ref)
        m_ref[...] = jnp.zeros_like(m_ref)

    neg, strict01, eye = fm_ref[0], fm_ref[1], fm_ref[2]
    up_b, last_b, eye_b = bm_ref[0], bm_ref[1], bm_ref[2]
    cw = cw_ref[...]

    mq_all = mq_ref[...]
    mk_all = mk_ref[...].astype(F32) * (HEAD_DIM ** -0.5)
    mq = [mq_all[:, lanes[hd]] for hd in hds]
    mk = [mk_all[:, lanes[hd]] for hd in hds]
    m_qk = [_dot_nt(mq[hd], mk[hd].astype(BF16)) for hd in hds]
    q_all = _conv_silu(gq_ref, cw[:, 0:QK_WIDTH], eq_ref)
    capped = _soft_cap(smt_ref[2 * HEADS:4 * HEADS, :] + gpm_ref[...][0:8, 0:1])
    i8 = capped
    f8 = pltpu.roll(_log_sigmoid(capped), HEADS, axis=0)
    bc8 = _rows_times(f8, up_b, 2)
    k_all = _conv_silu(gk_ref, cw[:, QK_WIDTH:2 * QK_WIDTH], ek_ref)
    bl8 = _rows_times(bc8, last_b, 3)
    lkw8 = bl8 - bc8 + i8
    v_all = _conv_silu(gv_ref, cw[:, 2 * QK_WIDTH:3 * QK_WIDTH], ev_ref)
    chunk_id = lax.broadcasted_iota(jnp.int32, (8, lb), 1) // CHUNK
    m_st = m_ref[...][:, 0:1]
    m_prev_row = jnp.zeros((8, lb), F32)
    m_next_row = jnp.zeros((8, lb), F32)
    carry_decay = []
    for c in range(n_chunks):
        in_c = chunk_id == c
        b_last = bl8[:, c * CHUNK:c * CHUNK + 1]
        m_new = jnp.maximum(b_last + m_st, jnp.max(jnp.where(in_c, lkw8, -jnp.inf), axis=1, keepdims=True))
        carry_decay.append(jnp.exp(b_last + m_st - m_new))
        m_prev_row = jnp.where(in_c, m_st, m_prev_row)
        m_next_row = jnp.where(in_c, m_new, m_next_row)
        m_st = m_new
    m_ref[...] = jnp.broadcast_to(m_st, m_ref.shape)
    gp = gpg_ref[...]
    g8 = -jnp.exp(gp[0:8, 0:1]) * _softplus(smt_ref[0:8, :] + gp[8:16, 0:1])
    beta8 = _sigmoid(smt_ref[HEADS:HEADS + 8, :])
    gc8 = _rows_times(g8, up_b, 2)
    m_stack = jnp.concatenate([bc8, bc8 + m_prev_row, jnp.exp(lkw8 - m_next_row),
                               jnp.zeros((LANES - 24, lb), F32)], axis=0)
    m_cols = _rows_to_cols(m_stack, eye_b)
    gl8 = _rows_times(gc8, last_b, 3)
    egc8 = jnp.exp(gc8)
    tail8 = jnp.exp(gl8 - gc8)
    elast8 = jnp.exp(gl8)
    g_stack = jnp.concatenate([gc8, beta8, egc8, tail8, beta8 * egc8,
                               jnp.zeros((LANES - 40, lb), F32)], axis=0)
    g_cols = _rows_to_cols(g_stack, eye_b)
    mcol = lambda j, hd: m_cols[:, 8 * j + hd:8 * j + hd + 1]
    gcol = lambda j, hd: g_cols[:, 8 * j + hd:8 * j + hd + 1]

    qg = [q_all[:, lanes[hd]] for hd in hds]
    kg = [k_all[:, lanes[hd]] for hd in hds]
    qg = [x * lax.rsqrt(jnp.sum(x * x, axis=-1, keepdims=True) + RMS_EPS) * (HEAD_DIM ** -0.5) for x in qg]
    kg = [x * lax.rsqrt(jnp.sum(x * x, axis=-1, keepdims=True) + RMS_EPS) for x in kg]
    kg_b = [x.astype(BF16) for x in kg]
    mv_all = mv_ref[...]
    mv = [mv_all[:, lanes[hd]] for hd in hds]
    log_d = [mcol(0, hd) - bc8[hd:hd + 1, :] + i8[hd:hd + 1, :] + neg for hd in hds]
    m_t = [jnp.maximum(mcol(1, hd), jnp.max(log_d[hd], axis=-1, keepdims=True)) for hd in hds]
    decay = [jnp.exp(gcol(0, hd) - gc8[hd:hd + 1, :] + neg) for hd in hds]
    a_mat = [_dot_nt((kg[hd] * gcol(1, hd)).astype(BF16), kg_b[hd]) * decay[hd] * strict01 for hd in hds]
    w_inter = [jnp.exp(mcol(1, hd) - m_t[hd]) for hd in hds]
    s = [m_qk[hd] * jnp.exp(log_d[hd] - m_t[hd]) for hd in hds]
    pw = [-a for a in a_mat]
    t_mat = [eye + p for p in pw]
    ones = jnp.ones((lb, HEAD_DIM), BF16)
    s_vx = None
    floor = kw = kw_b = None
    for it in range(5):
        pw_b = [p.astype(BF16) for p in pw]
        pw = [_dot(p, p) for p in pw_b]
        t_mat = [t_mat[hd] + _dot(t_mat[hd].astype(BF16), pw[hd].astype(BF16)) for hd in hds]
        if it == 0:
            s_vx = [_dot(s[hd].astype(BF16), jnp.concatenate([mv[hd], ones], axis=1)) for hd in hds]
        if it == 1:
            floor = [jnp.exp(-m_t[hd]) for hd in hds]
            kw = [mk[hd] * mcol(2, hd) for hd in hds]
            kw_b = [x.astype(BF16) for x in kw]
    s_v = [x[:, :HEAD_DIM] for x in s_vx]
    s_sum = [x[:, HEAD_DIM:HEAD_DIM + 1] for x in s_vx]
    rhs = [jnp.concatenate([(v_all[:, lanes[hd]] * gcol(1, hd)).astype(BF16),
                            (kg[hd] * gcol(4, hd)).astype(BF16)], axis=1) for hd in hds]
    uw = [_dot(t_mat[hd].astype(BF16), rhs[hd]).astype(BF16) for hd in hds]
    qk = [(_dot_nt(qg[hd].astype(BF16), kg_b[hd]) * decay[hd]).astype(BF16) for hd in hds]
    qk_uw = [_dot(qk[hd], uw[hd]) for hd in hds]
    q_eff = [qg[hd] * gcol(2, hd) - qk_uw[hd][:, HEAD_DIM:] for hd in hds]
    k_tail = [(kg[hd] * gcol(3, hd)).astype(BF16) for hd in hds]
    z_all = gz_ref[...].astype(F32)
    o_all = mo_ref[...].astype(F32)
    ngg = ngg_ref[...]
    ngm = ngm_ref[...]
    state = [state_ref[hd] for hd in hds]
    c_st = [c_ref[hd] for hd in hds]
    n_all = n_ref[...]
    n_st = [n_all[hd:hd + 1, :] for hd in hds]
    for c in range(n_chunks):
        sl = slice(c * CHUNK, (c + 1) * CHUNK)
        kt_uw = [_dot_tn(k_tail[hd][sl], uw[hd][sl]) for hd in hds]
        lhs = [jnp.concatenate([q_eff[hd][sl], kt_uw[hd][:, HEAD_DIM:]], axis=0).astype(BF16) for hd in hds]
        res = [_dot(lhs[hd], state[hd].astype(BF16)) for hd in hds]
        q_c = [_dot(mq[hd][sl], c_st[hd].astype(BF16)) for hd in hds]
        q_n = [jnp.sum(mq[hd][sl].astype(F32) * n_st[hd], axis=-1, keepdims=True) for hd in hds]
        state = [state[hd] * elast8[hd:hd + 1, c * CHUNK:c * CHUNK + 1] - res[hd][CHUNK:]
                 + kt_uw[hd][:, :HEAD_DIM] for hd in hds]
        cd = [carry_decay[c][hd:hd + 1, :] for hd in hds]
        c_st = [cd[hd] * c_st[hd] + _dot_tn(kw_b[hd][sl], mv[hd][sl]) for hd in hds]
        n_st = [cd[hd] * n_st[hd] + jnp.sum(kw[hd][sl], axis=0, keepdims=True) for hd in hds]
        for hd in hds:
            out = res[hd][:CHUNK] + qk_uw[hd][sl, :HEAD_DIM]
            zc = z_all[sl, lanes[hd]]
            yg_ref[sl, lanes[hd]] = (_rms_norm(out, ngg) * (zc * _sigmoid(zc))).astype(yg_ref.dtype)
        for hd in hds:
            num = w_inter[hd][sl] * q_c[hd] + s_v[hd][sl]
            den = w_inter[hd][sl] * q_n[hd] + s_sum[hd][sl]
            hid = num / jnp.maximum(jnp.abs(den), floor[hd][sl])
            ym_ref[sl, lanes[hd]] = (_sigmoid(o_all[sl, lanes[hd]])
                                     * _rms_norm(hid, ngm[:, lanes[hd]])).astype(ym_ref.dtype)
    for hd in hds:
        state_ref[hd] = state[hd]
        c_ref[hd] = c_st[hd]
        n_ref[hd:hd + 1, :] = n_st[hd]


def _gate_params(first, second):
    out = jnp.zeros((16, LANES), F32)
    out = out.at[0:first.shape[0], :].set(jnp.broadcast_to(first.astype(F32)[:, None], (first.shape[0], LANES)))
    if second is not None:
        out = out.at[8:8 + second.shape[0], :].set(
            jnp.broadcast_to(second.astype(F32)[:, None], (second.shape[0], LANES)))
    return out


def recurrent_mixers(h, small_t, conv_w, a_log, dt_bias, gdn_norm, gate_bias, mlstm_norm, fmasks, bmasks,
                     bsz, seq):
    t = h.shape[0]
    lb = GROUP
    nb = seq // lb
    wide = QK_WIDTH
    gdn0 = COL_GDN * LANES // wide
    mls0 = COL_MLSTM * LANES // wide

    def col(j):
        return pl.BlockSpec((lb, wide), lambda b, s: (b * nb + s, j))

    fixed2 = lambda b, s: (0, 0)
    fixed3 = lambda b, s: (0, 0, 0)
    out_spec = pl.BlockSpec((lb, wide), lambda b, s: (b * nb + s, 0))
    return pl.pallas_call(
        _recurrent_body,
        grid=(bsz, nb),
        in_specs=[col(gdn0), col(gdn0 + 1), col(gdn0 + 2), col(gdn0 + 3),
                  pl.BlockSpec((CONV_WIDTH, 3 * wide), fixed2),
                  col(mls0), col(mls0 + 1), col(mls0 + 2), col(mls0 + 3),
                  pl.BlockSpec((4 * HEADS, lb), lambda b, s: (0, b * nb + s)),
                  pl.BlockSpec((16, LANES), fixed2),
                  pl.BlockSpec((16, LANES), fixed2),
                  pl.BlockSpec((1, HEAD_DIM), fixed2),
                  pl.BlockSpec((1, wide), fixed2),
                  pl.BlockSpec((3, GROUP, GROUP), fixed3),
                  pl.BlockSpec((3, GROUP, GROUP), fixed3)],
        out_specs=[out_spec, out_spec],
        out_shape=[jax.ShapeDtypeStruct((t, wide), BF16), jax.ShapeDtypeStruct((t, wide), BF16)],
        scratch_shapes=[pltpu.VMEM((HEADS, HEAD_DIM, HEAD_DIM), F32),
                        pltpu.VMEM((lb + 8, wide), F32),
                        pltpu.VMEM((lb + 8, wide), F32),
                        pltpu.VMEM((lb + 8, wide), F32),
                        pltpu.VMEM((HEADS, HEAD_DIM, HEAD_DIM), F32),
                        pltpu.VMEM((8, HEAD_DIM), F32),
                        pltpu.VMEM((8, LANES), F32)],
        compiler_params=_cparams(("parallel", "arbitrary")),
        name="recurrent_mixers",
    )(h, h, h, h, conv_w, h, h, h, h, small_t, _gate_params(a_log, dt_bias), _gate_params(gate_bias, None),
      gdn_norm.reshape(1, HEAD_DIM), mlstm_norm.reshape(1, wide), fmasks, bmasks)


def _rope_table_body(pos_ref, freq_ref, sign_ref, cc_ref, ss_ref):
    ang = pos_ref[...] * freq_ref[...]
    sign = sign_ref[...]
    cc_ref[...] = jnp.cos(ang) * jnp.abs(sign)
    ss_ref[...] = jnp.sin(ang) * sign


def rope_tables(positions, tm=512):
    t = positions.size
    half = MLA_ROPE // 2
    inv_freq = 1.0 / (ROPE_THETA ** (jnp.arange(0, MLA_ROPE, 2, dtype=F32) / MLA_ROPE))
    zeros = jnp.zeros((LANES - MLA_ROPE,), F32)
    freq = jnp.concatenate([inv_freq, inv_freq, zeros]).reshape(1, LANES)
    sign = jnp.concatenate([-jnp.ones((half,), F32), jnp.ones((half,), F32), zeros]).reshape(1, LANES)
    return pl.pallas_call(
        _rope_table_body,
        grid=(t // tm,),
        in_specs=[pl.BlockSpec((tm, 1), lambda i: (i, 0)),
                  pl.BlockSpec((1, LANES), lambda i: (0, 0)),
                  pl.BlockSpec((1, LANES), lambda i: (0, 0))],
        out_specs=[pl.BlockSpec((tm, LANES), lambda i: (i, 0)),
                   pl.BlockSpec((tm, LANES), lambda i: (i, 0))],
        out_shape=[jax.ShapeDtypeStruct((t, LANES), F32), jax.ShapeDtypeStruct((t, LANES), F32)],
        compiler_params=_cparams(("parallel",)),
        name="rope_tables",
    )(positions.astype(F32).reshape(t, 1), freq, sign)


def _mla_pre_body(h_ref, cc_ref, ss_ref, qg_ref, kvg_ref, wqa_ref, wqb_ref, wkv_ref,
                  q_ref, kn_ref, kr_ref, v_ref):
    hblk = h_ref[...].astype(F32)
    cc = cc_ref[...]
    ss = ss_ref[...]
    cq = _rms_norm(hblk[:, :MLA_Q_LORA], qg_ref[...]).astype(BF16)
    ckv = _rms_norm(hblk[:, MLA_Q_LORA:MLA_Q_LORA + MLA_KV_LORA], kvg_ref[...]).astype(BF16)
    off = MLA_Q_LORA + MLA_KV_LORA
    kr_ref[...] = (hblk[:, off:off + LANES] * cc + hblk[:, off + LANES:off + 2 * LANES] * ss).astype(BF16)
    kv = _dot(ckv, wkv_ref[...])
    kn_ref[...] = kv[:, :QK_WIDTH].astype(BF16)
    v_ref[...] = kv[:, QK_WIDTH:].astype(BF16)
    qa = _dot(cq, wqa_ref[...])
    qb = _dot(cq, wqb_ref[...])
    scale = (HEAD_DIM + MLA_ROPE) ** -0.5 * LOG2_E
    for hh in range(HEADS):
        base = 2 * HEAD_DIM * hh
        q_ref[:, base:base + HEAD_DIM] = (qa[:, base:base + HEAD_DIM] * scale).astype(BF16)
        rope = qa[:, base + HEAD_DIM:base + 2 * HEAD_DIM] * cc + qb[:, hh * LANES:(hh + 1) * LANES] * ss
        q_ref[:, base + HEAD_DIM:base + 2 * HEAD_DIM] = (rope * scale).astype(BF16)


def mla_prepare(h, cc, ss, q_norm_g, kv_norm_g, wq_a, wq_b, wkv, tm=512):
    t = h.shape[0]
    row = lambda i: (i, 0)
    fixed = lambda i: (0, 0)
    return pl.pallas_call(
        _mla_pre_body,
        grid=(t // tm,),
        in_specs=[pl.BlockSpec((tm, 1024), row),
                  pl.BlockSpec((tm, LANES), row),
                  pl.BlockSpec((tm, LANES), row),
                  pl.BlockSpec((1, MLA_Q_LORA), fixed),
                  pl.BlockSpec((1, MLA_KV_LORA), fixed),
                  pl.BlockSpec(wq_a.shape, fixed),
                  pl.BlockSpec(wq_b.shape, fixed),
                  pl.BlockSpec(wkv.shape, fixed)],
        out_specs=[pl.BlockSpec((tm, 2 * QK_WIDTH), row),
                   pl.BlockSpec((tm, QK_WIDTH), row),
                   pl.BlockSpec((tm, LANES), row),
                   pl.BlockSpec((tm, QK_WIDTH), row)],
        out_shape=[jax.ShapeDtypeStruct((t, 2 * QK_WIDTH), BF16),
                   jax.ShapeDtypeStruct((t, QK_WIDTH), BF16),
                   jax.ShapeDtypeStruct((t, LANES), BF16),
                   jax.ShapeDtypeStruct((t, QK_WIDTH), BF16)],
        compiler_params=_cparams(("parallel",)),
        name="mla_prepare",
    )(h, cc, ss, q_norm_g.reshape(1, -1), kv_norm_g.reshape(1, -1), wq_a, wq_b, wkv)


def _attn_body(q_ref, kn_ref, kr_ref, v_ref, o_ref, *, tk, n_heads):
    qi = pl.program_id(2)
    tq = q_ref.shape[0]
    hds = range(n_heads)
    lanes = [slice(hd * HEAD_DIM, (hd + 1) * HEAD_DIM) for hd in hds]
    q = [q_ref[:, 2 * hd * HEAD_DIM:2 * (hd + 1) * HEAD_DIM] for hd in hds]

    def step(j, carry, masked):
        m, l, acc = carry
        start = pl.multiple_of(j * tk, tk)
        kr = kr_ref[pl.ds(start, tk), :]
        s = [_dot_nt(q[hd], jnp.concatenate([kn_ref[pl.ds(start, tk), lanes[hd]], kr], axis=1)) for hd in hds]
        if masked:
            r = lax.broadcasted_iota(jnp.int32, (tq, tk), 0)
            c = lax.broadcasted_iota(jnp.int32, (tq, tk), 1)
            s = [jnp.where(r >= c, x, -jnp.inf) for x in s]
        m_new = [jnp.maximum(m[hd], jnp.max(s[hd], axis=-1, keepdims=True)) for hd in hds]
        alpha = [jnp.exp2(m[hd] - m_new[hd]) for hd in hds]
        p = [jnp.exp2(s[hd] - m_new[hd]) for hd in hds]
        l_new = [alpha[hd] * l[hd] + jnp.sum(p[hd], axis=-1, keepdims=True) for hd in hds]
        p_b = [x.astype(BF16) for x in p]
        acc = [alpha[hd] * acc[hd] + _dot(p_b[hd], v_ref[pl.ds(start, tk), lanes[hd]]) for hd in hds]
        return tuple(m_new), tuple(l_new), tuple(acc)

    init = (tuple(jnp.full((tq, 1), -jnp.inf, F32) for _ in hds),
            tuple(jnp.zeros((tq, 1), F32) for _ in hds),
            tuple(jnp.zeros((tq, HEAD_DIM), F32) for _ in hds))
    carry = lax.fori_loop(0, qi, lambda j, cr: step(j, cr, False), init)
    _, l, acc = step(qi, carry, True)
    for hd in hds:
        o_ref[:, lanes[hd]] = (acc[hd] / l[hd]).astype(o_ref.dtype)


def latent_attention(q, kn, kr, v, bsz, seq, tq=512, n_heads=2):
    t = q.shape[0]
    nq = seq // tq
    wide = n_heads * HEAD_DIM
    return pl.pallas_call(
        functools.partial(_attn_body, tk=tq, n_heads=n_heads),
        grid=(bsz, HEADS // n_heads, nq),
        in_specs=[pl.BlockSpec((tq, 2 * wide), lambda b, hh, i: (b * nq + i, hh)),
                  pl.BlockSpec((seq, wide), lambda b, hh, i: (b, hh)),
                  pl.BlockSpec((seq, LANES), lambda b, hh, i: (b, 0)),
                  pl.BlockSpec((seq, wide), lambda b, hh, i: (b, hh))],
        out_specs=pl.BlockSpec((tq, wide), lambda b, hh, i: (b * nq + i, hh)),
        out_shape=jax.ShapeDtypeStruct((t, QK_WIDTH), BF16),
        compiler_params=_cparams(("parallel", "parallel", "arbitrary")),
        name="latent_attention",
    )(q, kn, kr, v)


def _top2_sum(a, b, c, d):
    hi1, lo1 = jnp.maximum(a, b), jnp.minimum(a, b)
    hi2, lo2 = jnp.maximum(c, d), jnp.minimum(c, d)
    return jnp.maximum(hi1, hi2) + jnp.maximum(jnp.minimum(hi1, hi2), jnp.maximum(lo1, lo2))


def _merge_route_body(yg_ref, ym_ref, ya_ref, g0_ref, g1_ref, g2_ref, gb_ref, pg_ref, pm_ref, pa_ref,
                      wo_ref, x_ref, ln_g_ref, ln_b_ref, rw_ref, rb_ref, su_ref, eye_ref,
                      xw_ref, grp_ref, rank_ref, cnt_ref, carry_ref, yprev_ref):
    step = pl.program_id(0)

    @pl.when(step == 0)
    def _():
        carry_ref[...] = jnp.zeros_like(carry_ref)
        yprev_ref[...] = jnp.zeros_like(yprev_ref)

    gb = gb_ref[...]

    def branch(y_ref, p_ref, g_ref, idx):
        gate = _sigmoid(g_ref[...].astype(F32) + gb[:, idx * D_MODEL:(idx + 1) * D_MODEL])
        return gate * _dot(y_ref[...], p_ref[...])

    y = yprev_ref[...]
    xw_ref[:, 0:D_MODEL] = y
    y_hi = y.astype(BF16)
    y_lo = (y - y_hi.astype(F32)).astype(BF16)
    rw_hi, rw_lo = rw_ref[0], rw_ref[1]
    logits_tok = _dot(y_hi, rw_hi) + (_dot(y_lo, rw_hi) + _dot(y_hi, rw_lo))
    merged = branch(yg_ref, pg_ref, g0_ref, 0)
    logits = logits_tok.T[0:N_EXPERTS, :]
    scores = _sigmoid(logits)
    biased = scores + rb_ref[...][:, :1]
    tm = logits.shape[1]
    brow = [biased[e:e + 1, :] for e in range(N_EXPERTS)]
    srow = [scores[e:e + 1, :] for e in range(N_EXPERTS)]
    best = _top2_sum(*brow[0:EXPERTS_PER_GROUP])
    grp = jnp.zeros((1, tm), jnp.int32)
    for g in range(1, N_GROUPS):
        gs = _top2_sum(*brow[g * EXPERTS_PER_GROUP:(g + 1) * EXPERTS_PER_GROUP])
        upd = gs > best
        best = jnp.where(upd, gs, best)
        grp = jnp.where(upd, g, grp)

    def in_group(rows, j):
        out = rows[j]
        for g in range(1, N_GROUPS):
            out = jnp.where(grp == g, rows[g * EXPERTS_PER_GROUP + j], out)
        return out

    ib = [in_group(brow, j) for j in range(EXPERTS_PER_GROUP)]
    isc = [in_group(srow, j) for j in range(EXPERTS_PER_GROUP)]
    v1, i1, s1 = ib[0], jnp.zeros((1, tm), jnp.int32), isc[0]
    for j in range(1, EXPERTS_PER_GROUP):
        upd = ib[j] > v1
        v1 = jnp.where(upd, ib[j], v1)
        i1 = jnp.where(upd, j, i1)
        s1 = jnp.where(upd, isc[j], s1)
    v2 = jnp.full((1, tm), -jnp.inf, F32)
    i2 = jnp.zeros((1, tm), jnp.int32)
    s2 = jnp.zeros((1, tm), F32)
    for j in range(EXPERTS_PER_GROUP):
        upd = jnp.logical_and(i1 != j, ib[j] > v2)
        v2 = jnp.where(upd, ib[j], v2)
        i2 = jnp.where(upd, j, i2)
        s2 = jnp.where(upd, isc[j], s2)
    total = s1 + s2
    merged = merged + branch(ym_ref, pm_ref, g1_ref, 1)
    wrow = [jnp.where(i1 == j, s1 / total, jnp.where(i2 == j, s2 / total, 0.0)) for j in range(EXPERTS_PER_GROUP)]
    row_id = lax.broadcasted_iota(jnp.int32, (8, tm), 0)
    w8 = jnp.zeros((8, tm), F32)
    for j in range(EXPERTS_PER_GROUP):
        w8 = jnp.where(row_id == j, wrow[j], w8)
    merged = merged + branch(ya_ref, pa_ref, g2_ref, 2)
    stack = jnp.concatenate([w8, jnp.zeros((LANES - 8, tm), F32)], axis=0)
    xw_ref[:, D_MODEL:D_MODEL + LANES] = _rows_to_cols(stack, eye_ref[...])
    mix = _dot(merged.astype(BF16), wo_ref[...])
    grp_ref[...] = grp
    grow = lax.broadcasted_iota(jnp.int32, (N_EXPERTS, tm), 0)
    onehot = jnp.where(grow == grp, 1.0, 0.0).astype(F32)
    before = _dot(onehot.astype(BF16), su_ref[...]) + carry_ref[...][:, :1]
    rank_ref[...] = jnp.sum(onehot * before, axis=0, keepdims=True).astype(jnp.int32)
    counted = jnp.where(step > 0, 1.0, 0.0)
    carry = carry_ref[...] + counted * jnp.sum(onehot, axis=1, keepdims=True)
    carry_ref[...] = carry
    cnt_ref[...] = carry.astype(jnp.int32)
    yprev_ref[...] = _layer_norm(DEEPNORM_ALPHA * x_ref[...] + mix, ln_g_ref[...], ln_b_ref[...])


def merge_and_route(y_gdn, y_mlstm, y_mla, h, gate_bias, p_gdn, p_mlstm, p_mla, w_out, x, ln_g, ln_b,
                    router_w, router_bias, tm=512):
    t, d = x.shape
    nt = t // tm
    cur = lambda i: jnp.minimum(i, nt - 1)
    prev = lambda i: jnp.maximum(i - 1, 0)
    row = lambda i: (cur(i), 0)
    fixed = lambda i: (0, 0)
    tok = lambda i: (0, prev(i))
    ybs = pl.BlockSpec((tm, QK_WIDTH), row)
    pbs = pl.BlockSpec((QK_WIDTH, d), fixed)
    strict_upper = jnp.triu(jnp.ones((tm, tm), BF16), k=1)
    eye_b = jnp.eye(tm, dtype=BF16)
    rw = jnp.pad(router_w.astype(F32), ((0, 0), (0, LANES - N_EXPERTS)))
    rw_hi = rw.astype(BF16)
    rw_split = jnp.stack([rw_hi, (rw - rw_hi.astype(F32)).astype(BF16)])
    return pl.pallas_call(
        _merge_route_body,
        grid=(nt + 1,),
        in_specs=[ybs, ybs, ybs,
                  pl.BlockSpec((tm, d), lambda i: (cur(i), COL_GATE)),
                  pl.BlockSpec((tm, d), lambda i: (cur(i), COL_GATE + 1)),
                  pl.BlockSpec((tm, d), lambda i: (cur(i), COL_GATE + 2)),
                  pl.BlockSpec((1, 3 * d), fixed),
                  pbs, pbs, pbs,
                  pl.BlockSpec((d, d), fixed),
                  pl.BlockSpec((tm, d), row),
                  pl.BlockSpec((1, d), fixed),
                  pl.BlockSpec((1, d), fixed),
                  pl.BlockSpec((2, d, LANES), lambda i: (0, 0, 0)),
                  pl.BlockSpec((N_EXPERTS, LANES), fixed),
                  pl.BlockSpec((tm, tm), fixed),
                  pl.BlockSpec((tm, tm), fixed)],
        out_specs=[pl.BlockSpec((tm, d + LANES), lambda i: (prev(i), 0)),
                   pl.BlockSpec((1, tm), tok), pl.BlockSpec((1, tm), tok),
                   pl.BlockSpec((N_EXPERTS, LANES), fixed)],
        out_shape=[jax.ShapeDtypeStruct((t, d + LANES), F32),
                   jax.ShapeDtypeStruct((1, t), jnp.int32), jax.ShapeDtypeStruct((1, t), jnp.int32),
                   jax.ShapeDtypeStruct((N_EXPERTS, LANES), jnp.int32)],
        scratch_shapes=[pltpu.VMEM((N_EXPERTS, LANES), F32), pltpu.VMEM((tm, d), F32)],
        compiler_params=_cparams(("arbitrary",)),
        name="merge_route",
    )(y_gdn, y_mlstm, y_mla, h, h, h, gate_bias.reshape(1, 3 * d), p_gdn, p_mlstm, p_mla, w_out, x,
      ln_g.reshape(1, d), ln_b.reshape(1, d), rw_split,
      jnp.broadcast_to(router_bias.reshape(N_EXPERTS, 1), (N_EXPERTS, LANES)), strict_upper, eye_b)


def _gather_copy(x_hbm, buf, sem, slot, row, tok):
    return pltpu.make_async_copy(x_hbm.at[pl.ds(tok, 1), :], buf.at[slot, pl.ds(row, 1), :], sem.at[slot])


def _scatter_copy(ybuf, y_hbm, sem, slot, row, dst):
    return pltpu.make_async_copy(ybuf.at[slot, pl.ds(row, 1), :], y_hbm.at[pl.ds(dst, 1), :], sem.at[slot])


def _group_ffn_body(bg_ref, src_ref, dst_ref, x_hbm, w1_ref, w3_ref, w2_ref, y_hbm,
                    buf, ybuf, gsem, ssem, *, spare_row):
    i = pl.program_id(0)
    n = pl.num_programs(0)

    def wait_rows(make):
        def one(r, carry):
            make(r).wait()
            return carry

        lax.fori_loop(0, MOE_BLOCK, one, 0, unroll=8)

    @pl.when(i == 0)
    def _():
        def one(r, carry):
            _gather_copy(x_hbm, buf, gsem, 0, r, src_ref[r]).start()
            return carry

        lax.fori_loop(0, MOE_BLOCK, one, 0, unroll=8)
        ybuf[1] = jnp.zeros((MOE_BLOCK, D_MODEL), F32)

    def step(slot):
        other = 1 - slot
        wait_rows(lambda r: _gather_copy(x_hbm, buf, gsem, slot, r, 0))
        nxt = jnp.minimum(i + 1, n - 1) * MOE_BLOCK
        prv = jnp.maximum(i - 1, 0) * MOE_BLOCK
        for r in range(MOE_BLOCK):
            _gather_copy(x_hbm, buf, gsem, other, r, src_ref[nxt + r]).start()
        for r in range(MOE_BLOCK):
            dst = jnp.where(i > 0, dst_ref[prv + r], spare_row + r)
            _scatter_copy(ybuf, y_hbm, ssem, other, r, dst).start()
        xw = buf[slot]
        xb = xw[:, :D_MODEL].astype(BF16)
        acts = []
        for e in range(EXPERTS_PER_GROUP):
            h1 = _dot(xb, w1_ref[e])
            h3 = _dot(xb, w3_ref[e])
            acts.append((h1 * _sigmoid(h1) * h3 * xw[:, D_MODEL + e:D_MODEL + e + 1]).astype(BF16))
        ybuf[slot] = _dot(jnp.concatenate(acts, axis=1), w2_ref[...])
        wait_rows(lambda r: _scatter_copy(ybuf, y_hbm, ssem, other, r, 0))

        @pl.when(i == n - 1)
        def _():
            wait_rows(lambda r: _gather_copy(x_hbm, buf, gsem, other, r, 0))

            def one(r, carry):
                _scatter_copy(ybuf, y_hbm, ssem, slot, r, dst_ref[i * MOE_BLOCK + r]).start()
                return carry

            lax.fori_loop(0, MOE_BLOCK, one, 0, unroll=8)
            wait_rows(lambda r: _scatter_copy(ybuf, y_hbm, ssem, slot, r, 0))

    pl.when(i % 2 == 0)(lambda: step(0))
    pl.when(i % 2 == 1)(lambda: step(1))


def group_ffn(block_group, row_src, row_dst, xw, w1g, w3g, w2g, n_out_rows, spare_row):
    d = D_MODEL
    n_rows = row_src.shape[0]
    n_blocks = n_rows // MOE_BLOCK
    wide = EXPERTS_PER_GROUP * D_EXPERT
    once = pl.Buffered(1)
    grid_spec = pltpu.PrefetchScalarGridSpec(
        num_scalar_prefetch=3,
        grid=(n_blocks,),
        in_specs=[pl.BlockSpec(memory_space=pl.ANY),
                  pl.BlockSpec((EXPERTS_PER_GROUP, d, D_EXPERT), lambda i, bg, rs, rd: (bg[i], 0, 0),
                               pipeline_mode=once),
                  pl.BlockSpec((EXPERTS_PER_GROUP, d, D_EXPERT), lambda i, bg, rs, rd: (bg[i], 0, 0),
                               pipeline_mode=once),
                  pl.BlockSpec((None, wide, d), lambda i, bg, rs, rd: (bg[i], 0, 0), pipeline_mode=once)],
        out_specs=pl.BlockSpec(memory_space=pl.ANY),
        scratch_shapes=[pltpu.VMEM((2, MOE_BLOCK, d + LANES), F32),
                        pltpu.VMEM((2, MOE_BLOCK, d), F32),
                        pltpu.SemaphoreType.DMA((2,)),
                        pltpu.SemaphoreType.DMA((2,))])
    return pl.pallas_call(
        functools.partial(_group_ffn_body, spare_row=spare_row),
        grid_spec=grid_spec,
        out_shape=jax.ShapeDtypeStruct((n_out_rows, d), F32),
        compiler_params=pltpu.CompilerParams(dimension_semantics=("arbitrary",),
                                             vmem_limit_bytes=MOE_VMEM_LIMIT),
        name="group_ffn",
    )(block_group, row_src, row_dst, xw, w1g, w3g, w2g)


def _residual_norm_body(x_ref, y_ref, ln_g_ref, ln_b_ref, of_ref, ob_ref):
    y = _layer_norm(DEEPNORM_ALPHA * x_ref[...] + y_ref[...], ln_g_ref[...], ln_b_ref[...])
    of_ref[...] = y
    ob_ref[...] = y.astype(BF16)


def residual_norm(xw, y_tok, ln_g, ln_b, tm=512):
    t = xw.shape[0]
    d = D_MODEL
    row = lambda i: (i, 0)
    fixed = lambda i: (0, 0)
    return pl.pallas_call(
        _residual_norm_body,
        grid=(t // tm,),
        in_specs=[pl.BlockSpec((tm, d), row), pl.BlockSpec((tm, d), row),
                  pl.BlockSpec((1, d), fixed), pl.BlockSpec((1, d), fixed)],
        out_specs=[pl.BlockSpec((tm, d), row), pl.BlockSpec((tm, d), row)],
        out_shape=[jax.ShapeDtypeStruct((t, d), F32), jax.ShapeDtypeStruct((t, d), BF16)],
        compiler_params=_cparams(("parallel",)),
        name="residual_norm",
    )(xw, y_tok, ln_g.reshape(1, d), ln_b.reshape(1, d))


def grouped_experts(xw, grp, rank, counts, w1, w3, w2, ln_g, ln_b):
    t = xw.shape[0]
    grp = grp.reshape(t)
    counts = counts[:N_GROUPS, 0]
    padded = (counts + MOE_BLOCK - 1) // MOE_BLOCK * MOE_BLOCK
    padded_ends = jnp.cumsum(padded)
    padded_starts = padded_ends - padded
    start_of = jnp.zeros_like(grp)
    for g in range(N_GROUPS):
        start_of = jnp.where(grp == g, padded_starts[g], start_of)
    dest = start_of + rank.reshape(t)
    n_rows = t + N_GROUPS * MOE_BLOCK
    slot_tok = jnp.full((n_rows,), -1, jnp.int32).at[dest].set(jnp.arange(t, dtype=jnp.int32))
    row_ids = jnp.arange(n_rows, dtype=jnp.int32)
    block_start = jnp.arange(n_rows // MOE_BLOCK, dtype=jnp.int32) * MOE_BLOCK
    block_group = jnp.minimum(jnp.sum(block_start[:, None] >= padded_ends[None, :], axis=1),
                              N_GROUPS - 1).astype(jnp.int32)
    pads = padded - counts
    pad_base = jnp.cumsum(pads) - pads - (padded_starts + counts)
    row_group = jnp.repeat(block_group, MOE_BLOCK)
    pad_off = jnp.zeros_like(row_ids)
    for g in range(N_GROUPS):
        pad_off = jnp.where(row_group == g, pad_base[g], pad_off)
    row_src = jnp.maximum(slot_tok, 0)
    row_dst = jnp.where(slot_tok < 0, t + row_ids + pad_off, slot_tok)
    d = D_MODEL
    wide = EXPERTS_PER_GROUP * D_EXPERT
    w2g = w2.reshape(N_GROUPS, wide, d)
    spare_row = n_rows
    y_tok = group_ffn(block_group, row_src, row_dst, xw, w1.astype(BF16), w3.astype(BF16), w2g.astype(BF16),
                      spare_row + MOE_BLOCK, spare_row)
    return residual_norm(xw, y_tok, ln_g, ln_b)


def _split_w_in(w_in):
    sizes = (QK_WIDTH, QK_WIDTH, QK_WIDTH, QK_WIDTH, HEADS, HEADS,
             QK_WIDTH, QK_WIDTH, QK_WIDTH, QK_WIDTH, HEADS, HEADS,
             MLA_Q_LORA, MLA_KV_LORA, MLA_ROPE, 3 * D_MODEL)
    parts, acc = [], 0
    for size in sizes:
        parts.append(w_in[:, acc:acc + size])
        acc += size
    return parts


def _arrange_w_in(w_in):
    (g_q, g_k, g_v, g_z, g_a, g_b, m_q, m_k, m_v, m_o, m_i, m_f, c_q, c_kv, k_rope, gates) = _split_w_in(w_in)
    d = w_in.shape[0]
    half = MLA_ROPE // 2
    pad64 = jnp.zeros((d, LANES - MLA_ROPE), w_in.dtype)
    rope_sw = jnp.concatenate([k_rope[:, half:], k_rope[:, :half]], axis=1)
    main = jnp.concatenate([c_q, c_kv, k_rope, pad64, rope_sw, pad64, jnp.zeros((d, LANES), w_in.dtype),
                            g_q, g_k, g_v, g_z, m_q, m_k, m_v, m_o, gates], axis=1)
    small = jnp.concatenate([g_a, g_b, m_i, m_f, jnp.zeros((d, LANES - 4 * HEADS), w_in.dtype)], axis=1)
    return main.astype(BF16), small.astype(BF16)


def _arrange_mla(w_uq, w_ukv):
    half = MLA_ROPE // 2
    wq = w_uq.reshape(MLA_Q_LORA, HEADS, HEAD_DIM + MLA_ROPE)
    nope, rope = wq[:, :, :HEAD_DIM], wq[:, :, HEAD_DIM:]
    pad = jnp.zeros((MLA_Q_LORA, HEADS, LANES - MLA_ROPE), w_uq.dtype)
    wq_a = jnp.concatenate([nope, rope, pad], axis=2).reshape(MLA_Q_LORA, HEADS * 2 * HEAD_DIM)
    rope_sw = jnp.concatenate([rope[:, :, half:], rope[:, :, :half]], axis=2)
    wq_b = jnp.concatenate([rope_sw, pad], axis=2).reshape(MLA_Q_LORA, HEADS * LANES)
    wkv = w_ukv.reshape(MLA_KV_LORA, HEADS, 2 * HEAD_DIM)
    wkv = jnp.concatenate([wkv[:, :, :HEAD_DIM].reshape(MLA_KV_LORA, QK_WIDTH),
                           wkv[:, :, HEAD_DIM:].reshape(MLA_KV_LORA, QK_WIDTH)], axis=1)
    return wq_a.astype(BF16), wq_b.astype(BF16), wkv.astype(BF16)


def kernel(x, positions, ln_in_g, ln_in_b, w_in, gdn_conv, gdn_a_log, gdn_dt_bias, gdn_norm, mlstm_gate_bias, mlstm_norm, mla_q_norm, mla_kv_norm, mla_w_uq, mla_w_ukv, w_br_gdn, w_br_mlstm, w_br_mla, gate_bias, w_out, ln1_g, ln1_b, router_w, router_bias, moe_w1, moe_w3, moe_w2, ln2_g, ln2_b):
    bsz, seq, d = x.shape
    t = bsz * seq
    xf, xb = layer_norm_entry(x.reshape(t, d), ln_in_g, ln_in_b)
    cc, ss = rope_tables(positions)
    fmasks, bmasks = _group_constants()
    for l in range(DEPTH):
        w_main, w_small = _arrange_w_in(w_in[l])
        h, small = in_projection(xb, w_main, w_small)
        small_t = small[:, :4 * HEADS].T
        y_gdn, y_mlstm = recurrent_mixers(h, small_t, gdn_conv[l], gdn_a_log[l], gdn_dt_bias[l], gdn_norm[l],
                                          mlstm_gate_bias[l], mlstm_norm[l], fmasks, bmasks, bsz, seq)
        wq_a, wq_b, wkv = _arrange_mla(mla_w_uq[l], mla_w_ukv[l])
        q, kn, kr, v = mla_prepare(h, cc, ss, mla_q_norm[l], mla_kv_norm[l], wq_a, wq_b, wkv)
        y_mla = latent_attention(q, kn, kr, v, bsz, seq)
        xw, grp, rank, counts = merge_and_route(
            y_gdn, y_mlstm, y_mla, h, gate_bias[l], w_br_gdn[l].astype(BF16), w_br_mlstm[l].astype(BF16),
            w_br_mla[l].astype(BF16), w_out[l].astype(BF16), xf, ln1_g[l], ln1_b[l], router_w, router_bias)
        xf, xb = grouped_experts(xw, grp, rank, counts, moe_w1[l], moe_w3[l], moe_w2[l], ln2_g[l], ln2_b[l])
    return xf.reshape(bsz, seq, d)
```

```python
import functools

import jax
import jax.numpy as jnp
from jax import lax
from jax.experimental import pallas as pl
from jax.experimental.pallas import tpu as pltpu

F32 = jnp.float32
BF16 = jnp.bfloat16

D_MODEL = 1024
DEPTH = 2
HEADS = 4
HEAD_DIM = 128
CHUNK = 64
CONV_WIDTH = 4
GATE_CAP = 15.0
MLA_ROPE = 64
MLA_Q_LORA = 384
MLA_KV_LORA = 256
ROPE_THETA = 10000.0
N_EXPERTS = 16
N_GROUPS = 4
EXPERTS_PER_GROUP = 4
D_EXPERT = 512
LN_EPS = 1e-5
RMS_EPS = 1e-6
DEEPNORM_ALPHA = (2 * DEPTH) ** 0.25
LOG2_E = 1.4426950408889634

LANES = 128
QK_WIDTH = HEADS * HEAD_DIM
COL_GDN = 8
COL_MLSTM = 24
COL_GATE = 5
GROUP = 256
MOE_BLOCK = 512
VMEM_LIMIT = 48 * 1024 * 1024
MOE_VMEM_LIMIT = 56 * 1024 * 1024


def _cparams(sem):
    return pltpu.CompilerParams(dimension_semantics=sem, vmem_limit_bytes=VMEM_LIMIT)


def _sigmoid(x):
    return 1.0 / (1.0 + jnp.exp(-x))


def _layer_norm(x, g, b):
    mu = jnp.mean(x, axis=-1, keepdims=True)
    xc = x - mu
    var = jnp.mean(xc * xc, axis=-1, keepdims=True)
    return xc * lax.rsqrt(var + LN_EPS) * g + b


def _rms_norm(x, g):
    return x * lax.rsqrt(jnp.mean(x * x, axis=-1, keepdims=True) + RMS_EPS) * g


def _dot(a, b):
    return jnp.dot(a, b, preferred_element_type=F32)


def _dot_nt(a, b):
    return lax.dot_general(a, b, (((1,), (1,)), ((), ())), preferred_element_type=F32)


def _dot_tn(a, b):
    return lax.dot_general(a, b, (((0,), (0,)), ((), ())), preferred_element_type=F32)


def _ln_body(x_ref, g_ref, b_ref, of_ref, ob_ref):
    y = _layer_norm(x_ref[...], g_ref[...], b_ref[...])
    of_ref[...] = y
    ob_ref[...] = y.astype(BF16)


def layer_norm_entry(x, g, b, tm=512):
    t, d = x.shape
    return pl.pallas_call(
        _ln_body,
        grid=(t // tm,),
        in_specs=[pl.BlockSpec((tm, d), lambda i: (i, 0)),
                  pl.BlockSpec((1, d), lambda i: (0, 0)),
                  pl.BlockSpec((1, d), lambda i: (0, 0))],
        out_specs=[pl.BlockSpec((tm, d), lambda i: (i, 0)),
                   pl.BlockSpec((tm, d), lambda i: (i, 0))],
        out_shape=[jax.ShapeDtypeStruct((t, d), F32), jax.ShapeDtypeStruct((t, d), BF16)],
        compiler_params=_cparams(("parallel",)),
        name="ln_entry",
    )(x, g.reshape(1, d), b.reshape(1, d))


def _inproj_body(x_ref, w_ref, ws_ref, h_ref, hs_ref):
    x = x_ref[...]
    h_ref[...] = _dot(x, w_ref[...]).astype(BF16)

    @pl.when(pl.program_id(1) == 0)
    def _():
        hs_ref[...] = _dot(x, ws_ref[...])


def in_projection(xb, w_main, w_small, tm=4096, tn=512):
    t, d = xb.shape
    n = w_main.shape[1]
    tm = min(tm, t)
    return pl.pallas_call(
        _inproj_body,
        grid=(t // tm, n // tn),
        in_specs=[pl.BlockSpec((tm, d), lambda i, j: (i, 0)),
                  pl.BlockSpec((d, tn), lambda i, j: (0, j)),
                  pl.BlockSpec((d, LANES), lambda i, j: (0, 0))],
        out_specs=[pl.BlockSpec((tm, tn), lambda i, j: (i, j)),
                   pl.BlockSpec((tm, LANES), lambda i, j: (i, 0))],
        out_shape=[jax.ShapeDtypeStruct((t, n), BF16), jax.ShapeDtypeStruct((t, LANES), F32)],
        compiler_params=_cparams(("parallel", "arbitrary")),
        name="in_proj",
    )(xb, w_main, w_small)


def _softplus(x):
    return jnp.maximum(x, 0.0) + jnp.log1p(jnp.exp(-jnp.abs(x)))


def _soft_cap(x):
    return GATE_CAP * jnp.tanh(x / GATE_CAP)


def _log_sigmoid(x):
    return jnp.minimum(x, 0.0) - jnp.log1p(jnp.exp(-jnp.abs(x)))


def _group_constants():
    r = jnp.arange(GROUP, dtype=jnp.int32)[:, None]
    c = jnp.arange(GROUP, dtype=jnp.int32)[None, :]
    same = (r // CHUNK) == (c // CHUNK)
    neg = jnp.where(same & (r >= c), 0.0, -jnp.inf).astype(F32)
    strict = (same & (r > c)).astype(F32)
    eye = (r == c).astype(F32)
    upper = (same & (r <= c)).astype(BF16)
    last = (r == (c // CHUNK) * CHUNK + CHUNK - 1).astype(BF16)
    return jnp.stack([neg, strict, eye]), jnp.stack([upper, last, eye.astype(BF16)])


def _split3(x):
    hi = x.astype(BF16)
    r1 = x - hi.astype(F32)
    mid = r1.astype(BF16)
    return hi, mid, (r1 - mid.astype(F32)).astype(BF16)


def _rows_times(rows8, mat_b, terms):
    rows = jnp.concatenate([rows8, jnp.zeros_like(rows8)], axis=0)
    parts = _split3(rows)[:terms]
    out = _dot(parts[0], mat_b)
    for p in parts[1:]:
        out = out + _dot(p, mat_b)
    return out[0:8]


def _rows_to_cols(stack, eye_b):
    parts = _split3(stack)
    out = _dot_nt(eye_b, parts[0])
    for p in parts[1:]:
        out = out + _dot_nt(eye_b, p)
    return out


def _conv_silu(x_ref, w, e_ref):
    lb = x_ref.shape[0]
    e_ref[8:, :] = x_ref[...].astype(F32)
    y = w[0:1, :] * e_ref[pl.ds(8 - CONV_WIDTH + 1, lb), :]
    for j in range(1, CONV_WIDTH):
        y = y + w[j:j + 1, :] * e_ref[pl.ds(8 - CONV_WIDTH + 1 + j, lb), :]
    e_ref[0:8, :] = e_ref[lb:lb + 8, :]
    return y * _sigmoid(y)


def _recurrent_body(gq_ref, gk_ref, gv_ref, gz_ref, cw_ref, mq_ref, mk_ref, mv_ref, mo_ref, smt_ref,
                    gpg_ref, gpm_ref, ngg_ref, ngm_ref, fm_ref, bm_ref, yg_ref, ym_ref,
                    state_ref, eq_ref, ek_ref, ev_ref, c_ref, n_ref, m_ref):
    lb = gq_ref.shape[0]
    n_chunks = lb // CHUNK
    hds = range(HEADS)
    lanes = [slice(hd * HEAD_DIM, (hd + 1) * HEAD_DIM) for hd in hds]

    @pl.when(pl.program_id(1) == 0)
    def _():
        state_ref[...] = jnp.zeros_like(state_ref)
        for e_ref in (eq_ref, ek_ref, ev_ref):
            e_ref[0:8, :] = jnp.zeros((8, QK_WIDTH), F32)
        c_ref[...] = jnp.zeros_like(c_ref)
        n_ref[...] = jnp.zeros_like(n_ref)
        m_ref[...] = jnp.zeros_like(m_ref)

    neg, strict01, eye = fm_ref[0], fm_ref[1], fm_ref[2]
    up_b, last_b, eye_b = bm_ref[0], bm_ref[1], bm_ref[2]
    cw = cw_ref[...]

    mq_all = mq_ref[...]
    mk_all = mk_ref[...].astype(F32) * (HEAD_DIM ** -0.5)
    mq = [mq_all[:, lanes[hd]] for hd in hds]
    mk = [mk_all[:, lanes[hd]] for hd in hds]
    m_qk = [_dot_nt(mq[hd], mk[hd].astype(BF16)) for hd in hds]
    q_all = _conv_silu(gq_ref, cw[:, 0:QK_WIDTH], eq_ref)
    capped = _soft_cap(smt_ref[2 * HEADS:4 * HEADS, :] + gpm_ref[...][0:8, 0:1])
    i8 = capped
    f8 = pltpu.roll(_log_sigmoid(capped), HEADS, axis=0)
    bc8 = _rows_times(f8, up_b, 2)
    k_all = _conv_silu(gk_ref, cw[:, QK_WIDTH:2 * QK_WIDTH], ek_ref)
    bl8 = _rows_times(bc8, last_b, 3)
    lkw8 = bl8 - bc8 + i8
    v_all = _conv_silu(gv_ref, cw[:, 2 * QK_WIDTH:3 * QK_WIDTH], ev_ref)
    chunk_id = lax.broadcasted_iota(jnp.int32, (8, lb), 1) // CHUNK
    m_st = m_ref[...][:, 0:1]
    m_prev_row = jnp.zeros((8, lb), F32)
    m_next_row = jnp.zeros((8, lb), F32)
    carry_decay = []
    for c in range(n_chunks):
        in_c = chunk_id == c
        b_last = bl8[:, c * CHUNK:c * CHUNK + 1]
        m_new = jnp.maximum(b_last + m_st, jnp.max(jnp.where(in_c, lkw8, -jnp.inf), axis=1, keepdims=True))
        carry_decay.append(jnp.exp(b_last + m_st - m_new))
        m_prev_row = jnp.where(in_c, m_st, m_prev_row)
        m_next_row = jnp.where(in_c, m_new, m_next_row)
        m_st = m_new
    m_ref[...] = jnp.broadcast_to(m_st, m_ref.shape)
    gp = gpg_ref[...]
    g8 = -jnp.exp(gp[0:8, 0:1]) * _softplus(smt_ref[0:8, :] + gp[8:16, 0:1])
    beta8 = _sigmoid(smt_ref[HEADS:HEADS + 8, :])
    gc8 = _rows_times(g8, up_b, 2)
    m_stack = jnp.concatenate([bc8, bc8 + m_prev_row, jnp.exp(lkw8 - m_next_row),
                               jnp.zeros((LANES - 24, lb), F32)], axis=0)
    m_cols = _rows_to_cols(m_stack, eye_b)
    gl8 = _rows_times(gc8, last_b, 3)
    egc8 = jnp.exp(gc8)
    tail8 = jnp.exp(gl8 - gc8)
    elast8 = jnp.exp(gl8)
    g_stack = jnp.concatenate([gc8, beta8, egc8, tail8, beta8 * egc8,
                               jnp.zeros((LANES - 40, lb), F32)], axis=0)
    g_cols = _rows_to_cols(g_stack, eye_b)
    mcol = lambda j, hd: m_cols[:, 8 * j + hd:8 * j + hd + 1]
    gcol = lambda j, hd: g_cols[:, 8 * j + hd:8 * j + hd + 1]

    qg = [q_all[:, lanes[hd]] for hd in hds]
    kg = [k_all[:, lanes[hd]] for hd in hds]
    qg = [x * lax.rsqrt(jnp.sum(x * x, axis=-1, keepdims=True) + RMS_EPS) * (HEAD_DIM ** -0.5) for x in qg]
    kg = [x * lax.rsqrt(jnp.sum(x * x, axis=-1, keepdims=True) + RMS_EPS) for x in kg]
    kg_b = [x.astype(BF16) for x in kg]
    mv_all = mv_ref[...]
    mv = [mv_all[:, lanes[hd]] for hd in hds]
    log_d = [mcol(0, hd) - bc8[hd:hd + 1, :] + i8[hd:hd + 1, :] + neg for hd in hds]
    m_t = [jnp.maximum(mcol(1, hd), jnp.max(log_d[hd], axis=-1, keepdims=True)) for hd in hds]
    decay = [jnp.exp(gcol(0, hd) - gc8[hd:hd + 1, :] + neg) for hd in hds]
    a_mat = [_dot_nt((kg[hd] * gcol(1, hd)).astype(BF16), kg_b[hd]) * decay[hd] * strict01 for hd in hds]
    w_inter = [jnp.exp(mcol(1, hd) - m_t[hd]) for hd in hds]
    s = [m_qk[hd] * jnp.exp(log_d[hd] - m_t[hd]) for hd in hds]
    pw = [-a for a in a_mat]
    t_mat = [eye + p for p in pw]
    ones = jnp.ones((lb, HEAD_DIM), BF16)
    s_vx = None
    floor = kw = kw_b = None
    for it in range(5):
        pw_b = [p.astype(BF16) for p in pw]
        pw = [_dot(p, p) for p in pw_b]
        t_mat = [t_mat[hd] + _dot(t_mat[hd].astype(BF16), pw[hd].astype(BF16)) for hd in hds]
        if it == 0:
            s_vx = [_dot(s[hd].astype(BF16), jnp.concatenate([mv[hd], ones], axis=1)) for hd in hds]
        if it == 1:
            floor = [jnp.exp(-m_t[hd]) for hd in hds]
            kw = [mk[hd] * mcol(2, hd) for hd in hds]
            kw_b = [x.astype(BF16) for x in kw]
    s_v = [x[:, :HEAD_DIM] for x in s_vx]
    s_sum = [x[:, HEAD_DIM:HEAD_DIM + 1] for x in s_vx]
    rhs = [jnp.concatenate([(v_all[:, lanes[hd]] * gcol(1, hd)).astype(BF16),
                            (kg[hd] * gcol(4, hd)).astype(BF16)], axis=1) for hd in hds]
    uw = [_dot(t_mat[hd].astype(BF16), rhs[hd]).astype(BF16) for hd in hds]
    qk = [(_dot_nt(qg[hd].astype(BF16), kg_b[hd]) * decay[hd]).astype(BF16) for hd in hds]
    qk_uw = [_dot(qk[hd], uw[hd]) for hd in hds]
    q_eff = [qg[hd] * gcol(2, hd) - qk_uw[hd][:, HEAD_DIM:] for hd in hds]
    k_tail = [(kg[hd] * gcol(3, hd)).astype(BF16) for hd in hds]
    z_all = gz_ref[...].astype(F32)
    o_all = mo_ref[...].astype(F32)
    ngg = ngg_ref[...]
    ngm = ngm_ref[...]
    state = [state_ref[hd] for hd in hds]
    c_st = [c_ref[hd] for hd in hds]
    n_all = n_ref[...]
    n_st = [n_all[hd:hd + 1, :] for hd in hds]
    for c in range(n_chunks):
        sl = slice(c * CHUNK, (c + 1) * CHUNK)
        kt_uw = [_dot_tn(k_tail[hd][sl], uw[hd][sl]) for hd in hds]
        lhs = [jnp.concatenate([q_eff[hd][sl], kt_uw[hd][:, HEAD_DIM:]], axis=0).astype(BF16) for hd in hds]
        res = [_dot(lhs[hd], state[hd].astype(BF16)) for hd in hds]
        q_c = [_dot(mq[hd][sl], c_st[hd].astype(BF16)) for hd in hds]
        q_n = [jnp.sum(mq[hd][sl].astype(F32) * n_st[hd], axis=-1, keepdims=True) for hd in hds]
        state = [state[hd] * elast8[hd:hd + 1, c * CHUNK:c * CHUNK + 1] - res[hd][CHUNK:]
                 + kt_uw[hd][:, :HEAD_DIM] for hd in hds]
        cd = [carry_decay[c][hd:hd + 1, :] for hd in hds]
        c_st = [cd[hd] * c_st[hd] + _dot_tn(kw_b[hd][sl], mv[hd][sl]) for hd in hds]
        n_st = [cd[hd] * n_st[hd] + jnp.sum(kw[hd][sl], axis=0, keepdims=True) for hd in hds]
        for hd in hds:
            out = res[hd][:CHUNK] + qk_uw[hd][sl, :HEAD_DIM]
            zc = z_all[sl, lanes[hd]]
            yg_ref[sl, lanes[hd]] = (_rms_norm(out, ngg) * (zc * _sigmoid(zc))).astype(yg_ref.dtype)
        for hd in hds:
            num = w_inter[hd][sl] * q_c[hd] + s_v[hd][sl]
            den = w_inter[hd][sl] * q_n[hd] + s_sum[hd][sl]
            hid = num / jnp.maximum(jnp.abs(den), floor[hd][sl])
            ym_ref[sl, lanes[hd]] = (_sigmoid(o_all[sl, lanes[hd]])
                                     * _rms_norm(hid, ngm[:, lanes[hd]])).astype(ym_ref.dtype)
    for hd in hds:
        state_ref[hd] = state[hd]
        c_ref[hd] = c_st[hd]
        n_ref[hd:hd + 1, :] = n_st[hd]


def _gate_params(first, second):
    out = jnp.zeros((16, LANES), F32)
    out = out.at[0:first.shape[0], :].set(jnp.broadcast_to(first.astype(F32)[:, None], (first.shape[0], LANES)))
    if second is not None:
        out = out.at[8:8 + second.shape[0], :].set(
            jnp.broadcast_to(second.astype(F32)[:, None], (second.shape[0], LANES)))
    return out


def recurrent_mixers(h, small_t, conv_w, a_log, dt_bias, gdn_norm, gate_bias, mlstm_norm, fmasks, bmasks,
                     bsz, seq):
    t = h.shape[0]
    lb = GROUP
    nb = seq // lb
    wide = QK_WIDTH
    gdn0 = COL_GDN * LANES // wide
    mls0 = COL_MLSTM * LANES // wide

    def col(j):
        return pl.BlockSpec((lb, wide), lambda b, s: (b * nb + s, j))

    fixed2 = lambda b, s: (0, 0)
    fixed3 = lambda b, s: (0, 0, 0)
    out_spec = pl.BlockSpec((lb, wide), lambda b, s: (b * nb + s, 0))
    return pl.pallas_call(
        _recurrent_body,
        grid=(bsz, nb),
        in_specs=[col(gdn0), col(gdn0 + 1), col(gdn0 + 2), col(gdn0 + 3),
                  pl.BlockSpec((CONV_WIDTH, 3 * wide), fixed2),
                  col(mls0), col(mls0 + 1), col(mls0 + 2), col(mls0 + 3),
                  pl.BlockSpec((4 * HEADS, lb), lambda b, s: (0, b * nb + s)),
                  pl.BlockSpec((16, LANES), fixed2),
                  pl.BlockSpec((16, LANES), fixed2),
                  pl.BlockSpec((1, HEAD_DIM), fixed2),
                  pl.BlockSpec((1, wide), fixed2),
                  pl.BlockSpec((3, GROUP, GROUP), fixed3),
                  pl.BlockSpec((3, GROUP, GROUP), fixed3)],
        out_specs=[out_spec, out_spec],
        out_shape=[jax.ShapeDtypeStruct((t, wide), BF16), jax.ShapeDtypeStruct((t, wide), BF16)],
        scratch_shapes=[pltpu.VMEM((HEADS, HEAD_DIM, HEAD_DIM), F32),
                        pltpu.VMEM((lb + 8, wide), F32),
                        pltpu.VMEM((lb + 8, wide), F32),
                        pltpu.VMEM((lb + 8, wide), F32),
                        pltpu.VMEM((HEADS, HEAD_DIM, HEAD_DIM), F32),
                        pltpu.VMEM((8, HEAD_DIM), F32),
                        pltpu.VMEM((8, LANES), F32)],
        compiler_params=_cparams(("parallel", "arbitrary")),
        name="recurrent_mixers",
    )(h, h, h, h, conv_w, h, h, h, h, small_t, _gate_params(a_log, dt_bias), _gate_params(gate_bias, None),
      gdn_norm.reshape(1, HEAD_DIM), mlstm_norm.reshape(1, wide), fmasks, bmasks)


def _rope_table_body(pos_ref, freq_ref, sign_ref, cc_ref, ss_ref):
    ang = pos_ref[...] * freq_ref[...]
    sign = sign_ref[...]
    cc_ref[...] = jnp.cos(ang) * jnp.abs(sign)
    ss_ref[...] = jnp.sin(ang) * sign


def rope_tables(positions, tm=512):
    t = positions.size
    half = MLA_ROPE // 2
    inv_freq = 1.0 / (ROPE_THETA ** (jnp.arange(0, MLA_ROPE, 2, dtype=F32) / MLA_ROPE))
    zeros = jnp.zeros((LANES - MLA_ROPE,), F32)
    freq = jnp.concatenate([inv_freq, inv_freq, zeros]).reshape(1, LANES)
    sign = jnp.concatenate([-jnp.ones((half,), F32), jnp.ones((half,), F32), zeros]).reshape(1, LANES)
    return pl.pallas_call(
        _rope_table_body,
        grid=(t // tm,),
        in_specs=[pl.BlockSpec((tm, 1), lambda i: (i, 0)),
                  pl.BlockSpec((1, LANES), lambda i: (0, 0)),
                  pl.BlockSpec((1, LANES), lambda i: (0, 0))],
        out_specs=[pl.BlockSpec((tm, LANES), lambda i: (i, 0)),
                   pl.BlockSpec((tm, LANES), lambda i: (i, 0))],
        out_shape=[jax.ShapeDtypeStruct((t, LANES), F32), jax.ShapeDtypeStruct((t, LANES), F32)],
        compiler_params=_cparams(("parallel",)),
        name="rope_tables",
    )(positions.astype(F32).reshape(t, 1), freq, sign)


def _mla_pre_body(h_ref, cc_ref, ss_ref, qg_ref, kvg_ref, wqa_ref, wqb_ref, wkv_ref,
                  q_ref, kn_ref, kr_ref, v_ref):
    hblk = h_ref[...].astype(F32)
    cc = cc_ref[...]
    ss = ss_ref[...]
    cq = _rms_norm(hblk[:, :MLA_Q_LORA], qg_ref[...]).astype(BF16)
    ckv = _rms_norm(hblk[:, MLA_Q_LORA:MLA_Q_LORA + MLA_KV_LORA], kvg_ref[...]).astype(BF16)
    off = MLA_Q_LORA + MLA_KV_LORA
    kr_ref[...] = (hblk[:, off:off + LANES] * cc + hblk[:, off + LANES:off + 2 * LANES] * ss).astype(BF16)
    kv = _dot(ckv, wkv_ref[...])
    kn_ref[...] = kv[:, :QK_WIDTH].astype(BF16)
    v_ref[...] = kv[:, QK_WIDTH:].astype(BF16)
    qa = _dot(cq, wqa_ref[...])
    qb = _dot(cq, wqb_ref[...])
    scale = (HEAD_DIM + MLA_ROPE) ** -0.5 * LOG2_E
    for hh in range(HEADS):
        base = 2 * HEAD_DIM * hh
        q_ref[:, base:base + HEAD_DIM] = (qa[:, base:base + HEAD_DIM] * scale).astype(BF16)
        rope = qa[:, base + HEAD_DIM:base + 2 * HEAD_DIM] * cc + qb[:, hh * LANES:(hh + 1) * LANES] * ss
        q_ref[:, base + HEAD_DIM:base + 2 * HEAD_DIM] = (rope * scale).astype(BF16)


def mla_prepare(h, cc, ss, q_norm_g, kv_norm_g, wq_a, wq_b, wkv, tm=512):
    t = h.shape[0]
    row = lambda i: (i, 0)
    fixed = lambda i: (0, 0)
    return pl.pallas_call(
        _mla_pre_body,
        grid=(t // tm,),
        in_specs=[pl.BlockSpec((tm, 1024), row),
                  pl.BlockSpec((tm, LANES), row),
                  pl.BlockSpec((tm, LANES), row),
                  pl.BlockSpec((1, MLA_Q_LORA), fixed),
                  pl.BlockSpec((1, MLA_KV_LORA), fixed),
                  pl.BlockSpec(wq_a.shape, fixed),
                  pl.BlockSpec(wq_b.shape, fixed),
                  pl.BlockSpec(wkv.shape, fixed)],
        out_specs=[pl.BlockSpec((tm, 2 * QK_WIDTH), row),
                   pl.BlockSpec((tm, QK_WIDTH), row),
                   pl.BlockSpec((tm, LANES), row),
                   pl.BlockSpec((tm, QK_WIDTH), row)],
        out_shape=[jax.ShapeDtypeStruct((t, 2 * QK_WIDTH), BF16),
                   jax.ShapeDtypeStruct((t, QK_WIDTH), BF16),
                   jax.ShapeDtypeStruct((t, LANES), BF16),
                   jax.ShapeDtypeStruct((t, QK_WIDTH), BF16)],
        compiler_params=_cparams(("parallel",)),
        name="mla_prepare",
    )(h, cc, ss, q_norm_g.reshape(1, -1), kv_norm_g.reshape(1, -1), wq_a, wq_b, wkv)


def _attn_body(q_ref, kn_ref, kr_ref, v_ref, o_ref, *, tk, n_heads):
    qi = pl.program_id(2)
    tq = q_ref.shape[0]
    hds = range(n_heads)
    lanes = [slice(hd * HEAD_DIM, (hd + 1) * HEAD_DIM) for hd in hds]
    q = [q_ref[:, 2 * hd * HEAD_DIM:2 * (hd + 1) * HEAD_DIM] for hd in hds]

    def step(j, carry, masked):
        m, l, acc = carry
        start = pl.multiple_of(j * tk, tk)
        kr = kr_ref[pl.ds(start, tk), :]
        s = [_dot_nt(q[hd], jnp.concatenate([kn_ref[pl.ds(start, tk), lanes[hd]], kr], axis=1)) for hd in hds]
        if masked:
            r = lax.broadcasted_iota(jnp.int32, (tq, tk), 0)
            c = lax.broadcasted_iota(jnp.int32, (tq, tk), 1)
            s = [jnp.where(r >= c, x, -jnp.inf) for x in s]
        m_new = [jnp.maximum(m[hd], jnp.max(s[hd], axis=-1, keepdims=True)) for hd in hds]
        alpha = [jnp.exp2(m[hd] - m_new[hd]) for hd in hds]
        p = [jnp.exp2(s[hd] - m_new[hd]) for hd in hds]
        l_new = [alpha[hd] * l[hd] + jnp.sum(p[hd], axis=-1, keepdims=True) for hd in hds]
        p_b = [x.astype(BF16) for x in p]
        acc = [alpha[hd] * acc[hd] + _dot(p_b[hd], v_ref[pl.ds(start, tk), lanes[hd]]) for hd in hds]
        return tuple(m_new), tuple(l_new), tuple(acc)

    init = (tuple(jnp.full((tq, 1), -jnp.inf, F32) for _ in hds),
            tuple(jnp.zeros((tq, 1), F32) for _ in hds),
            tuple(jnp.zeros((tq, HEAD_DIM), F32) for _ in hds))
    carry = lax.fori_loop(0, qi, lambda j, cr: step(j, cr, False), init)
    _, l, acc = step(qi, carry, True)
    for hd in hds:
        o_ref[:, lanes[hd]] = (acc[hd] / l[hd]).astype(o_ref.dtype)


def latent_attention(q, kn, kr, v, bsz, seq, tq=512, n_heads=2):
    t = q.shape[0]
    nq = seq // tq
    wide = n_heads * HEAD_DIM
    return pl.pallas_call(
        functools.partial(_attn_body, tk=tq, n_heads=n_heads),
        grid=(bsz, HEADS // n_heads, nq),
        in_specs=[pl.BlockSpec((tq, 2 * wide), lambda b, hh, i: (b * nq + i, hh)),
                  pl.BlockSpec((seq, wide), lambda b, hh, i: (b, hh)),
                  pl.BlockSpec((seq, LANES), lambda b, hh, i: (b, 0)),
                  pl.BlockSpec((seq, wide), lambda b, hh, i: (b, hh))],
        out_specs=pl.BlockSpec((tq, wide), lambda b, hh, i: (b * nq + i, hh)),
        out_shape=jax.ShapeDtypeStruct((t, QK_WIDTH), BF16),
        compiler_params=_cparams(("parallel", "parallel", "arbitrary")),
        name="latent_attention",
    )(q, kn, kr, v)


def _top2_sum(a, b, c, d):
    hi1, lo1 = jnp.maximum(a, b), jnp.minimum(a, b)
    hi2, lo2 = jnp.maximum(c, d), jnp.minimum(c, d)
    return jnp.maximum(hi1, hi2) + jnp.maximum(jnp.minimum(hi1, hi2), jnp.maximum(lo1, lo2))


def _merge_route_body(yg_ref, ym_ref, ya_ref, g0_ref, g1_ref, g2_ref, gb_ref, pg_ref, pm_ref, pa_ref,
                      wo_ref, x_ref, ln_g_ref, ln_b_ref, rw_ref, rb_ref, su_ref, eye_ref,
                      xw_ref, grp_ref, rank_ref, cnt_ref, carry_ref, yprev_ref):
    step = pl.program_id(0)

    @pl.when(step == 0)
    def _():
        carry_ref[...] = jnp.zeros_like(carry_ref)
        yprev_ref[...] = jnp.zeros_like(yprev_ref)

    gb = gb_ref[...]

    def branch(y_ref, p_ref, g_ref, idx):
        gate = _sigmoid(g_ref[...].astype(F32) + gb[:, idx * D_MODEL:(idx + 1) * D_MODEL])
        return gate * _dot(y_ref[...], p_ref[...])

    y = yprev_ref[...]
    xw_ref[:, 0:D_MODEL] = y
    y_hi = y.astype(BF16)
    y_lo = (y - y_hi.astype(F32)).astype(BF16)
    rw_hi, rw_lo = rw_ref[0], rw_ref[1]
    logits_tok = _dot(y_hi, rw_hi) + (_dot(y_lo, rw_hi) + _dot(y_hi, rw_lo))
    merged = branch(yg_ref, pg_ref, g0_ref, 0)
    logits = logits_tok.T[0:N_EXPERTS, :]
    scores = _sigmoid(logits)
    biased = scores + rb_ref[...][:, :1]
    tm = logits.shape[1]
    brow = [biased[e:e + 1, :] for e in range(N_EXPERTS)]
    srow = [scores[e:e + 1, :] for e in range(N_EXPERTS)]
    best = _top2_sum(*brow[0:EXPERTS_PER_GROUP])
    grp = jnp.zeros((1, tm), jnp.int32)
    for g in range(1, N_GROUPS):
        gs = _top2_sum(*brow[g * EXPERTS_PER_GROUP:(g + 1) * EXPERTS_PER_GROUP])
        upd = gs > best
        best = jnp.where(upd, gs, best)
        grp = jnp.where(upd, g, grp)

    def in_group(rows, j):
        out = rows[j]
        for g in range(1, N_GROUPS):
            out = jnp.where(grp == g, rows[g * EXPERTS_PER_GROUP + j], out)
        return out

    ib = [in_group(brow, j) for j in range(EXPERTS_PER_GROUP)]
    isc = [in_group(srow, j) for j in range(EXPERTS_PER_GROUP)]
    v1, i1, s1 = ib[0], jnp.zeros((1, tm), jnp.int32), isc[0]
    for j in range(1, EXPERTS_PER_GROUP):
        upd = ib[j] > v1
        v1 = jnp.where(upd, ib[j], v1)
        i1 = jnp.where(upd, j, i1)
        s1 = jnp.where(upd, isc[j], s1)
    v2 = jnp.full((1, tm), -jnp.inf, F32)
    i2 = jnp.zeros((1, tm), jnp.int32)
    s2 = jnp.zeros((1, tm), F32)
    for j in range(EXPERTS_PER_GROUP):
        upd = jnp.logical_and(i1 != j, ib[j] > v2)
        v2 = jnp.where(upd, ib[j], v2)
        i2 = jnp.where(upd, j, i2)
        s2 = jnp.where(upd, isc[j], s2)
    total = s1 + s2
    merged = merged + branch(ym_ref, pm_ref, g1_ref, 1)
    wrow = [jnp.where(i1 == j, s1 / total, jnp.where(i2 == j, s2 / total, 0.0)) for j in range(EXPERTS_PER_GROUP)]
    row_id = lax.broadcasted_iota(jnp.int32, (8, tm), 0)
    w8 = jnp.zeros((8, tm), F32)
    for j in range(EXPERTS_PER_GROUP):
        w8 = jnp.where(row_id == j, wrow[j], w8)
    merged = merged + branch(ya_ref, pa_ref, g2_ref, 2)
    stack = jnp.concatenate([w8, jnp.zeros((LANES - 8, tm), F32)], axis=0)
    xw_ref[:, D_MODEL:D_MODEL + LANES] = _rows_to_cols(stack, eye_ref[...])
    mix = _dot(merged.astype(BF16), wo_ref[...])
    grp_ref[...] = grp
    grow = lax.broadcasted_iota(jnp.int32, (N_EXPERTS, tm), 0)
    onehot = jnp.where(grow == grp, 1.0, 0.0).astype(F32)
    before = _dot(onehot.astype(BF16), su_ref[...]) + carry_ref[...][:, :1]
    rank_ref[...] = jnp.sum(onehot * before, axis=0, keepdims=True).astype(jnp.int32)
    counted = jnp.where(step > 0, 1.0, 0.0)
    carry = carry_ref[...] + counted * jnp.sum(onehot, axis=1, keepdims=True)
    carry_ref[...] = carry
    cnt_ref[...] = carry.astype(jnp.int32)
    yprev_ref[...] = _layer_norm(DEEPNORM_ALPHA * x_ref[...] + mix, ln_g_ref[...], ln_b_ref[...])


def merge_and_route(y_gdn, y_mlstm, y_mla, h, gate_bias, p_gdn, p_mlstm, p_mla, w_out, x, ln_g, ln_b,
                    router_w, router_bias, tm=512):
    t, d = x.shape
    nt = t // tm
    cur = lambda i: jnp.minimum(i, nt - 1)
    prev = lambda i: jnp.maximum(i - 1, 0)
    row = lambda i: (cur(i), 0)
    fixed = lambda i: (0, 0)
    tok = lambda i: (0, prev(i))
    ybs = pl.BlockSpec((tm, QK_WIDTH), row)
    pbs = pl.BlockSpec((QK_WIDTH, d), fixed)
    strict_upper = jnp.triu(jnp.ones((tm, tm), BF16), k=1)
    eye_b = jnp.eye(tm, dtype=BF16)
    rw = jnp.pad(router_w.astype(F32), ((0, 0), (0, LANES - N_EXPERTS)))
    rw_hi = rw.astype(BF16)
    rw_split = jnp.stack([rw_hi, (rw - rw_hi.astype(F32)).astype(BF16)])
    return pl.pallas_call(
        _merge_route_body,
        grid=(nt + 1,),
        in_specs=[ybs, ybs, ybs,
                  pl.BlockSpec((tm, d), lambda i: (cur(i), COL_GATE)),
                  pl.BlockSpec((tm, d), lambda i: (cur(i), COL_GATE + 1)),
                  pl.BlockSpec((tm, d), lambda i: (cur(i), COL_GATE + 2)),
                  pl.BlockSpec((1, 3 * d), fixed),
                  pbs, pbs, pbs,
                  pl.BlockSpec((d, d), fixed),
                  pl.BlockSpec((tm, d), row),
                  pl.BlockSpec((1, d), fixed),
                  pl.BlockSpec((1, d), fixed),
                  pl.BlockSpec((2, d, LANES), lambda i: (0, 0, 0)),
                  pl.BlockSpec((N_EXPERTS, LANES), fixed),
                  pl.BlockSpec((tm, tm), fixed),
                  pl.BlockSpec((tm, tm), fixed)],
        out_specs=[pl.BlockSpec((tm, d + LANES), lambda i: (prev(i), 0)),
                   pl.BlockSpec((1, tm), tok), pl.BlockSpec((1, tm), tok),
                   pl.BlockSpec((N_EXPERTS, LANES), fixed)],
        out_shape=[jax.ShapeDtypeStruct((t, d + LANES), F32),
                   jax.ShapeDtypeStruct((1, t), jnp.int32), jax.ShapeDtypeStruct((1, t), jnp.int32),
                   jax.ShapeDtypeStruct((N_EXPERTS, LANES), jnp.int32)],
        scratch_shapes=[pltpu.VMEM((N_EXPERTS, LANES), F32), pltpu.VMEM((tm, d), F32)],
        compiler_params=_cparams(("arbitrary",)),
        name="merge_route",
    )(y_gdn, y_mlstm, y_mla, h, h, h, gate_bias.reshape(1, 3 * d), p_gdn, p_mlstm, p_mla, w_out, x,
      ln_g.reshape(1, d), ln_b.reshape(1, d), rw_split,
      jnp.broadcast_to(router_bias.reshape(N_EXPERTS, 1), (N_EXPERTS, LANES)), strict_upper, eye_b)


def _gather_copy(x_hbm, buf, sem, slot, row, tok):
    return pltpu.make_async_copy(x_hbm.at[pl.ds(tok, 1), :], buf.at[slot, pl.ds(row, 1), :], sem.at[slot])


def _scatter_copy(ybuf, y_hbm, sem, slot, row, dst):
    return pltpu.make_async_copy(ybuf.at[slot, pl.ds(row, 1), :], y_hbm.at[pl.ds(dst, 1), :], sem.at[slot])


def _group_ffn_body(bg_ref, src_ref, dst_ref, x_hbm, w1_ref, w3_ref, w2_ref, y_hbm,
                    buf, ybuf, gsem, ssem, *, spare_row):
    i = pl.program_id(0)
    n = pl.num_programs(0)

    def wait_rows(make):
        def one(r, carry):
            make(r).wait()
            return carry

        lax.fori_loop(0, MOE_BLOCK, one, 0, unroll=8)

    @pl.when(i == 0)
    def _():
        def one(r, carry):
            _gather_copy(x_hbm, buf, gsem, 0, r, src_ref[r]).start()
            return carry

        lax.fori_loop(0, MOE_BLOCK, one, 0, unroll=8)
        ybuf[1] = jnp.zeros((MOE_BLOCK, D_MODEL), F32)

    def step(slot):
        other = 1 - slot
        wait_rows(lambda r: _gather_copy(x_hbm, buf, gsem, slot, r, 0))
        nxt = jnp.minimum(i + 1, n - 1) * MOE_BLOCK
        prv = jnp.maximum(i - 1, 0) * MOE_BLOCK
        for r in range(MOE_BLOCK):
            _gather_copy(x_hbm, buf, gsem, other, r, src_ref[nxt + r]).start(priority=r % 2)
        for r in range(MOE_BLOCK):
            dst = jnp.where(i > 0, dst_ref[prv + r], spare_row + r)
            _scatter_copy(ybuf, y_hbm, ssem, other, r, dst).start(priority=r % 2)
        xw = buf[slot]
        xb = xw[:, :D_MODEL].astype(BF16)
        acts = []
        for e in range(EXPERTS_PER_GROUP):
            h1 = _dot(xb, w1_ref[e])
            h3 = _dot(xb, w3_ref[e])
            acts.append((h1 * _sigmoid(h1) * h3 * xw[:, D_MODEL + e:D_MODEL + e + 1]).astype(BF16))
        ybuf[slot] = _dot(jnp.concatenate(acts, axis=1), w2_ref[...])
        wait_rows(lambda r: _scatter_copy(ybuf, y_hbm, ssem, other, r, 0))

        @pl.when(i == n - 1)
        def _():
            wait_rows(lambda r: _gather_copy(x_hbm, buf, gsem, other, r, 0))

            def one(r, carry):
                _scatter_copy(ybuf, y_hbm, ssem, slot, r, dst_ref[i * MOE_BLOCK + r]).start()
                return carry

            lax.fori_loop(0, MOE_BLOCK, one, 0, unroll=8)
            wait_rows(lambda r: _scatter_copy(ybuf, y_hbm, ssem, slot, r, 0))

    pl.when(i % 2 == 0)(lambda: step(0))
    pl.when(i % 2 == 1)(lambda: step(1))


def group_ffn(block_group, row_src, row_dst, xw, w1g, w3g, w2g, n_out_rows, spare_row):
    d = D_MODEL
    n_rows = row_src.shape[0]
    n_blocks = n_rows // MOE_BLOCK
    wide = EXPERTS_PER_GROUP * D_EXPERT
    once = pl.Buffered(1)
    grid_spec = pltpu.PrefetchScalarGridSpec(
        num_scalar_prefetch=3,
        grid=(n_blocks,),
        in_specs=[pl.BlockSpec(memory_space=pl.ANY),
                  pl.BlockSpec((EXPERTS_PER_GROUP, d, D_EXPERT), lambda i, bg, rs, rd: (bg[i], 0, 0),
                               pipeline_mode=once),
                  pl.BlockSpec((EXPERTS_PER_GROUP, d, D_EXPERT), lambda i, bg, rs, rd: (bg[i], 0, 0),
                               pipeline_mode=once),
                  pl.BlockSpec((None, wide, d), lambda i, bg, rs, rd: (bg[i], 0, 0), pipeline_mode=once)],
        out_specs=pl.BlockSpec(memory_space=pl.ANY),
        scratch_shapes=[pltpu.VMEM((2, MOE_BLOCK, d + LANES), F32),
                        pltpu.VMEM((2, MOE_BLOCK, d), F32),
                        pltpu.SemaphoreType.DMA((2,)),
                        pltpu.SemaphoreType.DMA((2,))])
    return pl.pallas_call(
        functools.partial(_group_ffn_body, spare_row=spare_row),
        grid_spec=grid_spec,
        out_shape=jax.ShapeDtypeStruct((n_out_rows, d), F32),
        compiler_params=pltpu.CompilerParams(dimension_semantics=("arbitrary",),
                                             vmem_limit_bytes=MOE_VMEM_LIMIT),
        name="group_ffn",
    )(block_group, row_src, row_dst, xw, w1g, w3g, w2g)


def _residual_norm_body(x_ref, y_ref, ln_g_ref, ln_b_ref, of_ref, ob_ref):
    y = _layer_norm(DEEPNORM_ALPHA * x_ref[...] + y_ref[...], ln_g_ref[...], ln_b_ref[...])
    of_ref[...] = y
    ob_ref[...] = y.astype(BF16)


def residual_norm(xw, y_tok, ln_g, ln_b, tm=512):
    t = xw.shape[0]
    d = D_MODEL
    row = lambda i: (i, 0)
    fixed = lambda i: (0, 0)
    return pl.pallas_call(
        _residual_norm_body,
        grid=(t // tm,),
        in_specs=[pl.BlockSpec((tm, d), row), pl.BlockSpec((tm, d), row),
                  pl.BlockSpec((1, d), fixed), pl.BlockSpec((1, d), fixed)],
        out_specs=[pl.BlockSpec((tm, d), row), pl.BlockSpec((tm, d), row)],
        out_shape=[jax.ShapeDtypeStruct((t, d), F32), jax.ShapeDtypeStruct((t, d), BF16)],
        compiler_params=_cparams(("parallel",)),
        name="residual_norm",
    )(xw, y_tok, ln_g.reshape(1, d), ln_b.reshape(1, d))


def grouped_experts(xw, grp, rank, counts, w1, w3, w2, ln_g, ln_b):
    t = xw.shape[0]
    grp = grp.reshape(t)
    counts = counts[:N_GROUPS, 0]
    padded = (counts + MOE_BLOCK - 1) // MOE_BLOCK * MOE_BLOCK
    padded_ends = jnp.cumsum(padded)
    padded_starts = padded_ends - padded
    start_of = jnp.zeros_like(grp)
    for g in range(N_GROUPS):
        start_of = jnp.where(grp == g, padded_starts[g], start_of)
    dest = start_of + rank.reshape(t)
    n_rows = t + N_GROUPS * MOE_BLOCK
    slot_tok = jnp.full((n_rows,), -1, jnp.int32).at[dest].set(jnp.arange(t, dtype=jnp.int32))
    row_ids = jnp.arange(n_rows, dtype=jnp.int32)
    block_start = jnp.arange(n_rows // MOE_BLOCK, dtype=jnp.int32) * MOE_BLOCK
    block_group = jnp.minimum(jnp.sum(block_start[:, None] >= padded_ends[None, :], axis=1),
                              N_GROUPS - 1).astype(jnp.int32)
    pads = padded - counts
    pad_base = jnp.cumsum(pads) - pads - (padded_starts + counts)
    row_group = jnp.repeat(block_group, MOE_BLOCK)
    pad_off = jnp.zeros_like(row_ids)
    for g in range(N_GROUPS):
        pad_off = jnp.where(row_group == g, pad_base[g], pad_off)
    row_src = jnp.maximum(slot_tok, 0)
    row_dst = jnp.where(slot_tok < 0, t + row_ids + pad_off, slot_tok)
    d = D_MODEL
    wide = EXPERTS_PER_GROUP * D_EXPERT
    w2g = w2.reshape(N_GROUPS, wide, d)
    spare_row = n_rows
    y_tok = group_ffn(block_group, row_src, row_dst, xw, w1.astype(BF16), w3.astype(BF16), w2g.astype(BF16),
                      spare_row + MOE_BLOCK, spare_row)
    return residual_norm(xw, y_tok, ln_g, ln_b)


def _split_w_in(w_in):
    sizes = (QK_WIDTH, QK_WIDTH, QK_WIDTH, QK_WIDTH, HEADS, HEADS,
             QK_WIDTH, QK_WIDTH, QK_WIDTH, QK_WIDTH, HEADS, HEADS,
             MLA_Q_LORA, MLA_KV_LORA, MLA_ROPE, 3 * D_MODEL)
    parts, acc = [], 0
    for size in sizes:
        parts.append(w_in[:, acc:acc + size])
        acc += size
    return parts


def _arrange_w_in(w_in):
    (g_q, g_k, g_v, g_z, g_a, g_b, m_q, m_k, m_v, m_o, m_i, m_f, c_q, c_kv, k_rope, gates) = _split_w_in(w_in)
    d = w_in.shape[0]
    half = MLA_ROPE // 2
    pad64 = jnp.zeros((d, LANES - MLA_ROPE), w_in.dtype)
    rope_sw = jnp.concatenate([k_rope[:, half:], k_rope[:, :half]], axis=1)
    main = jnp.concatenate([c_q, c_kv, k_rope, pad64, rope_sw, pad64, jnp.zeros((d, LANES), w_in.dtype),
                            g_q, g_k, g_v, g_z, m_q, m_k, m_v, m_o, gates], axis=1)
    small = jnp.concatenate([g_a, g_b, m_i, m_f, jnp.zeros((d, LANES - 4 * HEADS), w_in.dtype)], axis=1)
    return main.astype(BF16), small.astype(BF16)


def _arrange_mla(w_uq, w_ukv):
    half = MLA_ROPE // 2
    wq = w_uq.reshape(MLA_Q_LORA, HEADS, HEAD_DIM + MLA_ROPE)
    nope, rope = wq[:, :, :HEAD_DIM], wq[:, :, HEAD_DIM:]
    pad = jnp.zeros((MLA_Q_LORA, HEADS, LANES - MLA_ROPE), w_uq.dtype)
    wq_a = jnp.concatenate([nope, rope, pad], axis=2).reshape(MLA_Q_LORA, HEADS * 2 * HEAD_DIM)
    rope_sw = jnp.concatenate([rope[:, :, half:], rope[:, :, :half]], axis=2)
    wq_b = jnp.concatenate([rope_sw, pad], axis=2).reshape(MLA_Q_LORA, HEADS * LANES)
    wkv = w_ukv.reshape(MLA_KV_LORA, HEADS, 2 * HEAD_DIM)
    wkv = jnp.concatenate([wkv[:, :, :HEAD_DIM].reshape(MLA_KV_LORA, QK_WIDTH),
                           wkv[:, :, HEAD_DIM:].reshape(MLA_KV_LORA, QK_WIDTH)], axis=1)
    return wq_a.astype(BF16), wq_b.astype(BF16), wkv.astype(BF16)


def kernel(x, positions, ln_in_g, ln_in_b, w_in, gdn_conv, gdn_a_log, gdn_dt_bias, gdn_norm, mlstm_gate_bias, mlstm_norm, mla_q_norm, mla_kv_norm, mla_w_uq, mla_w_ukv, w_br_gdn, w_br_mlstm, w_br_mla, gate_bias, w_out, ln1_g, ln1_b, router_w, router_bias, moe_w1, moe_w3, moe_w2, ln2_g, ln2_b):
    bsz, seq, d = x.shape
    t = bsz * seq
    xf, xb = layer_norm_entry(x.reshape(t, d), ln_in_g, ln_in_b)
    cc, ss = rope_tables(positions)
    fmasks, bmasks = _group_constants()
    for l in range(DEPTH):
        w_main, w_small = _arrange_w_in(w_in[l])
        h, small = in_projection(xb, w_main, w_small)
        small_t = small[:, :4 * HEADS].T
        y_gdn, y_mlstm = recurrent_mixers(h, small_t, gdn_conv[l], gdn_a_log[l], gdn_dt_bias[l], gdn_norm[l],
                                          mlstm_gate_bias[l], mlstm_norm[l], fmasks, bmasks, bsz, seq)
        wq_a, wq_b, wkv = _arrange_mla(mla_w_uq[l], mla_w_ukv[l])
        q, kn, kr, v = mla_prepare(h, cc, ss, mla_q_norm[l], mla_kv_norm[l], wq_a, wq_b, wkv)
        y_mla = latent_attention(q, kn, kr, v, bsz, seq)
        xw, grp, rank, counts = merge_and_route(
            y_gdn, y_mlstm, y_mla, h, gate_bias[l], w_br_gdn[l].astype(BF16), w_br_mlstm[l].astype(BF16),
            w_br_mla[l].astype(BF16), w_out[l].astype(BF16), xf, ln1_g[l], ln1_b[l], router_w, router_bias)
        xf, xb = grouped_experts(xw, grp, rank, counts, moe_w1[l], moe_w3[l], moe_w2[l], ln2_g[l], ln2_b[l])
    return xf.reshape(bsz, seq, d)
```
